```python
import jax
import jax.numpy as jnp
from jax import lax
import numpy as np

D_MODEL = 2048
BATCH = 8
SEQ = 2048
DEPTH = 1
DEC_BATCH = 32
DEC_SEQ = 8
PAST_LEN = 16384
PAGE_SIZE = 128

R_HEAD_DIM = 64
R_WIDTH = D_MODEL // 2
R_HEADS = R_WIDTH // R_HEAD_DIM
DECAY_LORA = max(32, int(round(1.8 * D_MODEL ** 0.5 / 32)) * 32)
ICLR_LORA = DECAY_LORA
GATE_LORA = max(32, int(round(0.6 * D_MODEL ** 0.8 / 32)) * 32)
DECAY_SCALE = 0.606531
GN_EPS = R_HEAD_DIM * 1e-5
S_HEAD_DIM = 64
S_WIDTH = D_MODEL // 2
S_HEADS = S_WIDTH // S_HEAD_DIM
Q_BLOCK = 128
SB_BIAS_INIT = 8.0
R_COLS = 3 * R_WIDTH + DECAY_LORA + ICLR_LORA + GATE_LORA
S_COLS = 3 * S_WIDTH
GATE_COLS = 2 * D_MODEL
N_IN = R_COLS + S_COLS + GATE_COLS
D_FF = 256 * ((8 * D_MODEL // 3 + 255) // 256)
CONV_W = 3
PLE_DIM = 256
NORM_EPS = 1e-6

kernel_name = 'rwkv7_stickbreak_parallel_hybrid_step'


def rms_norm(x, g):
    xf = x.astype(jnp.float32)
    y = xf * lax.rsqrt(jnp.mean(xf * xf, axis=-1, keepdims=True) + NORM_EPS)
    return (y * g.astype(jnp.float32)).astype(x.dtype)


def head_group_norm(y, g, b):
    B, T, H, N = y.shape
    mu = jnp.mean(y, axis=-1, keepdims=True)
    yc = y - mu
    yn = yc * lax.rsqrt(jnp.mean(yc * yc, axis=-1, keepdims=True) + GN_EPS)
    return yn.reshape(B, T, H * N) * g.astype(jnp.float32) + b.astype(jnp.float32)


def wkv_recurrence(r, decay, k, v, kk, iclr, s0):
    def step(s, inp):
        r_t, w_t, k_t, v_t, kk_t, a_t = inp
        s_kk = jnp.einsum('bhvk,bhk->bhv', s, kk_t)
        s = (s * w_t[:, :, None, :]
             - s_kk[..., None] * (kk_t * a_t)[:, :, None, :]
             + v_t[..., None] * k_t[:, :, None, :])
        return s, jnp.einsum('bhvk,bhk->bhv', s, r_t)
    xs = tuple(jnp.moveaxis(t, 1, 0) for t in (r, decay, k, v, kk, iclr))
    s_final, ys = lax.scan(step, s0.astype(jnp.float32), xs)
    return jnp.moveaxis(ys, 0, 1), s_final


def stick_breaking_attention(q, k_past, v_past, k_new, v_new, bias):
    B, T, H, Dh = q.shape
    P = k_past.shape[1]
    qb = Q_BLOCK if T % Q_BLOCK == 0 else T
    nb = T // qb
    scale = Dh ** -0.5
    key_pos = jnp.arange(P + T)
    bias_f = bias.astype(jnp.float32)[None, :, None, None]
    q_blocks = jnp.moveaxis(q.reshape(B, nb, qb, H, Dh), 1, 0)

    def one_block(args):
        q_blk, blk = args
        q_pos = P + blk * qb + jnp.arange(qb)
        z = jnp.concatenate([
            jnp.einsum('bqhd,bkhd->bhqk', q_blk, k_past, preferred_element_type=jnp.float32),
            jnp.einsum('bqhd,bkhd->bhqk', q_blk, k_new, preferred_element_type=jnp.float32)],
            axis=-1) * scale + bias_f
        causal = key_pos[None, :] < q_pos[:, None]
        log_keep = jnp.where(causal, jax.nn.log_sigmoid(-z), 0.0)
        log_after = lax.cumsum(log_keep, axis=3, reverse=True) - log_keep
        att = jnp.where(causal, jnp.exp(jax.nn.log_sigmoid(z) + log_after), 0.0)
        return (jnp.einsum('bhqk,bkhd->bqhd', att[..., :P].astype(v_past.dtype), v_past)
                + jnp.einsum('bhqk,bkhd->bqhd', att[..., P:].astype(v_new.dtype), v_new))

    out = lax.map(one_block, (q_blocks, jnp.arange(nb)))
    return jnp.moveaxis(out, 0, 1).reshape(B, T, H, Dh)


def hybrid_layer(x, pe, shift_prev, wkv_prev, conv_prev, k_past, v_past,
                 g_mix, w_in, sb_bias, mu_shift, w0, w2, a0, a2, g2, k_k, k_a, r_k, ln_x_g, ln_x_b,
                 w_br_r, w_br_s, w_o, g_ffn, w_up, conv_w, w_down, g_ple, w_ple, w_pg):
    B, T, _ = x.shape
    dt = x.dtype
    f32 = jnp.float32
    xn = rms_norm(x, g_mix)
    proj = xn @ w_in
    p_r, p_s, p_g = jnp.split(proj, [R_COLS, R_COLS + S_COLS], axis=-1)

    prev_rows = jnp.concatenate([shift_prev[:, None, :].astype(dt), p_r[:, :-1]], axis=1)
    p_mix = p_r + (prev_rows - p_r) * mu_shift
    c1, c2, c3 = R_WIDTH, 2 * R_WIDTH, 3 * R_WIDTH
    r, k, v, xw, xa, xg = jnp.split(
        p_mix, [c1, c2, c3, c3 + DECAY_LORA, c3 + DECAY_LORA + ICLR_LORA], axis=-1)
    decay = jnp.exp(-DECAY_SCALE * jax.nn.sigmoid((w0 + jnp.tanh(xw) @ w2).astype(f32)))
    iclr = jax.nn.sigmoid((a0 + xa @ a2).astype(f32))
    gate = jax.nn.sigmoid(xg) @ g2

    def heads(t):
        return t.reshape(B, T, R_HEADS, R_HEAD_DIM)

    kk = heads((k * k_k).astype(f32))
    kk = kk * lax.rsqrt(jnp.maximum(jnp.sum(kk * kk, axis=-1, keepdims=True), 1e-24))
    k_mod = k.astype(f32) * (1.0 + (iclr - 1.0) * k_a)
    rh, kh, vh = heads(r.astype(f32)), heads(k_mod), heads(v.astype(f32))
    y_wkv, wkv_new = wkv_recurrence(rh, heads(decay), kh, vh, kk, heads(iclr), wkv_prev)
    bonus = jnp.sum(rh * kh * r_k, axis=-1, keepdims=True) * vh
    y_r = (head_group_norm(y_wkv, ln_x_g, ln_x_b) + bonus.reshape(B, T, R_WIDTH)).astype(dt) * gate

    q_s, k_s, v_s = (t.reshape(B, T, S_HEADS, S_HEAD_DIM) for t in jnp.split(p_s, 3, axis=-1))
    o_s = stick_breaking_attention(q_s, k_past, v_past, k_s, v_s, sb_bias).astype(dt).reshape(B, T, S_WIDTH)

    gate_r, gate_s = jnp.split(p_g, 2, axis=-1)
    mixed = jax.nn.sigmoid(gate_r) * (y_r @ w_br_r) + jax.nn.sigmoid(gate_s) * (o_s @ w_br_s)
    h = x + mixed @ w_o

    hn = rms_norm(h, g_ffn)
    u_gate, u_val = jnp.split(hn @ w_up, 2, axis=-1)
    buf = jnp.concatenate([conv_prev.astype(dt), u_gate], axis=1)
    conv = sum(conv_w[j] * buf[:, j:j + T] for j in range(CONV_W))
    h = h + (jax.nn.gelu(conv) * u_val) @ w_down
    conv_new = buf[:, T:]

    hn = rms_norm(h, g_ple)
    h = h + (pe.astype(dt) @ w_ple) * jax.nn.sigmoid(hn @ w_pg)
    return h, p_r[:, -1], wkv_new, conv_new, k_s, v_s


def setup_inputs(seed: int = 0) -> dict:
    key = jax.random.key(seed)
    keys = iter(jax.random.split(key, 64))
    f32 = jnp.float32
    L = DEPTH

    def normal(shape, scale=1.0):
        return jax.random.normal(next(keys), shape, f32) * scale

    def gain(shape, noise=0.02):
        return 1.0 + noise * jax.random.normal(next(keys), shape, f32)

    n_pages = PAST_LEN // PAGE_SIZE
    n_used = DEC_BATCH * n_pages
    n_pool = n_used + n_used // 4
    page_table = jax.random.permutation(next(keys), n_pool)[:n_used].reshape(
        DEC_BATCH, n_pages).astype(jnp.int32)
    return {
        'x_prompt': normal((BATCH, SEQ, D_MODEL)),
        'x_sample': normal((DEC_BATCH, DEC_SEQ, D_MODEL)),
        'state_shift': normal((L, DEC_BATCH, R_COLS)),
        'state_wkv': normal((L, DEC_BATCH, R_HEADS, R_HEAD_DIM, R_HEAD_DIM), 0.5),
        'state_conv': normal((L, DEC_BATCH, CONV_W - 1, D_FF)),
        'cache_k': normal((L, n_pool, PAGE_SIZE, S_HEADS, S_HEAD_DIM)),
        'cache_v': normal((L, n_pool, PAGE_SIZE, S_HEADS, S_HEAD_DIM)),
        'page_table': page_table,
        'p_prompt': normal((L, BATCH, SEQ, PLE_DIM)),
        'p_sample': normal((L, DEC_BATCH, DEC_SEQ, PLE_DIM)),
        'g_mix': gain((L, D_MODEL)),
        'w_in': normal((L, D_MODEL, N_IN), D_MODEL ** -0.5),
        'sb_bias': normal((L, S_HEADS), 0.5) - SB_BIAS_INIT,
        'mu_shift': jax.random.uniform(next(keys), (L, R_COLS), f32),
        'w0': normal((L, R_WIDTH), 0.5),
        'w2': normal((L, DECAY_LORA, R_WIDTH), 0.5 * DECAY_LORA ** -0.5),
        'a0': normal((L, R_WIDTH), 0.5),
        'a2': normal((L, ICLR_LORA, R_WIDTH), 0.5 * ICLR_LORA ** -0.5),
        'g2': normal((L, GATE_LORA, R_WIDTH), GATE_LORA ** -0.5),
        'k_k': gain((L, R_WIDTH), 0.1),
        'k_a': gain((L, R_WIDTH), 0.1),
        'r_k': normal((L, R_HEADS, R_HEAD_DIM), 0.1),
        'ln_x_g': gain((L, R_WIDTH)),
        'ln_x_b': normal((L, R_WIDTH), 0.02),
        'w_br_r': normal((L, R_WIDTH, D_MODEL), R_WIDTH ** -0.5),
        'w_br_s': normal((L, S_WIDTH, D_MODEL), S_WIDTH ** -0.5),
        'w_o': normal((L, D_MODEL, D_MODEL), D_MODEL ** -0.5),
        'g_ffn': gain((L, D_MODEL)),
        'w_up': normal((L, D_MODEL, 2 * D_FF), D_MODEL ** -0.5),
        'conv_w': normal((L, CONV_W, D_FF), CONV_W ** -0.5),
        'w_down': normal((L, D_FF, D_MODEL), D_FF ** -0.5),
        'g_ple': gain((L, D_MODEL)),
        'w_ple': normal((L, PLE_DIM, D_MODEL), PLE_DIM ** -0.5),
        'w_pg': normal((L, D_MODEL, D_MODEL), D_MODEL ** -0.5),
        'g_final': gain((D_MODEL,)),
    }


def reference(x_prompt, x_sample, state_shift, state_wkv, state_conv, cache_k, cache_v,
              page_table, p_prompt, p_sample, g_mix, w_in, sb_bias, mu_shift, w0, w2, a0, a2, g2,
              k_k, k_a, r_k, ln_x_g, ln_x_b, w_br_r, w_br_s, w_o, g_ffn, w_up, conv_w,
              w_down, g_ple, w_ple, w_pg, g_final):
    B = x_prompt.shape[0]
    Bd = x_sample.shape[0]
    n_pages = page_table.shape[1]
    dt = x_prompt.dtype
    h_p, h_s = x_prompt, x_sample
    ps_shift, ps_wkv, ps_conv, ps_k, ps_v = [], [], [], [], []
    ss_shift, ss_wkv, ss_conv, ss_k, ss_v = [], [], [], [], []
    for i in range(DEPTH):
        lw = (g_mix[i], w_in[i], sb_bias[i], mu_shift[i], w0[i], w2[i], a0[i], a2[i], g2[i], k_k[i],
              k_a[i], r_k[i], ln_x_g[i], ln_x_b[i], w_br_r[i], w_br_s[i], w_o[i], g_ffn[i], w_up[i],
              conv_w[i], w_down[i], g_ple[i], w_ple[i], w_pg[i])
        empty_kv = jnp.zeros((B, 0, S_HEADS, S_HEAD_DIM), dt)
        h_p, sh, wk, cv, kn, vn = hybrid_layer(
            h_p, p_prompt[i], jnp.zeros((B, R_COLS), dt),
            jnp.zeros((B, R_HEADS, R_HEAD_DIM, R_HEAD_DIM), jnp.float32),
            jnp.zeros((B, CONV_W - 1, D_FF), dt), empty_kv, empty_kv, *lw)
        ps_shift.append(sh); ps_wkv.append(wk); ps_conv.append(cv); ps_k.append(kn); ps_v.append(vn)
        k_past = cache_k[i][page_table].reshape(Bd, n_pages * PAGE_SIZE, S_HEADS, S_HEAD_DIM)
        v_past = cache_v[i][page_table].reshape(Bd, n_pages * PAGE_SIZE, S_HEADS, S_HEAD_DIM)
        h_s, sh, wk, cv, kn, vn = hybrid_layer(
            h_s, p_sample[i], state_shift[i], state_wkv[i], state_conv[i], k_past, v_past, *lw)
        ss_shift.append(sh); ss_wkv.append(wk); ss_conv.append(cv); ss_k.append(kn); ss_v.append(vn)
    y_prompt = rms_norm(h_p, g_final)
    y_sample = rms_norm(h_s, g_final)
    return (y_prompt, y_sample,
            jnp.stack(ps_shift), jnp.stack(ps_wkv), jnp.stack(ps_conv), jnp.stack(ps_k), jnp.stack(ps_v),
            jnp.stack(ss_shift), jnp.stack(ss_wkv), jnp.stack(ss_conv), jnp.stack(ss_k), jnp.stack(ss_v))
```

```python
import functools
import math

import jax
import jax.numpy as jnp
from jax import lax
from jax.experimental import pallas as pl
from jax.experimental.pallas import tpu as pltpu

F32 = jnp.float32
BF16 = jnp.bfloat16

LANES = 128
SUBLANES = 8
V7X_SCOPED_VMEM_BYTES = 60000 * 1024

HEAD_DIM = 64
CHUNK = 128
DECAY_SCALE = 0.606531
NORM_EPS = 1e-6
GN_EPS = HEAD_DIM * 1e-5
KK_EPS = 1e-24


def _vmem_limit(block_bytes):
    return int(min(V7X_SCOPED_VMEM_BYTES, 2 * block_bytes + (16 << 20)))


def _pick(n, pref, align):
    if n <= pref:
        return n
    best = None
    for d in range(align, pref + 1, align):
        if n % d == 0:
            best = d
    assert best is not None, (n, pref, align)
    return best


def _pad_to(x, axis, size):
    pad = size - x.shape[axis]
    if pad == 0:
        return x
    widths = [(0, 0)] * x.ndim
    widths[axis] = (0, pad)
    return jnp.pad(x, widths)


def _ceil_to(n, m):
    return -(-n // m) * m


def _dot(a, b):
    return jnp.dot(a, b, preferred_element_type=F32)


def _dot_nt(a, b):
    return lax.dot_general(a, b, (((1,), (1,)), ((), ())), preferred_element_type=F32)


def _dot_tn(a, b):
    return lax.dot_general(a, b, (((0,), (0,)), ((), ())), preferred_element_type=F32)


def _split2(x):
    hi = x.astype(BF16)
    lo = (x - hi.astype(F32)).astype(BF16)
    return hi, lo


def _split3(x):
    hi = x.astype(BF16)
    r1 = x - hi.astype(F32)
    mid = r1.astype(BF16)
    lo = (r1 - mid.astype(F32)).astype(BF16)
    return hi, mid, lo


def _iota2(shape, dim):
    return lax.broadcasted_iota(jnp.int32, shape, dim)


def _div_pow2(x, n):
    assert n & (n - 1) == 0, n
    return x >> (n.bit_length() - 1)


def _mod_pow2(x, n):
    assert n & (n - 1) == 0, n
    return x & (n - 1)


def _head_sum(x, e2):
    hi, lo = _split2(x)
    return _dot(jnp.concatenate([hi, lo], axis=1), e2)


def _same_head_matrix():
    r = _mod_pow2(_iota2((2 * LANES, LANES), 0), LANES)
    c = _iota2((2 * LANES, LANES), 1)
    return jnp.where(_div_pow2(r, HEAD_DIM) == _div_pow2(c, HEAD_DIM), 1.0, 0.0).astype(BF16)


def _rms_matmul_kernel(x_ref, g_ref, w_ref, o_ref, xn_ref):
    @pl.when(pl.program_id(1) == 0)
    def _():
        x = x_ref[...]
        ms = jnp.mean(x * x, axis=-1, keepdims=True)
        xn_ref[...] = (x * lax.rsqrt(ms + NORM_EPS) * g_ref[...]).astype(BF16)

    o_ref[...] = _dot(xn_ref[...], w_ref[...]).astype(o_ref.dtype)


def _rms_matmul(x, g, w, out_dtype, name):
    m, d = x.shape
    n = w.shape[1]
    bm = _pick(m, 512, SUBLANES)
    bn = _pick(n, 1024, 2 * LANES) if n % (2 * LANES) == 0 else _pick(n, 1024, LANES)
    blk = bm * d * 4 + d * bn * 2 + bm * bn * jnp.dtype(out_dtype).itemsize + bm * d * 2
    return pl.pallas_call(
        _rms_matmul_kernel,
        out_shape=jax.ShapeDtypeStruct((m, n), out_dtype),
        grid=(m // bm, n // bn),
        in_specs=[
            pl.BlockSpec((bm, d), lambda i, j: (i, 0)),
            pl.BlockSpec((1, d), lambda i, j: (0, 0)),
            pl.BlockSpec((d, bn), lambda i, j: (0, j)),
        ],
        out_specs=pl.BlockSpec((bm, bn), lambda i, j: (i, j)),
        scratch_shapes=[pltpu.VMEM((bm, d), BF16)],
        compiler_params=pltpu.CompilerParams(
            dimension_semantics=("parallel", "arbitrary"),
            vmem_limit_bytes=_vmem_limit(blk)),
        name=name,
    )(x, g, w)


def _neumann_inverse(a):
    c = a.shape[0]
    eye = jnp.where(_iota2((c, c), 0) == _iota2((c, c), 1), 1.0, 0.0).astype(F32)
    levels = int(math.log2(c))
    t = eye + a
    pb = a.astype(BF16)
    p = _dot(pb, pb)
    for _ in range(levels - 2):
        pb = p.astype(BF16)
        st = _dot(jnp.concatenate([t.astype(BF16), pb], axis=0), pb)
        t = t + st[:c]
        p = st[c:]
    return t + _dot(t.astype(BF16), p.astype(BF16))


def _wkv_intra_kernel(r_ref, k_ref, v_ref, xw_ref, xa_ref, xg_ref,
                      rp_ref, kp_ref, vp_ref, xwp_ref, xap_ref, xgp_ref,
                      rs_ref, ks_ref, vs_ref, xws_ref, xas_ref, xgs_ref,
                      mur_ref, muk_ref, muv_ref, muw_ref, mua_ref, mug_ref,
                      w0_ref, a0_ref, w2_ref, a2_ref, g2_ref,
                      kkw_ref, kaw_ref, rkw_ref,
                      g_out, h_out, rp_out, yl_out, gate_out, bonus_out,
                      *, t_real, n_pairs):
    c_idx = pl.program_id(1)
    C = CHUNK
    first = c_idx == 0

    def mixed(x_ref, p_ref, s_ref, mu_ref):
        x = x_ref[0]
        prev_last = jnp.where(first, s_ref[0], p_ref[0][SUBLANES - 1:SUBLANES])
        rolled = pltpu.roll(x, 1, 0)
        prev = jnp.where(_iota2(x.shape, 0) == 0, prev_last, rolled)
        y = x + (prev - x) * mu_ref[...]
        if t_real < C:
            y = jnp.concatenate([y, jnp.zeros((C - t_real, y.shape[1]), F32)], axis=0)
        return y

    r_all = mixed(r_ref, rp_ref, rs_ref, mur_ref)
    k_all = mixed(k_ref, kp_ref, ks_ref, muk_ref)
    v_all = mixed(v_ref, vp_ref, vs_ref, muv_ref)
    xw = jnp.tanh(mixed(xw_ref, xwp_ref, xws_ref, muw_ref)).astype(BF16)
    xa = mixed(xa_ref, xap_ref, xas_ref, mua_ref).astype(BF16)
    xg = jax.nn.sigmoid(mixed(xg_ref, xgp_ref, xgs_ref, mug_ref)).astype(BF16)

    lw_all = -DECAY_SCALE * jax.nn.sigmoid(w0_ref[...] + _dot(xw, w2_ref[...]))
    if t_real < C:
        lw_all = jnp.where(_iota2(lw_all.shape, 0) < t_real, lw_all, 0.0)
    iclr_all = jax.nn.sigmoid(a0_ref[...] + _dot(xa, a2_ref[...]))
    gate_all = _dot(xg, g2_ref[...])
    gate_out[0] = gate_all.astype(gate_out.dtype)

    e2 = _same_head_matrix()
    row = _iota2((C, C), 0)
    col = _iota2((C, C), 1)
    strict_lower = row > col
    lower = row >= col
    l_incl = jnp.where(lower, 1.0, 0.0).astype(BF16)
    l3 = jnp.concatenate([l_incl, l_incl, l_incl], axis=1)
    lane = _iota2((C, LANES), 1)
    head_masks = (lane < HEAD_DIM, lane >= HEAD_DIM)
    rr = _iota2((LANES, LANES), 0)
    cc = _iota2((LANES, LANES), 1)
    block_mask = _div_pow2(rr, HEAD_DIM) == _div_pow2(cc, HEAD_DIM)
    eye_mask = rr == cc

    for p in range(n_pairs):
        sl = slice(p * LANES, (p + 1) * LANES)
        r, k, v = r_all[:, sl], k_all[:, sl], v_all[:, sl]
        lw, iclr = lw_all[:, sl], iclr_all[:, sl]

        kkr = k * kkw_ref[:, sl]
        kk = kkr * lax.rsqrt(jnp.maximum(_head_sum(kkr * kkr, e2), KK_EPS))
        k_mod = k * (1.0 + (iclr - 1.0) * kaw_ref[:, sl])
        bonus_out[0, :, sl] = _head_sum(r * k_mod * rkw_ref[:, sl], e2) * v
        a = -kk
        b = kk * iclr

        hi, mid, lo = _split3(lw)
        cum = _dot(l3, jnp.concatenate([hi, mid, lo], axis=0))
        m_row = cum[C // 2 - 1:C // 2]
        cum_last = cum[C - 1:C]
        g = cum - m_row
        eg = jnp.exp(g)
        eng = jnp.exp(-g)
        ebar = jnp.exp(cum_last - cum)
        at = a * jnp.exp(g - lw)
        rt = r * eg
        kh = (k_mod * eng).astype(BF16)
        bh = (b * eng).astype(BF16)
        kbar = (k_mod * ebar).astype(BF16)
        bbar = (b * ebar).astype(BF16)
        em = jnp.exp(m_row)
        ecl = jnp.exp(cum_last)
        vb = v.astype(BF16)
        rhs_scores = jnp.concatenate([kh, bh], axis=0)

        ul_pair = None
        for h in range(2):
            mh = head_masks[h]
            am = jnp.where(mh, at, 0.0)
            rm = jnp.where(mh, rt, 0.0)
            amb = am.astype(BF16)
            sc = _dot_nt(jnp.concatenate([amb, rm.astype(BF16)], axis=0), rhs_scores)
            m_ak = jnp.where(strict_lower, sc[:C, :C], 0.0)
            m_ab = jnp.where(strict_lower, sc[:C, C:], 0.0)
            p_rk = jnp.where(lower, sc[C:, :C], 0.0)
            p_rb = jnp.where(lower, sc[C:, C:], 0.0)
            t_inv = _neumann_inverse(m_ab).astype(BF16)
            w1 = _dot(m_ak.astype(BF16), vb)
            tu = _dot(t_inv, jnp.concatenate([w1.astype(BF16), amb], axis=1))
            ul = tu[:, :LANES]
            ap = tu[:, LANES:]
            p_rb_b = p_rb.astype(BF16)
            yl = _dot(jnp.concatenate([p_rk.astype(BF16), p_rb_b], axis=1),
                      jnp.concatenate([vb, ul.astype(BF16)], axis=0))
            rp = rm + _dot(p_rb_b, ap.astype(BF16))
            if h == 0:
                ul_pair, ap_pair, yl_pair, rp_pair = ul, ap, yl, rp
            else:
                ul_pair = jnp.where(mh, ul, ul_pair)
                yl_pair = jnp.where(mh, yl, yl_pair)
                ap_pair = ap_pair + ap
                rp_pair = rp_pair + rp

        ba = _dot_tn(bbar, ap_pair.astype(BF16))
        hh = _dot_tn(jnp.concatenate([kbar, bbar], axis=0),
                     jnp.concatenate([vb, ul_pair.astype(BF16)], axis=0))
        g_bd = jnp.where(block_mask, ba * em, 0.0) + jnp.where(eye_mask, ecl, 0.0)
        g_out[0, 0, :, sl] = g_bd
        h_out[0, 0, :, sl] = jnp.where(block_mask, hh, 0.0)
        rp_out[0, :, sl] = (rp_pair * em).astype(rp_out.dtype)
        yl_out[0, :, sl] = yl_pair


def _wkv_intra(p3, shift, mu, w0, a0, w2, a2, g2, kkw, kaw, rkw, *, rw, dlp, ilp, glp):
    bsz, t, npc = p3.shape
    C = CHUNK
    t_real = min(t, C)
    assert t % t_real == 0 and t_real % SUBLANES == 0
    nch = t // t_real
    tp = nch * C
    n_pairs = 2 if rw % (2 * LANES) == 0 else 1
    lw_ = n_pairs * LANES
    ng = rw // lw_
    o_w, o_a, o_g = 3 * rw, 3 * rw + dlp, 3 * rw + dlp + ilp
    assert o_w % dlp == 0 and o_a % ilp == 0 and o_g % glp == 0
    rows_prev = t_real // SUBLANES

    def cur(width, off):
        return pl.BlockSpec((1, t_real, width), lambda b, c, g, o=off // width: (b, c, o))

    def cur_g(width, off):
        return pl.BlockSpec((1, t_real, width), lambda b, c, g, o=off // width: (b, c, o + g))

    def prev(width, off):
        return pl.BlockSpec((1, SUBLANES, width),
                            lambda b, c, g, o=off // width: (b, jnp.maximum(c * rows_prev - 1, 0), o))

    def prev_g(width, off):
        return pl.BlockSpec((1, SUBLANES, width),
                            lambda b, c, g, o=off // width: (b, jnp.maximum(c * rows_prev - 1, 0), o + g))

    def sh(width, off):
        return pl.BlockSpec((1, 1, width), lambda b, c, g, o=off // width: (b, 0, o))

    def sh_g(width, off):
        return pl.BlockSpec((1, 1, width), lambda b, c, g, o=off // width: (b, 0, o + g))

    def vec(width, off):
        return pl.BlockSpec((1, width), lambda b, c, g, o=off // width: (0, o))

    def vec_g(width, off=0):
        return pl.BlockSpec((1, width), lambda b, c, g, o=off // width: (0, o + g))

    def mat_g(rows):
        return pl.BlockSpec((rows, lw_), lambda b, c, g: (0, g))

    in_specs = (
        [cur_g(lw_, 0), cur_g(lw_, rw), cur_g(lw_, 2 * rw), cur(dlp, o_w), cur(ilp, o_a), cur(glp, o_g)]
        + [prev_g(lw_, 0), prev_g(lw_, rw), prev_g(lw_, 2 * rw), prev(dlp, o_w), prev(ilp, o_a), prev(glp, o_g)]
        + [sh_g(lw_, 0), sh_g(lw_, rw), sh_g(lw_, 2 * rw), sh(dlp, o_w), sh(ilp, o_a), sh(glp, o_g)]
        + [vec_g(lw_, 0), vec_g(lw_, rw), vec_g(lw_, 2 * rw), vec(dlp, o_w), vec(ilp, o_a), vec(glp, o_g)]
        + [vec_g(lw_), vec_g(lw_), mat_g(dlp), mat_g(ilp), mat_g(glp)]
        + [vec_g(lw_), vec_g(lw_), vec_g(lw_)]
    )
    args = ([p3] * 6 + [p3] * 6 + [shift] * 6 + [mu] * 6 + [w0, a0, w2, a2, g2, kkw, kaw, rkw])
    out_shape = (
        jax.ShapeDtypeStruct((bsz, nch, LANES, rw), F32),
        jax.ShapeDtypeStruct((bsz, nch, LANES, rw), F32),
        jax.ShapeDtypeStruct((bsz, tp, rw), BF16),
        jax.ShapeDtypeStruct((bsz, tp, rw), F32),
        jax.ShapeDtypeStruct((bsz, tp, rw), BF16),
        jax.ShapeDtypeStruct((bsz, tp, rw), F32),
    )
    gh_spec = pl.BlockSpec((1, 1, LANES, lw_), lambda b, c, g: (b, c, 0, g))
    tok_spec = pl.BlockSpec((1, C, lw_), lambda b, c, g: (b, c, g))
    blk = (t_real * (3 * lw_ + dlp + ilp + glp) * 4 + (dlp + ilp + glp) * lw_ * 2
           + 2 * LANES * lw_ * 4 + C * lw_ * 12)
    return pl.pallas_call(
        functools.partial(_wkv_intra_kernel, t_real=t_real, n_pairs=n_pairs),
        out_shape=out_shape,
        grid=(bsz, nch, ng),
        in_specs=in_specs,
        out_specs=(gh_spec, gh_spec, tok_spec, tok_spec, tok_spec, tok_spec),
        compiler_params=pltpu.CompilerParams(
            dimension_semantics=("parallel", "parallel", "parallel"),
            vmem_limit_bytes=_vmem_limit(blk)),
        name="wkv_intra",
    )(*args)


def _wkv_scan_kernel(g_ref, h_ref, rp_ref, yl_ref, gate_ref, bonus_ref, s0_ref, lng_ref, lnb_ref,
                     y_ref, s_out_ref, *, n_chunks, n_pairs):
    C = CHUNK
    e2 = _same_head_matrix()
    inv_n = 1.0 / HEAD_DIM
    lanes = [slice(p * LANES, (p + 1) * LANES) for p in range(n_pairs)]

    s_out_ref[...] = s0_ref[...]

    def body(c, carry):
        rows = pl.ds(pl.multiple_of(c * C, C), C)
        for sl in lanes:
            sb = s_out_ref[0, :, sl].astype(BF16)
            y = _dot(rp_ref[0, rows, sl], sb) + yl_ref[0, rows, sl]
            mu = _head_sum(y, e2) * inv_n
            yc = y - mu
            var = _head_sum(yc * yc, e2) * inv_n
            yn = yc * lax.rsqrt(var + GN_EPS) * lng_ref[:, sl] + lnb_ref[:, sl]
            out = (yn + bonus_ref[0, rows, sl]) * gate_ref[0, rows, sl].astype(F32)
            y_ref[0, rows, sl] = out.astype(y_ref.dtype)
            s_out_ref[0, :, sl] = _dot(g_ref[0, c, :, sl].astype(BF16), sb) + h_ref[0, c, :, sl]
        return carry

    lax.fori_loop(0, n_chunks, body, 0)


def _wkv_scan(g, h, rp, yl, gate, bonus, s0, lng, lnb):
    bsz, nch, _, rw = g.shape
    tp = rp.shape[1]
    n_pairs = 2 if rw % (2 * LANES) == 0 else 1
    lw_ = n_pairs * LANES
    gh_spec = pl.BlockSpec((1, nch, LANES, lw_), lambda b, q: (b, 0, 0, q))
    tok_spec = pl.BlockSpec((1, tp, lw_), lambda b, q: (b, 0, q))
    st_spec = pl.BlockSpec((1, LANES, lw_), lambda b, q: (b, 0, q))
    vec_spec = pl.BlockSpec((1, lw_), lambda b, q: (0, q))
    blk = 2 * nch * LANES * lw_ * 4 + tp * lw_ * (2 + 4 + 2 + 4 + 2) + 2 * LANES * lw_ * 4
    return pl.pallas_call(
        functools.partial(_wkv_scan_kernel, n_chunks=nch, n_pairs=n_pairs),
        out_shape=(jax.ShapeDtypeStruct((bsz, tp, rw), BF16),
                   jax.ShapeDtypeStruct((bsz, LANES, rw), F32)),
        grid=(bsz, rw // lw_),
        in_specs=[gh_spec, gh_spec, tok_spec, tok_spec, tok_spec, tok_spec, st_spec, vec_spec, vec_spec],
        out_specs=(tok_spec, st_spec),
        compiler_params=pltpu.CompilerParams(
            dimension_semantics=("parallel", "parallel"),
            vmem_limit_bytes=_vmem_limit(blk)),
        name="wkv_scan",
    )(g, h, rp, yl, gate, bonus, s0, lng, lnb)


def _cumsum_rhs():
    r = _mod_pow2(_iota2((2 * LANES, 2 * LANES), 0), LANES)
    c = _iota2((2 * LANES, 2 * LANES), 1)
    return jnp.where((c >= LANES) | (r > c), 1.0, 0.0).astype(BF16)


def _sb_tile(z, v_tile, run, acc, w2, mask):
    lk = -(jnp.maximum(z, 0.0) + jnp.log1p(jnp.exp(-jnp.abs(z))))
    if mask is not None:
        lk = jnp.where(mask, lk, 0.0)
    hi, lo = _split2(lk)
    cs2 = _dot(jnp.concatenate([hi, lo], axis=1), w2)
    att = jnp.exp(z + lk + cs2[:, :LANES] + run)
    if mask is not None:
        att = jnp.where(mask, att, 0.0)
    acc = acc + _dot(att.astype(BF16), v_tile)
    return run + cs2[:, LANES:], acc


def _attn_prompt_kernel(q_ref, k_ref, v_ref, bias_ref, o_ref, kb_ref, vb_ref):
    qi = pl.program_id(2)
    B = CHUNK

    @pl.when(qi == 0)
    def _():
        kb_ref[...] = k_ref[0].astype(BF16)
        vb_ref[...] = v_ref[0].astype(BF16)

    q = q_ref[0]
    lane = _iota2((B, LANES), 1)
    row = _iota2((B, LANES), 0)
    m0 = lane < HEAD_DIM
    causal = lane < row
    w2 = _cumsum_rhs()
    zero = jnp.zeros((), BF16)
    qh = (jnp.where(m0, q, zero), jnp.where(m0, zero, q))
    bias = (bias_ref[0, 0:1, :], bias_ref[0, 1:2, :])

    def block(j, carry, mask):
        rows = pl.ds(pl.multiple_of(j * B, B), B)
        kt = kb_ref[rows, :]
        vt = vb_ref[rows, :]
        out = []
        for h in range(2):
            z = _dot_nt(qh[h], kt) + bias[h]
            out.extend(_sb_tile(z, vt, carry[2 * h], carry[2 * h + 1], w2, mask))
        return tuple(out)

    zeros = jnp.zeros((B, LANES), F32)
    carry = block(qi, (zeros, zeros, zeros, zeros), causal)
    carry = lax.fori_loop(0, qi, lambda jj, c: block(qi - 1 - jj, c, None), carry)
    o_ref[0] = jnp.where(m0, carry[1], carry[3]).astype(o_ref.dtype)


def _attn_prompt(q, k, v, bias2):
    bsz, t, w = q.shape
    B = CHUNK
    assert t % B == 0
    blk = B * LANES * 2 * 2 + 2 * t * LANES * 4
    return pl.pallas_call(
        _attn_prompt_kernel,
        out_shape=jax.ShapeDtypeStruct((bsz, t, w), BF16),
        grid=(bsz, w // LANES, t // B),
        in_specs=[
            pl.BlockSpec((1, B, LANES), lambda b, p, i: (b, i, p)),
            pl.BlockSpec((1, t, LANES), lambda b, p, i: (b, 0, p)),
            pl.BlockSpec((1, t, LANES), lambda b, p, i: (b, 0, p)),
            pl.BlockSpec((1, 2, LANES), lambda b, p, i: (p, 0, 0)),
        ],
        out_specs=pl.BlockSpec((1, B, LANES), lambda b, p, i: (b, i, p)),
        scratch_shapes=[pltpu.VMEM((t, LANES), BF16), pltpu.VMEM((t, LANES), BF16)],
        compiler_params=pltpu.CompilerParams(
            dimension_semantics=("parallel", "parallel", "arbitrary"),
            vmem_limit_bytes=_vmem_limit(blk + 2 * t * LANES * 2)),
        name="attn_prompt",
    )(q, k, v, bias2)


PAGES_PER_STEP = 4


def _attn_paged_kernel(pt_ref, *refs, n_groups):
    del pt_ref
    pps = PAGES_PER_STEP
    k_refs = refs[:pps]
    v_refs = refs[pps:2 * pps]
    q_ref, kn_ref, vn_ref, bias_ref, o_ref, run_ref, acc_ref = refs[2 * pps:]
    g = pl.program_id(1)
    rows_n = q_ref.shape[1]
    t_new = o_ref.shape[1]
    w2 = _cumsum_rhs()
    q = q_ref[0]
    bias = bias_ref[...]

    def tile(k_tile, v_tile, run, acc, mask):
        z = _dot_nt(q, k_tile) + bias
        return _sb_tile(z, v_tile, run, acc, w2, mask)

    @pl.when(g == 0)
    def _():
        lane = _iota2((rows_n, LANES), 1)
        row = _iota2((rows_n, LANES), 0)
        mask = lane < _mod_pow2(row, t_new)
        run, acc = tile(kn_ref[0], vn_ref[0], jnp.zeros((rows_n, LANES), F32),
                        jnp.zeros(acc_ref.shape, F32), mask)
        run_ref[...] = run
        acc_ref[...] = acc

    for p in range(pps):
        run, acc = tile(k_refs[p][0].astype(BF16), v_refs[p][0].astype(BF16),
                        run_ref[...], acc_ref[...], None)
        run_ref[...] = run
        acc_ref[...] = acc

    @pl.when(g == n_groups - 1)
    def _():
        acc = acc_ref[...]
        rr = _iota2(acc.shape, 0)
        cc = _iota2(acc.shape, 1)
        picked = jnp.where(_div_pow2(rr, t_new) == _div_pow2(cc, HEAD_DIM), acc, 0.0)
        out = picked[0:t_new]
        for h in range(1, rows_n // t_new):
            out = out + picked[h * t_new:(h + 1) * t_new]
        o_ref[0] = out.astype(o_ref.dtype)


def _attn_paged(q_bd, k_new, v_new, bias_rows, cache_k, cache_v, page_table, t_new):
    bsz, rows_n, w = q_bd.shape
    n_pages = page_table.shape[1]
    page = cache_k.shape[1]
    pps = PAGES_PER_STEP
    assert page == LANES and n_pages % pps == 0
    n_groups = n_pages // pps

    def page_spec(p):
        return pl.BlockSpec((1, page, w),
                            lambda b, g, pt, p=p % pps: (pt[b, n_pages - 1 - (g * pps + p)], 0, 0))

    in_specs = ([page_spec(p) for p in range(2 * pps)] + [
        pl.BlockSpec((1, rows_n, w), lambda b, g, pt: (b, 0, 0)),
        pl.BlockSpec((1, LANES, w), lambda b, g, pt: (b, 0, 0)),
        pl.BlockSpec((1, LANES, w), lambda b, g, pt: (b, 0, 0)),
        pl.BlockSpec((rows_n, LANES), lambda b, g, pt: (0, 0)),
    ])
    blk = 2 * pps * page * w * 4 + rows_n * w * 2 + 2 * LANES * w * 2 + rows_n * w * 4
    return pl.pallas_call(
        functools.partial(_attn_paged_kernel, n_groups=n_groups),
        out_shape=jax.ShapeDtypeStruct((bsz, t_new, w), BF16),
        grid_spec=pltpu.PrefetchScalarGridSpec(
            num_scalar_prefetch=1,
            grid=(bsz, n_groups),
            in_specs=in_specs,
            out_specs=pl.BlockSpec((1, t_new, w), lambda b, g, pt: (b, 0, 0)),
            scratch_shapes=[pltpu.VMEM((rows_n, LANES), F32), pltpu.VMEM((rows_n, w), F32)],
        ),
        compiler_params=pltpu.CompilerParams(
            dimension_semantics=("parallel", "arbitrary"),
            vmem_limit_bytes=_vmem_limit(blk)),
        name="attn_paged",
    )(page_table, *([cache_k] * pps), *([cache_v] * pps), q_bd, k_new, v_new, bias_rows)


def _rms(x, g):
    ms = jnp.mean(x * x, axis=-1, keepdims=True)
    return x * lax.rsqrt(ms + NORM_EPS) * g


def _merge_kernel(yr_ref, os_ref, gr_ref, gs_ref, x_ref, wr_ref, ws_ref, wo_ref, g_ref, h_ref, hn_ref):
    a = _dot(yr_ref[...], wr_ref[...])
    b = _dot(os_ref[...], ws_ref[...])
    mixed = (jax.nn.sigmoid(gr_ref[...].astype(F32)) * a
             + jax.nn.sigmoid(gs_ref[...].astype(F32)) * b)
    h = x_ref[...] + _dot(mixed.astype(BF16), wo_ref[...])
    h_ref[...] = h
    hn_ref[...] = _rms(h, g_ref[...]).astype(hn_ref.dtype)


def _resident(shape):
    return pl.BlockSpec(shape, lambda *_: (0,) * len(shape), pipeline_mode=pl.Buffered(1))


def _merge(yr, os_, gates, x, wr, ws, wo, g_ffn):
    m, d = x.shape
    rw = yr.shape[1]
    bm = _pick(m, 256, SUBLANES)
    blk = bm * (2 * rw * 2 + 2 * d * 2 + d * 4 + d * 4 + d * 2) + (2 * rw * d + d * d)
    return pl.pallas_call(
        _merge_kernel,
        out_shape=(jax.ShapeDtypeStruct((m, d), F32), jax.ShapeDtypeStruct((m, d), BF16)),
        grid=(m // bm,),
        in_specs=[
            pl.BlockSpec((bm, rw), lambda i: (i, 0)),
            pl.BlockSpec((bm, rw), lambda i: (i, 0)),
            pl.BlockSpec((bm, d), lambda i: (i, 0)),
            pl.BlockSpec((bm, d), lambda i: (i, 1)),
            pl.BlockSpec((bm, d), lambda i: (i, 0)),
            _resident(wr.shape), _resident(ws.shape), _resident(wo.shape),
            pl.BlockSpec((1, d), lambda i: (0, 0)),
        ],
        out_specs=(pl.BlockSpec((bm, d), lambda i: (i, 0)), pl.BlockSpec((bm, d), lambda i: (i, 0))),
        compiler_params=pltpu.CompilerParams(
            dimension_semantics=("parallel",),
            vmem_limit_bytes=_vmem_limit(blk + bm * d * 16)),
        name="merge",
    )(yr, os_, gates, gates, x, wr, ws, wo, g_ffn)


def _gelu_tanh(x):
    return 0.5 * x * (1.0 + jnp.tanh(math.sqrt(2.0 / math.pi) * (x + 0.044715 * x * x * x)))


def _ffn_kernel(hn_ref, wg_ref, wv_ref, cw_ref, wd_ref, prev_ref, o_ref, tail_ref, carry_ref,
                *, blocks_per_seq, seq_len):
    i = pl.program_id(0)
    f = pl.program_id(1)
    hn = hn_ref[...]
    ug = _dot(hn, wg_ref[...])
    uv = _dot(hn, wv_ref[...])
    bm = ug.shape[0]
    row = _iota2(ug.shape, 0)
    r1 = pltpu.roll(ug, 1, 0)
    r2 = pltpu.roll(ug, 2, 0)
    if seq_len == SUBLANES:
        p2 = prev_ref[...]
        p1 = pltpu.roll(p2, bm - 1, 0)
        t = _mod_pow2(row, SUBLANES)
        s1 = jnp.where(t == 0, p1, r1)
        s2 = jnp.where(t < 2, p2, r2)
        tail_ref[...] = ug
    else:
        first = (i % blocks_per_seq) == 0
        pv = jnp.where(first, prev_ref[0], carry_ref[f])
        head = _iota2((SUBLANES, ug.shape[1]), 0)
        s1 = jnp.concatenate(
            [jnp.where(head == 0, pv[7:8], r1[:SUBLANES]), r1[SUBLANES:]], axis=0)
        s2 = jnp.concatenate(
            [jnp.where(head == 0, pv[6:7], jnp.where(head == 1, pv[7:8], r2[:SUBLANES])),
             r2[SUBLANES:]], axis=0)
        carry_ref[f] = ug[bm - SUBLANES:]
        tail_ref[0] = ug[bm - SUBLANES:]
    conv = cw_ref[0:1, :] * s2 + cw_ref[1:2, :] * s1 + cw_ref[2:3, :] * ug
    act = (_gelu_tanh(conv) * uv).astype(BF16)
    part = _dot(act, wd_ref[...])

    @pl.when(f == 0)
    def _():
        o_ref[...] = part

    @pl.when(f > 0)
    def _():
        o_ref[...] += part


def _ffn(hn, w_up, conv_w, w_down, prev8, seq_len):
    m, d = hn.shape
    ff = w_down.shape[0]
    bf = _pick(ff, 512, 2 * LANES) if ff % (2 * LANES) == 0 else _pick(ff, 512, LANES)
    nf = ff // bf
    if seq_len == SUBLANES:
        bm = m
        blocks_per_seq = 1
        prev_spec = pl.BlockSpec((bm, bf), lambda i, f: (i, f))
        tail_shape = (m, ff)
        tail_spec = pl.BlockSpec((bm, bf), lambda i, f: (i, f))
    else:
        bm = _pick(seq_len, 1024, SUBLANES)
        blocks_per_seq = seq_len // bm
        prev_spec = pl.BlockSpec((1, SUBLANES, bf), lambda i, f: (i // blocks_per_seq, 0, f))
        tail_shape = (m // seq_len, SUBLANES, ff)
        tail_spec = pl.BlockSpec((1, SUBLANES, bf), lambda i, f: (i // blocks_per_seq, 0, f))
    blk = bm * d * (2 + 4) + 3 * d * bf * 2 + 4 * bm * bf * 4
    return pl.pallas_call(
        functools.partial(_ffn_kernel, blocks_per_seq=blocks_per_seq, seq_len=seq_len),
        out_shape=(jax.ShapeDtypeStruct((m, d), F32), jax.ShapeDtypeStruct(tail_shape, F32)),
        grid=(m // bm, nf),
        in_specs=[
            pl.BlockSpec((bm, d), lambda i, f: (i, 0)),
            pl.BlockSpec((d, bf), lambda i, f: (0, f)),
            pl.BlockSpec((d, bf), lambda i, f: (0, nf + f)),
            pl.BlockSpec((3, bf), lambda i, f: (0, f)),
            pl.BlockSpec((bf, d), lambda i, f: (f, 0)),
            prev_spec,
        ],
        out_specs=(pl.BlockSpec((bm, d), lambda i, f: (i, 0)), tail_spec),
        scratch_shapes=[pltpu.VMEM((nf, SUBLANES, bf), F32)],
        compiler_params=pltpu.CompilerParams(
            dimension_semantics=("arbitrary", "arbitrary"),
            vmem_limit_bytes=_vmem_limit(blk)),
        name="ffn",
    )(hn, w_up, w_up, conv_w, w_down, prev8)


def _ple_kernel(h_ref, f_ref, pe_ref, wple_ref, wpg_ref, gp_ref, gf_ref, y_ref, *, final_norm):
    h2 = h_ref[...] + f_ref[...]
    hn = _rms(h2, gp_ref[...]).astype(BF16)
    gate = jax.nn.sigmoid(_dot(hn, wpg_ref[...]))
    emb = _dot(pe_ref[...].astype(BF16), wple_ref[...])
    h3 = h2 + emb * gate
    y_ref[...] = _rms(h3, gf_ref[...]) if final_norm else h3


def _ple_out(h, ffn_out, pe, w_ple, w_pg, g_ple, g_final, final_norm):
    m, d = h.shape
    pd = pe.shape[1]
    bm = _pick(m, 512, SUBLANES)
    blk = bm * (3 * d * 4 + pd * 4) + pd * d * 2 + d * d * 2
    return pl.pallas_call(
        functools.partial(_ple_kernel, final_norm=final_norm),
        out_shape=jax.ShapeDtypeStruct((m, d), F32),
        grid=(m // bm,),
        in_specs=[
            pl.BlockSpec((bm, d), lambda i: (i, 0)),
            pl.BlockSpec((bm, d), lambda i: (i, 0)),
            pl.BlockSpec((bm, pd), lambda i: (i, 0)),
            _resident(w_ple.shape), _resident(w_pg.shape),
            pl.BlockSpec((1, d), lambda i: (0, 0)),
            pl.BlockSpec((1, d), lambda i: (0, 0)),
        ],
        out_specs=pl.BlockSpec((bm, d), lambda i: (i, 0)),
        compiler_params=pltpu.CompilerParams(
            dimension_semantics=("parallel",),
            vmem_limit_bytes=_vmem_limit(blk + bm * d * 8)),
        name="ple_out",
    )(h, ffn_out, pe, w_ple, w_pg, g_ple, g_final)


def _prep_weights(g_mix, w_in, mu_shift, w0, w2, a0, a2, g2, k_k, k_a, r_k, ln_x_g, ln_x_b,
                  w_br_r, w_br_s, w_o, g_ffn, w_up, conv_w, w_down, g_ple, w_ple, w_pg, sb_bias):
    d = w_in.shape[0]
    rw = w0.shape[0]
    dl, il, gl = w2.shape[0], a2.shape[0], g2.shape[0]
    dlp, ilp, glp = (_ceil_to(n, LANES) for n in (dl, il, gl))
    sw = w_br_s.shape[0]
    rc = 3 * rw + dl + il + gl
    c_w, c_a, c_g = 3 * rw, 3 * rw + dl, 3 * rw + dl + il

    def regroup(x):
        return jnp.concatenate([
            x[..., :c_w], _pad_to(x[..., c_w:c_a], -1, dlp), _pad_to(x[..., c_a:c_g], -1, ilp),
            _pad_to(x[..., c_g:rc], -1, glp)], axis=-1)

    w_r = regroup(w_in[:, :rc]).astype(BF16)
    scale = HEAD_DIM ** -0.5
    w_q = (w_in[:, rc:rc + sw] * scale).astype(BF16)
    w_k = w_in[:, rc + sw:rc + 2 * sw].astype(BF16)
    w_v = w_in[:, rc + 2 * sw:rc + 3 * sw].astype(BF16)
    w_g = w_in[:, rc + 3 * sw:].astype(BF16)
    nh_s = sw // HEAD_DIM
    return dict(
        d=d, rw=rw, sw=sw, dl=dl, il=il, gl=gl, dlp=dlp, ilp=ilp, glp=glp, rc=rc,
        regroup=regroup,
        g_mix=g_mix.reshape(1, d), w_r=w_r, w_q=w_q, w_k=w_k, w_v=w_v, w_g=w_g,
        mu=regroup(mu_shift).reshape(1, -1),
        w0=w0.reshape(1, rw), a0=a0.reshape(1, rw),
        w2=_pad_to(w2, 0, dlp).astype(BF16), a2=_pad_to(a2, 0, ilp).astype(BF16),
        g2=_pad_to(g2, 0, glp).astype(BF16),
        kkw=k_k.reshape(1, rw), kaw=k_a.reshape(1, rw), rkw=r_k.reshape(1, rw),
        lng=ln_x_g.reshape(1, rw), lnb=ln_x_b.reshape(1, rw),
        w_br_r=w_br_r.astype(BF16), w_br_s=w_br_s.astype(BF16), w_o=w_o.astype(BF16),
        g_ffn=g_ffn.reshape(1, d), w_up=w_up.astype(BF16), conv_w=conv_w,
        w_down=w_down.astype(BF16), g_ple=g_ple.reshape(1, d),
        w_ple=w_ple.astype(BF16), w_pg=w_pg.astype(BF16),
        bias2=jnp.broadcast_to(sb_bias.reshape(nh_s // 2, 2, 1), (nh_s // 2, 2, LANES)).astype(F32),
        sb_bias=sb_bias,
    )


def _state_to_blockdiag(s):
    bsz, nh, n, _ = s.shape
    st = jnp.swapaxes(s, -1, -2).reshape(bsz, nh // 2, 2, n, n)
    eye = jnp.eye(2, dtype=s.dtype)
    bd = st[:, :, :, :, None, :] * eye[None, None, :, None, :, None]
    bd = bd.reshape(bsz, nh // 2, 2 * n, 2 * n)
    return jnp.swapaxes(bd, 1, 2).reshape(bsz, 2 * n, nh * n)


def _blockdiag_to_state(bd, nh):
    bsz = bd.shape[0]
    n = HEAD_DIM
    x = bd.reshape(bsz, 2, n, nh // 2, 2, n)
    heads = [x[:, h, :, :, h, :] for h in range(2)]
    st = jnp.stack(heads, axis=3)
    st = jnp.transpose(st, (0, 2, 3, 4, 1))
    return st.reshape(bsz, nh, n, n)


def _layer(x3, pe3, shift_prev, wkv_prev, conv_prev, past, wp, g_final, final_norm):
    bsz, t, d = x3.shape
    m = bsz * t
    rw, sw = wp["rw"], wp["sw"]
    nh = rw // HEAD_DIM
    x = x3.reshape(m, d)

    p_r = _rms_matmul(x, wp["g_mix"], wp["w_r"], F32, "proj_r")
    q = _rms_matmul(x, wp["g_mix"], wp["w_q"], BF16, "proj_q")
    k_s = _rms_matmul(x, wp["g_mix"], wp["w_k"], F32, "proj_k")
    v_s = _rms_matmul(x, wp["g_mix"], wp["w_v"], F32, "proj_v")
    gates = _rms_matmul(x, wp["g_mix"], wp["w_g"], BF16, "proj_g")

    p3 = p_r.reshape(bsz, t, -1)
    shift = wp["regroup"](shift_prev).reshape(bsz, 1, -1)
    g_c, h_c, rp, yl, gate, bonus = _wkv_intra(
        p3, shift, wp["mu"], wp["w0"], wp["a0"], wp["w2"], wp["a2"], wp["g2"],
        wp["kkw"], wp["kaw"], wp["rkw"], rw=rw, dlp=wp["dlp"], ilp=wp["ilp"], glp=wp["glp"])
    y_r, s_fin = _wkv_scan(g_c, h_c, rp, yl, gate, bonus, _state_to_blockdiag(wkv_prev),
                           wp["lng"], wp["lnb"])
    y_r = y_r[:, :t].reshape(m, rw)
    wkv_new = _blockdiag_to_state(s_fin, nh)
    last = p3[:, -1]
    dlp, ilp = wp["dlp"], wp["ilp"]
    c0 = 3 * rw
    shift_new = jnp.concatenate([
        last[:, :c0], last[:, c0:c0 + wp["dl"]], last[:, c0 + dlp:c0 + dlp + wp["il"]],
        last[:, c0 + dlp + ilp:c0 + dlp + ilp + wp["gl"]]], axis=-1)

    if past is None:
        o_s = _attn_prompt(q.reshape(bsz, t, sw), k_s.reshape(bsz, t, sw), v_s.reshape(bsz, t, sw),
                           wp["bias2"])
    else:
        cache_k, cache_v, page_table = past
        nh_s = sw // HEAD_DIM
        rows_n = nh_s * t
        q3 = q.reshape(bsz, t, sw)
        rr = jnp.arange(rows_n)[:, None] // t
        cc = jnp.arange(sw)[None, :] // HEAD_DIM
        q_bd = jnp.where(rr == cc, jnp.tile(q3, (1, nh_s, 1)), jnp.zeros((), BF16))
        k_new = _pad_to(k_s.reshape(bsz, t, sw).astype(BF16), 1, LANES)
        v_new = _pad_to(v_s.reshape(bsz, t, sw).astype(BF16), 1, LANES)
        bias_rows = jnp.broadcast_to(jnp.repeat(wp["sb_bias"].astype(F32), t)[:, None], (rows_n, LANES))
        o_s = _attn_paged(q_bd, k_new, v_new, bias_rows,
                          cache_k.reshape(cache_k.shape[0], cache_k.shape[1], sw),
                          cache_v.reshape(cache_v.shape[0], cache_v.shape[1], sw),
                          page_table, t)
    o_s = o_s.reshape(m, sw)

    h, hn = _merge(y_r, o_s, gates, x, wp["w_br_r"], wp["w_br_s"], wp["w_o"], wp["g_ffn"])
    ff = wp["w_down"].shape[0]
    nprev = conv_prev.shape[1]
    pad_rows = jnp.zeros((bsz, SUBLANES - nprev, ff), F32)
    if t == SUBLANES:
        prev8 = jnp.concatenate([conv_prev, pad_rows], axis=1).reshape(m, ff)
    else:
        prev8 = jnp.concatenate([pad_rows, conv_prev], axis=1)
    ffn_out, tail = _ffn(hn, wp["w_up"], wp["conv_w"], wp["w_down"], prev8, t)
    conv_new = tail.reshape(bsz, SUBLANES, ff)[:, SUBLANES - nprev:]
    y = _ple_out(h, ffn_out, pe3.reshape(m, -1), wp["w_ple"], wp["w_pg"], wp["g_ple"],
                 g_final.reshape(1, d), final_norm)
    return (y.reshape(bsz, t, d), shift_new, wkv_new, conv_new,
            k_s.reshape(bsz, t, sw // HEAD_DIM, HEAD_DIM), v_s.reshape(bsz, t, sw // HEAD_DIM, HEAD_DIM))


def kernel(x_prompt, x_sample, state_shift, state_wkv, state_conv, cache_k, cache_v, page_table, p_prompt, p_sample, g_mix, w_in, sb_bias, mu_shift, w0, w2, a0, a2, g2, k_k, k_a, r_k, ln_x_g, ln_x_b, w_br_r, w_br_s, w_o, g_ffn, w_up, conv_w, w_down, g_ple, w_ple, w_pg, g_final):
    depth = w_in.shape[0]
    bsz = x_prompt.shape[0]
    rw = w0.shape[1]
    nh = rw // HEAD_DIM
    ff = w_down.shape[1]
    rc = state_shift.shape[-1]
    nprev = state_conv.shape[2]
    h_p, h_s = x_prompt, x_sample
    outs_p = [[] for _ in range(5)]
    outs_s = [[] for _ in range(5)]
    for i in range(depth):
        wp = _prep_weights(g_mix[i], w_in[i], mu_shift[i], w0[i], w2[i], a0[i], a2[i], g2[i], k_k[i],
                           k_a[i], r_k[i], ln_x_g[i], ln_x_b[i], w_br_r[i], w_br_s[i], w_o[i],
                           g_ffn[i], w_up[i], conv_w[i], w_down[i], g_ple[i], w_ple[i], w_pg[i],
                           sb_bias[i])
        last = i == depth - 1
        res_p = _layer(h_p, p_prompt[i], jnp.zeros((bsz, rc), F32),
                       jnp.zeros((bsz, nh, HEAD_DIM, HEAD_DIM), F32),
                       jnp.zeros((bsz, nprev, ff), F32), None, wp, g_final, last)
        res_s = _layer(h_s, p_sample[i], state_shift[i], state_wkv[i], state_conv[i],
                       (cache_k[i], cache_v[i], page_table), wp, g_final, last)
        h_p, h_s = res_p[0], res_s[0]
        for dst, res in ((outs_p, res_p), (outs_s, res_s)):
            for lst, val in zip(dst, res[1:]):
                lst.append(val)
    return (h_p, h_s, *(jnp.stack(o) for o in outs_p), *(jnp.stack(o) for o in outs_s))
```

```python
import functools
import math

import jax
import jax.numpy as jnp
from jax import lax
from jax.experimental import pallas as pl
from jax.experimental.pallas import tpu as pltpu

F32 = jnp.float32
BF16 = jnp.bfloat16

LANES = 128
SUBLANES = 8
V7X_SCOPED_VMEM_BYTES = 60000 * 1024

HEAD_DIM = 64
CHUNK = 128
INTRA_PAIRS_PER_STEP = 4
SCAN_PAIRS_PER_STEP = 4
DECAY_SCALE = 0.606531
NORM_EPS = 1e-6
GN_EPS = HEAD_DIM * 1e-5
KK_EPS = 1e-24


def _vmem_limit(block_bytes):
    return int(min(V7X_SCOPED_VMEM_BYTES, 2 * block_bytes + (16 << 20)))


def _pick(n, pref, align):
    if n <= pref:
        return n
    best = None
    for d in range(align, pref + 1, align):
        if n % d == 0:
            best = d
    assert best is not None, (n, pref, align)
    return best


def _pad_to(x, axis, size):
    pad = size - x.shape[axis]
    if pad == 0:
        return x
    widths = [(0, 0)] * x.ndim
    widths[axis] = (0, pad)
    return jnp.pad(x, widths)


def _ceil_to(n, m):
    return -(-n // m) * m


def _dot(a, b):
    return jnp.dot(a, b, preferred_element_type=F32)


def _dot_nt(a, b):
    return lax.dot_general(a, b, (((1,), (1,)), ((), ())), preferred_element_type=F32)


def _dot_tn(a, b):
    return lax.dot_general(a, b, (((0,), (0,)), ((), ())), preferred_element_type=F32)


def _split2(x):
    hi = x.astype(BF16)
    lo = (x - hi.astype(F32)).astype(BF16)
    return hi, lo


def _split3(x):
    hi = x.astype(BF16)
    r1 = x - hi.astype(F32)
    mid = r1.astype(BF16)
    lo = (r1 - mid.astype(F32)).astype(BF16)
    return hi, mid, lo


def _iota2(shape, dim):
    return lax.broadcasted_iota(jnp.int32, shape, dim)


def _div_pow2(x, n):
    assert n & (n - 1) == 0, n
    return x >> (n.bit_length() - 1)


def _mod_pow2(x, n):
    assert n & (n - 1) == 0, n
    return x & (n - 1)


def _head_sum(x, e2):
    hi, lo = _split2(x)
    return _dot(jnp.concatenate([hi, lo], axis=1), e2)


def _same_head_matrix():
    r = _mod_pow2(_iota2((2 * LANES, LANES), 0), LANES)
    c = _iota2((2 * LANES, LANES), 1)
    return jnp.where(_div_pow2(r, HEAD_DIM) == _div_pow2(c, HEAD_DIM), 1.0, 0.0).astype(BF16)


def _rms_matmul_kernel(x_ref, g_ref, w_ref, o_ref, xn_ref):
    @pl.when(pl.program_id(1) == 0)
    def _():
        x = x_ref[...]
        ms = jnp.mean(x * x, axis=-1, keepdims=True)
        xn_ref[...] = (x * lax.rsqrt(ms + NORM_EPS) * g_ref[...]).astype(BF16)

    o_ref[...] = _dot(xn_ref[...], w_ref[...]).astype(o_ref.dtype)


def _rms_matmul(x, g, w, out_dtype, name):
    m, d = x.shape
    n = w.shape[1]
    bm = _pick(m, 512, SUBLANES)
    bn = _pick(n, 1024, 2 * LANES) if n % (2 * LANES) == 0 else _pick(n, 1024, LANES)
    blk = bm * d * 4 + d * bn * 2 + bm * bn * jnp.dtype(out_dtype).itemsize + bm * d * 2
    return pl.pallas_call(
        _rms_matmul_kernel,
        out_shape=jax.ShapeDtypeStruct((m, n), out_dtype),
        grid=(m // bm, n // bn),
        in_specs=[
            pl.BlockSpec((bm, d), lambda i, j: (i, 0)),
            pl.BlockSpec((1, d), lambda i, j: (0, 0)),
            pl.BlockSpec((d, bn), lambda i, j: (0, j)),
        ],
        out_specs=pl.BlockSpec((bm, bn), lambda i, j: (i, j)),
        scratch_shapes=[pltpu.VMEM((bm, d), BF16)],
        compiler_params=pltpu.CompilerParams(
            dimension_semantics=("parallel", "arbitrary"),
            vmem_limit_bytes=_vmem_limit(blk)),
        name=name,
    )(x, g, w)


def _wkv_intra_kernel(r_ref, k_ref, v_ref, xw_ref, xa_ref, xg_ref,
                      rp_ref, kp_ref, vp_ref, xwp_ref, xap_ref, xgp_ref,
                      rs_ref, ks_ref, vs_ref, xws_ref, xas_ref, xgs_ref,
                      mur_ref, muk_ref, muv_ref, muw_ref, mua_ref, mug_ref,
                      w0_ref, a0_ref, w2_ref, a2_ref, g2_ref,
                      kkw_ref, kaw_ref, rkw_ref,
                      g_out, h_out, rp_out, yl_out, gate_out, bonus_out,
                      *, t_real, n_pairs):
    c_idx = pl.program_id(1)
    C = CHUNK
    first = c_idx == 0

    def mixed(x_ref, p_ref, s_ref, mu_ref):
        x = x_ref[0]
        prev_last = jnp.where(first, s_ref[0], p_ref[0][SUBLANES - 1:SUBLANES])
        rolled = pltpu.roll(x, 1, 0)
        prev = jnp.where(_iota2(x.shape, 0) == 0, prev_last, rolled)
        y = x + (prev - x) * mu_ref[...]
        if t_real < C:
            y = jnp.concatenate([y, jnp.zeros((C - t_real, y.shape[1]), F32)], axis=0)
        return y

    r_all = mixed(r_ref, rp_ref, rs_ref, mur_ref)
    k_all = mixed(k_ref, kp_ref, ks_ref, muk_ref)
    v_all = mixed(v_ref, vp_ref, vs_ref, muv_ref)
    xw = jnp.tanh(mixed(xw_ref, xwp_ref, xws_ref, muw_ref)).astype(BF16)
    xa = mixed(xa_ref, xap_ref, xas_ref, mua_ref).astype(BF16)
    xg = jax.nn.sigmoid(mixed(xg_ref, xgp_ref, xgs_ref, mug_ref)).astype(BF16)

    lw_all = -DECAY_SCALE * jax.nn.sigmoid(w0_ref[...] + _dot(xw, w2_ref[...]))
    if t_real < C:
        lw_all = jnp.where(_iota2(lw_all.shape, 0) < t_real, lw_all, 0.0)
    iclr_all = jax.nn.sigmoid(a0_ref[...] + _dot(xa, a2_ref[...]))
    gate_all = _dot(xg, g2_ref[...])
    gate_out[0] = gate_all.astype(gate_out.dtype)

    e2 = _same_head_matrix()
    row = _iota2((C, C), 0)
    col = _iota2((C, C), 1)
    strict_lower = row > col
    lower = row >= col
    l_incl = jnp.where(lower, 1.0, 0.0).astype(BF16)
    l3 = jnp.concatenate([l_incl, l_incl, l_incl], axis=1)
    lane = _iota2((C, LANES), 1)
    head_masks = (lane < HEAD_DIM, lane >= HEAD_DIM)
    rr = _iota2((LANES, LANES), 0)
    cc = _iota2((LANES, LANES), 1)
    block_mask = _div_pow2(rr, HEAD_DIM) == _div_pow2(cc, HEAD_DIM)
    eye_mask = rr == cc

    pairs = range(n_pairs)
    sls = [slice(p * LANES, (p + 1) * LANES) for p in pairs]
    r = [r_all[:, sl] for sl in sls]
    k = [k_all[:, sl] for sl in sls]
    v = [v_all[:, sl] for sl in sls]
    lw = [lw_all[:, sl] for sl in sls]
    iclr = [iclr_all[:, sl] for sl in sls]

    kkr = [k[p] * kkw_ref[:, sls[p]] for p in pairs]
    k_mod = [k[p] * (1.0 + (iclr[p] - 1.0) * kaw_ref[:, sls[p]]) for p in pairs]
    kk_ss = [_head_sum(kkr[p] * kkr[p], e2) for p in pairs]
    rk_sum = [_head_sum(r[p] * k_mod[p] * rkw_ref[:, sls[p]], e2) for p in pairs]
    cum = [_dot(l3, jnp.concatenate(_split3(lw[p]), axis=0)) for p in pairs]

    vb, em, ecl, kbar, bbar, rhs_scores, lhs_scores, am_b, rm_f = [], [], [], [], [], [], [], [], []
    for p in pairs:
        bonus_out[0, :, sls[p]] = rk_sum[p] * v[p]
        kk = kkr[p] * lax.rsqrt(jnp.maximum(kk_ss[p], KK_EPS))
        b = kk * iclr[p]
        m_row = cum[p][C // 2 - 1:C // 2]
        cum_last = cum[p][C - 1:C]
        g = cum[p] - m_row
        eng = jnp.exp(-g)
        ebar = jnp.exp(cum_last - cum[p])
        at = -kk * jnp.exp(g - lw[p])
        rt = r[p] * jnp.exp(g)
        kbar.append((k_mod[p] * ebar).astype(BF16))
        bbar.append((b * ebar).astype(BF16))
        em.append(jnp.exp(m_row))
        ecl.append(jnp.exp(cum_last))
        vb.append(v[p].astype(BF16))
        rhs_scores.append(jnp.concatenate([(k_mod[p] * eng).astype(BF16), (b * eng).astype(BF16)], axis=0))
        stack = []
        for mh in head_masks:
            am_b.append(jnp.where(mh, at, 0.0).astype(BF16))
            rm_f.append(jnp.where(mh, rt, 0.0))
            stack += [am_b[-1], rm_f[-1].astype(BF16)]
        lhs_scores.append(jnp.concatenate(stack, axis=0))

    sc = [_dot_nt(lhs_scores[p], rhs_scores[p]) for p in pairs]
    heads = range(2 * n_pairs)
    m_ak, m_ab, p_rk_b, p_rb_b = [], [], [], []
    for i in heads:
        s = sc[i // 2][(i % 2) * 2 * C:(i % 2 + 1) * 2 * C]
        m_ak.append(jnp.where(strict_lower, s[:C, :C], 0.0).astype(BF16))
        m_ab.append(jnp.where(strict_lower, s[:C, C:], 0.0))
        p_rk_b.append(jnp.where(lower, s[C:, :C], 0.0).astype(BF16))
        p_rb_b.append(jnp.where(lower, s[C:, C:], 0.0).astype(BF16))

    eye = jnp.where(row == col, 1.0, 0.0).astype(F32)
    t = [eye + m_ab[i] for i in heads]
    qb = [m_ab[i].astype(BF16) for i in heads]
    q = [_dot(qb[i], qb[i]) for i in heads]
    w1 = [_dot(m_ak[i], vb[i // 2]).astype(BF16) for i in heads]
    for _ in range(int(math.log2(C)) - 2):
        qb = [q[i].astype(BF16) for i in heads]
        st = [_dot(jnp.concatenate([t[i].astype(BF16), qb[i]], axis=0), qb[i]) for i in heads]
        t = [t[i] + st[i][:C] for i in heads]
        q = [st[i][C:] for i in heads]
    corr = [_dot(t[i].astype(BF16), q[i].astype(BF16)) for i in heads]
    t_inv = [(t[i] + corr[i]).astype(BF16) for i in heads]
    tu = [_dot(t_inv[i], jnp.concatenate([w1[i], am_b[i]], axis=1)) for i in heads]
    ul = [tu[i][:, :LANES] for i in heads]
    ap = [tu[i][:, LANES:] for i in heads]
    yl = [_dot(jnp.concatenate([p_rk_b[i], p_rb_b[i]], axis=1),
               jnp.concatenate([vb[i // 2], ul[i].astype(BF16)], axis=0)) for i in heads]
    rp_add = [_dot(p_rb_b[i], ap[i].astype(BF16)) for i in heads]

    m1 = head_masks[1]
    ul_pair = [jnp.where(m1, ul[2 * p + 1], ul[2 * p]).astype(BF16) for p in pairs]
    ap_pair = [(ap[2 * p] + ap[2 * p + 1]).astype(BF16) for p in pairs]
    ba = [_dot_tn(bbar[p], ap_pair[p]) for p in pairs]
    hh = [_dot_tn(jnp.concatenate([kbar[p], bbar[p]], axis=0),
                  jnp.concatenate([vb[p], ul_pair[p]], axis=0)) for p in pairs]
    for p in pairs:
        sl = sls[p]
        g_out[0, 0, :, sl] = jnp.where(block_mask, ba[p] * em[p], 0.0) + jnp.where(eye_mask, ecl[p], 0.0)
        h_out[0, 0, :, sl] = jnp.where(block_mask, hh[p], 0.0)
        rp_pair = rm_f[2 * p] + rp_add[2 * p] + rm_f[2 * p + 1] + rp_add[2 * p + 1]
        rp_out[0, :, sl] = (rp_pair * em[p]).astype(rp_out.dtype)
        yl_out[0, :, sl] = jnp.where(m1, yl[2 * p + 1], yl[2 * p])


def _wkv_intra(p3, shift, mu, w0, a0, w2, a2, g2, kkw, kaw, rkw, *, rw, dlp, ilp, glp):
    bsz, t, npc = p3.shape
    C = CHUNK
    t_real = min(t, C)
    assert t % t_real == 0 and t_real % SUBLANES == 0
    nch = t // t_real
    tp = nch * C
    n_pairs = _pick(rw // LANES, INTRA_PAIRS_PER_STEP, 1)
    lw_ = n_pairs * LANES
    ng = rw // lw_
    o_w, o_a, o_g = 3 * rw, 3 * rw + dlp, 3 * rw + dlp + ilp
    assert o_w % dlp == 0 and o_a % ilp == 0 and o_g % glp == 0
    rows_prev = t_real // SUBLANES

    def cur(width, off):
        return pl.BlockSpec((1, t_real, width), lambda b, c, g, o=off // width: (b, c, o))

    def cur_g(width, off):
        return pl.BlockSpec((1, t_real, width), lambda b, c, g, o=off // width: (b, c, o + g))

    def prev(width, off):
        return pl.BlockSpec((1, SUBLANES, width),
                            lambda b, c, g, o=off // width: (b, jnp.maximum(c * rows_prev - 1, 0), o))

    def prev_g(width, off):
        return pl.BlockSpec((1, SUBLANES, width),
                            lambda b, c, g, o=off // width: (b, jnp.maximum(c * rows_prev - 1, 0), o + g))

    def sh(width, off):
        return pl.BlockSpec((1, 1, width), lambda b, c, g, o=off // width: (b, 0, o))

    def sh_g(width, off):
        return pl.BlockSpec((1, 1, width), lambda b, c, g, o=off // width: (b, 0, o + g))

    def vec(width, off):
        return pl.BlockSpec((1, width), lambda b, c, g, o=off // width: (0, o))

    def vec_g(width, off=0):
        return pl.BlockSpec((1, width), lambda b, c, g, o=off // width: (0, o + g))

    def mat_g(rows):
        return pl.BlockSpec((rows, lw_), lambda b, c, g: (0, g))

    in_specs = (
        [cur_g(lw_, 0), cur_g(lw_, rw), cur_g(lw_, 2 * rw), cur(dlp, o_w), cur(ilp, o_a), cur(glp, o_g)]
        + [prev_g(lw_, 0), prev_g(lw_, rw), prev_g(lw_, 2 * rw), prev(dlp, o_w), prev(ilp, o_a), prev(glp, o_g)]
        + [sh_g(lw_, 0), sh_g(lw_, rw), sh_g(lw_, 2 * rw), sh(dlp, o_w), sh(ilp, o_a), sh(glp, o_g)]
        + [vec_g(lw_, 0), vec_g(lw_, rw), vec_g(lw_, 2 * rw), vec(dlp, o_w), vec(ilp, o_a), vec(glp, o_g)]
        + [vec_g(lw_), vec_g(lw_), mat_g(dlp), mat_g(ilp), mat_g(glp)]
        + [vec_g(lw_), vec_g(lw_), vec_g(lw_)]
    )
    args = ([p3] * 6 + [p3] * 6 + [shift] * 6 + [mu] * 6 + [w0, a0, w2, a2, g2, kkw, kaw, rkw])
    out_shape = (
        jax.ShapeDtypeStruct((bsz, nch, LANES, rw), F32),
        jax.ShapeDtypeStruct((bsz, nch, LANES, rw), F32),
        jax.ShapeDtypeStruct((bsz, tp, rw), BF16),
        jax.ShapeDtypeStruct((bsz, tp, rw), F32),
        jax.ShapeDtypeStruct((bsz, tp, rw), BF16),
        jax.ShapeDtypeStruct((bsz, tp, rw), F32),
    )
    gh_spec = pl.BlockSpec((1, 1, LANES, lw_), lambda b, c, g: (b, c, 0, g))
    tok_spec = pl.BlockSpec((1, C, lw_), lambda b, c, g: (b, c, g))
    blk = (t_real * (3 * lw_ + dlp + ilp + glp) * 4 + (dlp + ilp + glp) * lw_ * 2
           + 2 * LANES * lw_ * 4 + C * lw_ * 12)
    return pl.pallas_call(
        functools.partial(_wkv_intra_kernel, t_real=t_real, n_pairs=n_pairs),
        out_shape=out_shape,
        grid=(bsz, nch, ng),
        in_specs=in_specs,
        out_specs=(gh_spec, gh_spec, tok_spec, tok_spec, tok_spec, tok_spec),
        compiler_params=pltpu.CompilerParams(
            dimension_semantics=("parallel", "parallel", "parallel"),
            vmem_limit_bytes=_vmem_limit(blk)),
        name="wkv_intra",
    )(*args)


def _wkv_scan_kernel(g_ref, h_ref, rp_ref, yl_ref, gate_ref, bonus_ref, s0_ref, lng_ref, lnb_ref,
                     y_ref, s_out_ref, *, n_chunks, n_pairs):
    C = CHUNK
    e2 = _same_head_matrix()
    inv_n = 1.0 / HEAD_DIM
    lanes = [slice(p * LANES, (p + 1) * LANES) for p in range(n_pairs)]

    s_out_ref[...] = s0_ref[...]

    def body(c, carry):
        rows = pl.ds(pl.multiple_of(c * C, C), C)
        sb = [s_out_ref[0, :, sl].astype(BF16) for sl in lanes]
        y = [_dot(rp_ref[0, rows, sl], s) for sl, s in zip(lanes, sb)]
        s_new = [_dot(g_ref[0, c, :, sl].astype(BF16), s) for sl, s in zip(lanes, sb)]
        for sl, s in zip(lanes, s_new):
            s_out_ref[0, :, sl] = s + h_ref[0, c, :, sl]
        y = [yy + yl_ref[0, rows, sl] for sl, yy in zip(lanes, y)]
        mu = [_head_sum(yy, e2) * inv_n for yy in y]
        yc = [yy - m for yy, m in zip(y, mu)]
        var = [_head_sum(x * x, e2) * inv_n for x in yc]
        for sl, x, vv in zip(lanes, yc, var):
            yn = x * lax.rsqrt(vv + GN_EPS) * lng_ref[:, sl] + lnb_ref[:, sl]
            out = (yn + bonus_ref[0, rows, sl]) * gate_ref[0, rows, sl].astype(F32)
            y_ref[0, rows, sl] = out.astype(y_ref.dtype)
        return carry

    lax.fori_loop(0, n_chunks, body, 0)


def _wkv_scan(g, h, rp, yl, gate, bonus, s0, lng, lnb):
    bsz, nch, _, rw = g.shape
    tp = rp.shape[1]
    n_pairs = _pick(rw // LANES, SCAN_PAIRS_PER_STEP, 1)
    lw_ = n_pairs * LANES
    gh_spec = pl.BlockSpec((1, nch, LANES, lw_), lambda b, q: (b, 0, 0, q))
    tok_spec = pl.BlockSpec((1, tp, lw_), lambda b, q: (b, 0, q))
    st_spec = pl.BlockSpec((1, LANES, lw_), lambda b, q: (b, 0, q))
    vec_spec = pl.BlockSpec((1, lw_), lambda b, q: (0, q))
    blk = 2 * nch * LANES * lw_ * 4 + tp * lw_ * (2 + 4 + 2 + 4 + 2) + 2 * LANES * lw_ * 4
    return pl.pallas_call(
        functools.partial(_wkv_scan_kernel, n_chunks=nch, n_pairs=n_pairs),
        out_shape=(jax.ShapeDtypeStruct((bsz, tp, rw), BF16),
                   jax.ShapeDtypeStruct((bsz, LANES, rw), F32)),
        grid=(bsz, rw // lw_),
        in_specs=[gh_spec, gh_spec, tok_spec, tok_spec, tok_spec, tok_spec, st_spec, vec_spec, vec_spec],
        out_specs=(tok_spec, st_spec),
        compiler_params=pltpu.CompilerParams(
            dimension_semantics=("parallel", "parallel"),
            vmem_limit_bytes=_vmem_limit(blk)),
        name="wkv_scan",
    )(g, h, rp, yl, gate, bonus, s0, lng, lnb)


def _cumsum_rhs():
    r = _mod_pow2(_iota2((2 * LANES, 2 * LANES), 0), LANES)
    c = _iota2((2 * LANES, 2 * LANES), 1)
    return jnp.where((c >= LANES) | (r > c), 1.0, 0.0).astype(BF16)


def _sb_tile(z, v_tile, run, acc, w2, mask):
    lk = -(jnp.maximum(z, 0.0) + jnp.log1p(jnp.exp(-jnp.abs(z))))
    if mask is not None:
        lk = jnp.where(mask, lk, 0.0)
    hi, lo = _split2(lk)
    cs2 = _dot(jnp.concatenate([hi, lo], axis=1), w2)
    att = jnp.exp(z + lk + cs2[:, :LANES] + run)
    if mask is not None:
        att = jnp.where(mask, att, 0.0)
    acc = acc + _dot(att.astype(BF16), v_tile)
    return run + cs2[:, LANES:], acc


def _attn_prompt_kernel(q_ref, k_ref, v_ref, bias_ref, o_ref, kb_ref, vb_ref, run_ref, acc_ref):
    qi = pl.program_id(2)
    qb = q_ref.shape[1]
    kb = 2 * LANES
    assert qb == kb

    @pl.when(qi == 0)
    def _():
        kb_ref[...] = k_ref[0].astype(BF16)
        vb_ref[...] = v_ref[0].astype(BF16)

    q = q_ref[0]
    m0 = _iota2((qb, LANES), 1) < HEAD_DIM
    zero = jnp.zeros((), BF16)
    qs = jnp.concatenate([jnp.where(m0, q, zero), jnp.where(m0, zero, q)], axis=0)
    b0 = jnp.concatenate([bias_ref[0, 0:1, :]] * 2, axis=1)
    b1 = jnp.concatenate([bias_ref[0, 1:2, :]] * 2, axis=1)
    w2 = _cumsum_rhs()
    run_ref[...] = jnp.zeros(run_ref.shape, F32)
    acc_ref[...] = jnp.zeros(acc_ref.shape, F32)

    def step(j, mask):
        rows = pl.ds(pl.multiple_of(j * kb, kb), kb)
        z = _dot_nt(qs, kb_ref[rows, :])
        z = jnp.concatenate([z[:qb] + b0, z[qb:] + b1], axis=0)
        lk = -(jnp.maximum(z, 0.0) + jnp.log1p(jnp.exp(-jnp.abs(z))))
        if mask is not None:
            lk = jnp.where(mask, lk, 0.0)
        hi, lo = _split2(lk)
        cs_r = _dot(jnp.concatenate([hi[:, LANES:], lo[:, LANES:]], axis=1), w2)
        cs_l = _dot(jnp.concatenate([hi[:, :LANES], lo[:, :LANES]], axis=1), w2)
        run = run_ref[...]
        tot_r = cs_r[:, LANES:]
        e_r = z[:, LANES:] + lk[:, LANES:] + cs_r[:, :LANES] + run
        e_l = z[:, :LANES] + lk[:, :LANES] + cs_l[:, :LANES] + (run + tot_r)
        att = jnp.exp(jnp.concatenate([e_l, e_r], axis=1))
        if mask is not None:
            att = jnp.where(mask, att, 0.0)
        acc_ref[...] += _dot(att.astype(BF16), vb_ref[rows, :])
        run_ref[...] = run + tot_r + cs_l[:, LANES:]

    qpos = _mod_pow2(_iota2((2 * qb, kb), 0), qb)
    step(qi, _iota2((2 * qb, kb), 1) < qpos)

    def body(jj, carry):
        step(qi - 1 - jj, None)
        return carry

    lax.fori_loop(0, qi, body, 0)
    o_ref[0] = jnp.where(m0, acc_ref[:qb], acc_ref[qb:]).astype(o_ref.dtype)


def _attn_prompt(q, k, v, bias2):
    bsz, t, w = q.shape
    qb = 2 * LANES
    assert t % qb == 0
    blk = qb * LANES * 2 * 2 + 2 * t * LANES * 4
    return pl.pallas_call(
        _attn_prompt_kernel,
        out_shape=jax.ShapeDtypeStruct((bsz, t, w), BF16),
        grid=(bsz, w // LANES, t // qb),
        in_specs=[
            pl.BlockSpec((1, qb, LANES), lambda b, p, i: (b, i, p)),
            pl.BlockSpec((1, t, LANES), lambda b, p, i: (b, 0, p)),
            pl.BlockSpec((1, t, LANES), lambda b, p, i: (b, 0, p)),
            pl.BlockSpec((1, 2, LANES), lambda b, p, i: (p, 0, 0)),
        ],
        out_specs=pl.BlockSpec((1, qb, LANES), lambda b, p, i: (b, i, p)),
        scratch_shapes=[pltpu.VMEM((t, LANES), BF16), pltpu.VMEM((t, LANES), BF16),
                        pltpu.VMEM((2 * qb, LANES), F32), pltpu.VMEM((2 * qb, LANES), F32)],
        compiler_params=pltpu.CompilerParams(
            dimension_semantics=("parallel", "parallel", "arbitrary"),
            vmem_limit_bytes=_vmem_limit(blk + 2 * t * LANES * 2)),
        name="attn_prompt",
    )(q, k, v, bias2)


PAGES_PER_STEP = 4


def _attn_paged_kernel(pt_ref, *refs, n_groups):
    del pt_ref
    pps = PAGES_PER_STEP
    k_refs = refs[:pps]
    v_refs = refs[pps:2 * pps]
    q_ref, kn_ref, vn_ref, bias_ref, o_ref, run_ref, acc_ref = refs[2 * pps:]
    g = pl.program_id(1)
    rows_n = q_ref.shape[1]
    t_new = o_ref.shape[1]
    w2 = _cumsum_rhs()
    q = q_ref[0]
    bias = bias_ref[...]

    n_heads = rows_n // t_new
    page = LANES

    def compact(ref):
        cols = []
        for hp in range(n_heads // 2):
            groups = []
            for s in range(page // SUBLANES):
                base = s * SUBLANES * n_heads + 2 * hp
                even = ref[0, pl.ds(base, SUBLANES, stride=n_heads), :]
                odd = ref[0, pl.ds(base + 1, SUBLANES, stride=n_heads), :]
                groups.append(jnp.concatenate([even, odd], axis=1))
            cols.append(jnp.concatenate(groups, axis=0).astype(BF16))
        return jnp.concatenate(cols, axis=1)

    @pl.when(g == 0)
    def _():
        lane = _iota2((rows_n, LANES), 1)
        row = _iota2((rows_n, LANES), 0)
        mask = lane < _mod_pow2(row, t_new)
        z = _dot_nt(q, kn_ref[0]) + bias
        run, acc = _sb_tile(z, vn_ref[0], jnp.zeros((rows_n, LANES), F32),
                            jnp.zeros(acc_ref.shape, F32), w2, mask)
        run_ref[...] = run
        acc_ref[...] = acc

    pages = range(pps)
    kc = [compact(k_refs[p]) for p in pages]
    z = [_dot_nt(q, kc[p]) + bias for p in pages]
    lk = [-(jnp.maximum(zz, 0.0) + jnp.log1p(jnp.exp(-jnp.abs(zz)))) for zz in z]
    cs2 = [_dot(jnp.concatenate(_split2(x), axis=1), w2) for x in lk]
    run = run_ref[...]
    att = []
    for p in pages:
        att.append(jnp.exp(z[p] + lk[p] + cs2[p][:, :LANES] + run).astype(BF16))
        run = run + cs2[p][:, LANES:]
    run_ref[...] = run
    vc = jnp.concatenate([compact(v_refs[p]) for p in pages], axis=0)
    acc_ref[...] += _dot(jnp.concatenate(att, axis=1), vc)

    @pl.when(g == n_groups - 1)
    def _():
        acc = acc_ref[...]
        rr = _iota2(acc.shape, 0)
        cc = _iota2(acc.shape, 1)
        picked = jnp.where(_div_pow2(rr, t_new) == _div_pow2(cc, HEAD_DIM), acc, 0.0)
        out = picked[0:t_new]
        for h in range(1, rows_n // t_new):
            out = out + picked[h * t_new:(h + 1) * t_new]
        o_ref[0] = out.astype(o_ref.dtype)


def _attn_paged(q_bd, k_new, v_new, bias_rows, cache_k, cache_v, page_table, t_new):
    bsz, rows_n, w = q_bd.shape
    n_pages = page_table.shape[1]
    n_heads = w // HEAD_DIM
    page = cache_k.shape[1] // n_heads
    pps = PAGES_PER_STEP
    assert page == LANES and n_pages % pps == 0 and cache_k.shape[2] == HEAD_DIM
    n_groups = n_pages // pps

    def page_spec(p):
        return pl.BlockSpec((1, page * n_heads, HEAD_DIM),
                            lambda b, g, pt, p=p % pps: (pt[b, n_pages - 1 - (g * pps + p)], 0, 0))

    in_specs = ([page_spec(p) for p in range(2 * pps)] + [
        pl.BlockSpec((1, rows_n, w), lambda b, g, pt: (b, 0, 0)),
        pl.BlockSpec((1, LANES, w), lambda b, g, pt: (b, 0, 0)),
        pl.BlockSpec((1, LANES, w), lambda b, g, pt: (b, 0, 0)),
        pl.BlockSpec((rows_n, LANES), lambda b, g, pt: (0, 0)),
    ])
    blk = 2 * pps * page * n_heads * LANES * 4 + rows_n * w * 2 + 2 * LANES * w * 2 + rows_n * w * 4
    return pl.pallas_call(
        functools.partial(_attn_paged_kernel, n_groups=n_groups),
        out_shape=jax.ShapeDtypeStruct((bsz, t_new, w), BF16),
        grid_spec=pltpu.PrefetchScalarGridSpec(
            num_scalar_prefetch=1,
            grid=(bsz, n_groups),
            in_specs=in_specs,
            out_specs=pl.BlockSpec((1, t_new, w), lambda b, g, pt: (b, 0, 0)),
            scratch_shapes=[pltpu.VMEM((rows_n, LANES), F32), pltpu.VMEM((rows_n, w), F32)],
        ),
        compiler_params=pltpu.CompilerParams(
            dimension_semantics=("parallel", "arbitrary"),
            vmem_limit_bytes=_vmem_limit(blk)),
        name="attn_paged",
    )(page_table, *([cache_k] * pps), *([cache_v] * pps), q_bd, k_new, v_new, bias_rows)


def _rms(x, g):
    ms = jnp.mean(x * x, axis=-1, keepdims=True)
    return x * lax.rsqrt(ms + NORM_EPS) * g


def _merge_kernel(yr_ref, os_ref, gr_ref, gs_ref, x_ref, wr_ref, ws_ref, wo_ref, g_ref, h_ref, hn_ref):
    a = _dot(yr_ref[...], wr_ref[...])
    b = _dot(os_ref[...], ws_ref[...])
    mixed = (jax.nn.sigmoid(gr_ref[...].astype(F32)) * a
             + jax.nn.sigmoid(gs_ref[...].astype(F32)) * b)
    h = x_ref[...] + _dot(mixed.astype(BF16), wo_ref[...])
    h_ref[...] = h
    hn_ref[...] = _rms(h, g_ref[...]).astype(hn_ref.dtype)


def _resident(shape):
    return pl.BlockSpec(shape, lambda *_: (0,) * len(shape), pipeline_mode=pl.Buffered(1))


def _merge(yr, os_, gates, x, wr, ws, wo, g_ffn):
    m, d = x.shape
    rw = yr.shape[1]
    bm = _pick(m, 256, SUBLANES)
    blk = bm * (2 * rw * 2 + 2 * d * 2 + d * 4 + d * 4 + d * 2) + (2 * rw * d + d * d)
    return pl.pallas_call(
        _merge_kernel,
        out_shape=(jax.ShapeDtypeStruct((m, d), F32), jax.ShapeDtypeStruct((m, d), BF16)),
        grid=(m // bm,),
        in_specs=[
            pl.BlockSpec((bm, rw), lambda i: (i, 0)),
            pl.BlockSpec((bm, rw), lambda i: (i, 0)),
            pl.BlockSpec((bm, d), lambda i: (i, 0)),
            pl.BlockSpec((bm, d), lambda i: (i, 1)),
            pl.BlockSpec((bm, d), lambda i: (i, 0)),
            _resident(wr.shape), _resident(ws.shape), _resident(wo.shape),
            pl.BlockSpec((1, d), lambda i: (0, 0)),
        ],
        out_specs=(pl.BlockSpec((bm, d), lambda i: (i, 0)), pl.BlockSpec((bm, d), lambda i: (i, 0))),
        compiler_params=pltpu.CompilerParams(
            dimension_semantics=("parallel",),
            vmem_limit_bytes=_vmem_limit(blk + bm * d * 16)),
        name="merge",
    )(yr, os_, gates, gates, x, wr, ws, wo, g_ffn)


def _gelu_tanh(x):
    return 0.5 * x * (1.0 + jnp.tanh(math.sqrt(2.0 / math.pi) * (x + 0.044715 * x * x * x)))


def _ffn_kernel(hn_ref, wg_ref, wv_ref, cw_ref, wd_ref, prev_ref, o_ref, tail_ref, carry_ref,
                *, blocks_per_seq, seq_len):
    i = pl.program_id(0)
    f = pl.program_id(1)
    hn = hn_ref[...]
    ug = _dot(hn, wg_ref[...])
    uv = _dot(hn, wv_ref[...])
    bm = ug.shape[0]
    row = _iota2(ug.shape, 0)
    r1 = pltpu.roll(ug, 1, 0)
    r2 = pltpu.roll(ug, 2, 0)
    if seq_len == SUBLANES:
        p2 = prev_ref[...]
        p1 = pltpu.roll(p2, bm - 1, 0)
        t = _mod_pow2(row, SUBLANES)
        s1 = jnp.where(t == 0, p1, r1)
        s2 = jnp.where(t < 2, p2, r2)
        tail_ref[...] = ug
    else:
        first = (i % blocks_per_seq) == 0
        pv = jnp.where(first, prev_ref[0], carry_ref[f])
        head = _iota2((SUBLANES, ug.shape[1]), 0)
        s1 = jnp.concatenate(
            [jnp.where(head == 0, pv[7:8], r1[:SUBLANES]), r1[SUBLANES:]], axis=0)
        s2 = jnp.concatenate(
            [jnp.where(head == 0, pv[6:7], jnp.where(head == 1, pv[7:8], r2[:SUBLANES])),
             r2[SUBLANES:]], axis=0)
        carry_ref[f] = ug[bm - SUBLANES:]
        tail_ref[0] = ug[bm - SUBLANES:]
    conv = cw_ref[0:1, :] * s2 + cw_ref[1:2, :] * s1 + cw_ref[2:3, :] * ug
    act = (_gelu_tanh(conv) * uv).astype(BF16)
    part = _dot(act, wd_ref[...])

    @pl.when(f == 0)
    def _():
        o_ref[...] = part

    @pl.when(f > 0)
    def _():
        o_ref[...] += part


def _ffn(hn, w_up, conv_w, w_down, prev8, seq_len):
    m, d = hn.shape
    ff = w_down.shape[0]
    bf = _pick(ff, 512, 2 * LANES) if ff % (2 * LANES) == 0 else _pick(ff, 512, LANES)
    nf = ff // bf
    if seq_len == SUBLANES:
        bm = m
        blocks_per_seq = 1
        prev_spec = pl.BlockSpec((bm, bf), lambda i, f: (i, f))
        tail_shape = (m, ff)
        tail_spec = pl.BlockSpec((bm, bf), lambda i, f: (i, f))
    else:
        bm = _pick(seq_len, 1024, SUBLANES)
        blocks_per_seq = seq_len // bm
        prev_spec = pl.BlockSpec((1, SUBLANES, bf), lambda i, f: (i // blocks_per_seq, 0, f))
        tail_shape = (m // bm, SUBLANES, ff)
        tail_spec = pl.BlockSpec((1, SUBLANES, bf), lambda i, f: (i, 0, f))
    blk = bm * d * (2 + 4) + 3 * d * bf * 2 + 4 * bm * bf * 4
    out, tail = pl.pallas_call(
        functools.partial(_ffn_kernel, blocks_per_seq=blocks_per_seq, seq_len=seq_len),
        out_shape=(jax.ShapeDtypeStruct((m, d), F32), jax.ShapeDtypeStruct(tail_shape, F32)),
        grid=(m // bm, nf),
        in_specs=[
            pl.BlockSpec((bm, d), lambda i, f: (i, 0)),
            pl.BlockSpec((d, bf), lambda i, f: (0, f)),
            pl.BlockSpec((d, bf), lambda i, f: (0, nf + f)),
            pl.BlockSpec((3, bf), lambda i, f: (0, f)),
            pl.BlockSpec((bf, d), lambda i, f: (f, 0)),
            prev_spec,
        ],
        out_specs=(pl.BlockSpec((bm, d), lambda i, f: (i, 0)), tail_spec),
        scratch_shapes=[pltpu.VMEM((nf, SUBLANES, bf), F32)],
        compiler_params=pltpu.CompilerParams(
            dimension_semantics=("arbitrary", "arbitrary"),
            vmem_limit_bytes=_vmem_limit(blk)),
        name="ffn",
    )(hn, w_up, w_up, conv_w, w_down, prev8)
    n_seq = m // seq_len
    if seq_len == SUBLANES:
        return out, tail.reshape(n_seq, SUBLANES, ff)
    return out, tail.reshape(n_seq, blocks_per_seq, SUBLANES, ff)[:, -1]


def _ple_kernel(h_ref, f_ref, pe_ref, wple_ref, wpg_ref, gp_ref, gf_ref, y_ref, *, final_norm):
    h2 = h_ref[...] + f_ref[...]
    hn = _rms(h2, gp_ref[...]).astype(BF16)
    gate = jax.nn.sigmoid(_dot(hn, wpg_ref[...]))
    emb = _dot(pe_ref[...].astype(BF16), wple_ref[...])
    h3 = h2 + emb * gate
    y_ref[...] = _rms(h3, gf_ref[...]) if final_norm else h3


def _ple_out(h, ffn_out, pe, w_ple, w_pg, g_ple, g_final, final_norm):
    m, d = h.shape
    pd = pe.shape[1]
    bm = _pick(m, 512, SUBLANES)
    blk = bm * (3 * d * 4 + pd * 4) + pd * d * 2 + d * d * 2
    return pl.pallas_call(
        functools.partial(_ple_kernel, final_norm=final_norm),
        out_shape=jax.ShapeDtypeStruct((m, d), F32),
        grid=(m // bm,),
        in_specs=[
            pl.BlockSpec((bm, d), lambda i: (i, 0)),
            pl.BlockSpec((bm, d), lambda i: (i, 0)),
            pl.BlockSpec((bm, pd), lambda i: (i, 0)),
            _resident(w_ple.shape), _resident(w_pg.shape),
            pl.BlockSpec((1, d), lambda i: (0, 0)),
            pl.BlockSpec((1, d), lambda i: (0, 0)),
        ],
        out_specs=pl.BlockSpec((bm, d), lambda i: (i, 0)),
        compiler_params=pltpu.CompilerParams(
            dimension_semantics=("parallel",),
            vmem_limit_bytes=_vmem_limit(blk + bm * d * 8)),
        name="ple_out",
    )(h, ffn_out, pe, w_ple, w_pg, g_ple, g_final)


def _prep_weights(g_mix, w_in, mu_shift, w0, w2, a0, a2, g2, k_k, k_a, r_k, ln_x_g, ln_x_b,
                  w_br_r, w_br_s, w_o, g_ffn, w_up, conv_w, w_down, g_ple, w_ple, w_pg, sb_bias):
    d = w_in.shape[0]
    rw = w0.shape[0]
    dl, il, gl = w2.shape[0], a2.shape[0], g2.shape[0]
    dlp, ilp, glp = (_ceil_to(n, LANES) for n in (dl, il, gl))
    sw = w_br_s.shape[0]
    rc = 3 * rw + dl + il + gl
    c_w, c_a, c_g = 3 * rw, 3 * rw + dl, 3 * rw + dl + il

    def regroup(x):
        return jnp.concatenate([
            x[..., :c_w], _pad_to(x[..., c_w:c_a], -1, dlp), _pad_to(x[..., c_a:c_g], -1, ilp),
            _pad_to(x[..., c_g:rc], -1, glp)], axis=-1)

    w_r = regroup(w_in[:, :rc]).astype(BF16)
    scale = HEAD_DIM ** -0.5
    w_q = (w_in[:, rc:rc + sw] * scale).astype(BF16)
    w_k = w_in[:, rc + sw:rc + 2 * sw].astype(BF16)
    w_v = w_in[:, rc + 2 * sw:rc + 3 * sw].astype(BF16)
    w_g = w_in[:, rc + 3 * sw:].astype(BF16)
    nh_s = sw // HEAD_DIM
    return dict(
        d=d, rw=rw, sw=sw, dl=dl, il=il, gl=gl, dlp=dlp, ilp=ilp, glp=glp, rc=rc,
        regroup=regroup,
        g_mix=g_mix.reshape(1, d), w_r=w_r, w_q=w_q, w_k=w_k, w_v=w_v, w_g=w_g,
        mu=regroup(mu_shift).reshape(1, -1),
        w0=w0.reshape(1, rw), a0=a0.reshape(1, rw),
        w2=_pad_to(w2, 0, dlp).astype(BF16), a2=_pad_to(a2, 0, ilp).astype(BF16),
        g2=_pad_to(g2, 0, glp).astype(BF16),
        kkw=k_k.reshape(1, rw), kaw=k_a.reshape(1, rw), rkw=r_k.reshape(1, rw),
        lng=ln_x_g.reshape(1, rw), lnb=ln_x_b.reshape(1, rw),
        w_br_r=w_br_r.astype(BF16), w_br_s=w_br_s.astype(BF16), w_o=w_o.astype(BF16),
        g_ffn=g_ffn.reshape(1, d), w_up=w_up.astype(BF16), conv_w=conv_w,
        w_down=w_down.astype(BF16), g_ple=g_ple.reshape(1, d),
        w_ple=w_ple.astype(BF16), w_pg=w_pg.astype(BF16),
        bias2=jnp.broadcast_to(sb_bias.reshape(nh_s // 2, 2, 1), (nh_s // 2, 2, LANES)).astype(F32),
        sb_bias=sb_bias,
    )


def _state_to_blockdiag(s):
    bsz, nh, n, _ = s.shape
    st = jnp.swapaxes(s, -1, -2).reshape(bsz, nh // 2, 2, n, n)
    eye = jnp.eye(2, dtype=s.dtype)
    bd = st[:, :, :, :, None, :] * eye[None, None, :, None, :, None]
    bd = bd.reshape(bsz, nh // 2, 2 * n, 2 * n)
    return jnp.swapaxes(bd, 1, 2).reshape(bsz, 2 * n, nh * n)


def _blockdiag_to_state(bd, nh):
    bsz = bd.shape[0]
    n = HEAD_DIM
    x = bd.reshape(bsz, 2, n, nh // 2, 2, n)
    heads = [x[:, h, :, :, h, :] for h in range(2)]
    st = jnp.stack(heads, axis=3)
    st = jnp.transpose(st, (0, 2, 3, 4, 1))
    return st.reshape(bsz, nh, n, n)


def _layer(x3, pe3, shift_prev, wkv_prev, conv_prev, past, wp, g_final, final_norm):
    bsz, t, d = x3.shape
    m = bsz * t
    rw, sw = wp["rw"], wp["sw"]
    nh = rw // HEAD_DIM
    x = x3.reshape(m, d)

    p_r = _rms_matmul(x, wp["g_mix"], wp["w_r"], F32, "proj_r")
    q = _rms_matmul(x, wp["g_mix"], wp["w_q"], BF16, "proj_q")
    k_s = _rms_matmul(x, wp["g_mix"], wp["w_k"], F32, "proj_k")
    v_s = _rms_matmul(x, wp["g_mix"], wp["w_v"], F32, "proj_v")
    gates = _rms_matmul(x, wp["g_mix"], wp["w_g"], BF16, "proj_g")

    p3 = p_r.reshape(bsz, t, -1)
    shift = wp["regroup"](shift_prev).reshape(bsz, 1, -1)
    g_c, h_c, rp, yl, gate, bonus = _wkv_intra(
        p3, shift, wp["mu"], wp["w0"], wp["a0"], wp["w2"], wp["a2"], wp["g2"],
        wp["kkw"], wp["kaw"], wp["rkw"], rw=rw, dlp=wp["dlp"], ilp=wp["ilp"], glp=wp["glp"])
    y_r, s_fin = _wkv_scan(g_c, h_c, rp, yl, gate, bonus, _state_to_blockdiag(wkv_prev),
                           wp["lng"], wp["lnb"])
    y_r = y_r[:, :t].reshape(m, rw)
    wkv_new = _blockdiag_to_state(s_fin, nh)
    last = p3[:, -1]
    dlp, ilp = wp["dlp"], wp["ilp"]
    c0 = 3 * rw
    shift_new = jnp.concatenate([
        last[:, :c0], last[:, c0:c0 + wp["dl"]], last[:, c0 + dlp:c0 + dlp + wp["il"]],
        last[:, c0 + dlp + ilp:c0 + dlp + ilp + wp["gl"]]], axis=-1)

    if past is None:
        o_s = _attn_prompt(q.reshape(bsz, t, sw), k_s.reshape(bsz, t, sw), v_s.reshape(bsz, t, sw),
                           wp["bias2"])
    else:
        cache_k, cache_v, page_table = past
        nh_s = sw // HEAD_DIM
        rows_n = nh_s * t
        q3 = q.reshape(bsz, t, sw)
        rr = jnp.arange(rows_n)[:, None] // t
        cc = jnp.arange(sw)[None, :] // HEAD_DIM
        q_bd = jnp.where(rr == cc, jnp.tile(q3, (1, nh_s, 1)), jnp.zeros((), BF16))
        k_new = _pad_to(k_s.reshape(bsz, t, sw).astype(BF16), 1, LANES)
        v_new = _pad_to(v_s.reshape(bsz, t, sw).astype(BF16), 1, LANES)
        bias_rows = jnp.broadcast_to(jnp.repeat(wp["sb_bias"].astype(F32), t)[:, None], (rows_n, LANES))
        o_s = _attn_paged(q_bd, k_new, v_new, bias_rows,
                          cache_k.reshape(cache_k.shape[0], -1, HEAD_DIM),
                          cache_v.reshape(cache_v.shape[0], -1, HEAD_DIM),
                          page_table, t)
    o_s = o_s.reshape(m, sw)

    h, hn = _merge(y_r, o_s, gates, x, wp["w_br_r"], wp["w_br_s"], wp["w_o"], wp["g_ffn"])
    ff = wp["w_down"].shape[0]
    nprev = conv_prev.shape[1]
    pad_rows = jnp.zeros((bsz, SUBLANES - nprev, ff), F32)
    if t == SUBLANES:
        prev8 = jnp.concatenate([conv_prev, pad_rows], axis=1).reshape(m, ff)
    else:
        prev8 = jnp.concatenate([pad_rows, conv_prev], axis=1)
    ffn_out, tail = _ffn(hn, wp["w_up"], wp["conv_w"], wp["w_down"], prev8, t)
    conv_new = tail[:, SUBLANES - nprev:]
    y = _ple_out(h, ffn_out, pe3.reshape(m, -1), wp["w_ple"], wp["w_pg"], wp["g_ple"],
                 g_final.reshape(1, d), final_norm)
    return (y.reshape(bsz, t, d), shift_new, wkv_new, conv_new,
            k_s.reshape(bsz, t, sw // HEAD_DIM, HEAD_DIM), v_s.reshape(bsz, t, sw // HEAD_DIM, HEAD_DIM))


def kernel(x_prompt, x_sample, state_shift, state_wkv, state_conv, cache_k, cache_v, page_table, p_prompt, p_sample, g_mix, w_in, sb_bias, mu_shift, w0, w2, a0, a2, g2, k_k, k_a, r_k, ln_x_g, ln_x_b, w_br_r, w_br_s, w_o, g_ffn, w_up, conv_w, w_down, g_ple, w_ple, w_pg, g_final):
    depth = w_in.shape[0]
    bsz = x_prompt.shape[0]
    rw = w0.shape[1]
    nh = rw // HEAD_DIM
    ff = w_down.shape[1]
    rc = state_shift.shape[-1]
    nprev = state_conv.shape[2]
    h_p, h_s = x_prompt, x_sample
    outs_p = [[] for _ in range(5)]
    outs_s = [[] for _ in range(5)]
    for i in range(depth):
        wp = _prep_weights(g_mix[i], w_in[i], mu_shift[i], w0[i], w2[i], a0[i], a2[i], g2[i], k_k[i],
                           k_a[i], r_k[i], ln_x_g[i], ln_x_b[i], w_br_r[i], w_br_s[i], w_o[i],
                           g_ffn[i], w_up[i], conv_w[i], w_down[i], g_ple[i], w_ple[i], w_pg[i],
                           sb_bias[i])
        last = i == depth - 1
        res_p = _layer(h_p, p_prompt[i], jnp.zeros((bsz, rc), F32),
                       jnp.zeros((bsz, nh, HEAD_DIM, HEAD_DIM), F32),
                       jnp.zeros((bsz, nprev, ff), F32), None, wp, g_final, last)
        res_s = _layer(h_s, p_sample[i], state_shift[i], state_wkv[i], state_conv[i],
                       (cache_k[i], cache_v[i], page_table), wp, g_final, last)
        h_p, h_s = res_p[0], res_s[0]
        for dst, res in ((outs_p, res_p), (outs_s, res_s)):
            for lst, val in zip(dst, res[1:]):
                lst.append(val)
    return (h_p, h_s, *(jnp.stack(o) for o in outs_p), *(jnp.stack(o) for o in outs_s))
```

```python
import functools
import math

import jax
import jax.numpy as jnp
from jax import lax
from jax.experimental import pallas as pl
from jax.experimental.pallas import tpu as pltpu

F32 = jnp.float32
BF16 = jnp.bfloat16

LANES = 128
SUBLANES = 8
V7X_SCOPED_VMEM_BYTES = 60000 * 1024

HEAD_DIM = 64
CHUNK = 128
INTRA_PAIRS_PER_STEP = 4
SCAN_PAIRS_PER_STEP = 4
DECAY_SCALE = 0.606531
NORM_EPS = 1e-6
GN_EPS = HEAD_DIM * 1e-5
KK_EPS = 1e-24


def _vmem_limit(block_bytes):
    return int(min(V7X_SCOPED_VMEM_BYTES, 2 * block_bytes + (16 << 20)))


def _pick(n, pref, align):
    if n <= pref:
        return n
    best = None
    for d in range(align, pref + 1, align):
        if n % d == 0:
            best = d
    assert best is not None, (n, pref, align)
    return best


def _pad_to(x, axis, size):
    pad = size - x.shape[axis]
    if pad == 0:
        return x
    widths = [(0, 0)] * x.ndim
    widths[axis] = (0, pad)
    return jnp.pad(x, widths)


def _ceil_to(n, m):
    return -(-n // m) * m


def _dot(a, b):
    return jnp.dot(a, b, preferred_element_type=F32)


def _dot_nt(a, b):
    return lax.dot_general(a, b, (((1,), (1,)), ((), ())), preferred_element_type=F32)


def _dot_tn(a, b):
    return lax.dot_general(a, b, (((0,), (0,)), ((), ())), preferred_element_type=F32)


def _split2(x):
    hi = x.astype(BF16)
    lo = (x - hi.astype(F32)).astype(BF16)
    return hi, lo


def _split3(x):
    hi = x.astype(BF16)
    r1 = x - hi.astype(F32)
    mid = r1.astype(BF16)
    lo = (r1 - mid.astype(F32)).astype(BF16)
    return hi, mid, lo


def _iota2(shape, dim):
    return lax.broadcasted_iota(jnp.int32, shape, dim)


def _div_pow2(x, n):
    assert n & (n - 1) == 0, n
    return x >> (n.bit_length() - 1)


def _mod_pow2(x, n):
    assert n & (n - 1) == 0, n
    return x & (n - 1)


def _head_sum(x, e2):
    hi, lo = _split2(x)
    return _dot(jnp.concatenate([hi, lo], axis=1), e2)


def _same_head_matrix():
    r = _mod_pow2(_iota2((2 * LANES, LANES), 0), LANES)
    c = _iota2((2 * LANES, LANES), 1)
    return jnp.where(_div_pow2(r, HEAD_DIM) == _div_pow2(c, HEAD_DIM), 1.0, 0.0).astype(BF16)


def _rms_matmul_kernel(x_ref, g_ref, w_ref, o_ref, xn_ref):
    @pl.when(pl.program_id(1) == 0)
    def _():
        x = x_ref[...]
        ms = jnp.mean(x * x, axis=-1, keepdims=True)
        xn_ref[...] = (x * lax.rsqrt(ms + NORM_EPS) * g_ref[...]).astype(BF16)

    o_ref[...] = _dot(xn_ref[...], w_ref[...]).astype(o_ref.dtype)


def _rms_matmul(x, g, w, out_dtype, name):
    m, d = x.shape
    n = w.shape[1]
    bm = _pick(m, 512, SUBLANES)
    bn = _pick(n, 1024, 2 * LANES) if n % (2 * LANES) == 0 else _pick(n, 1024, LANES)
    blk = bm * d * 4 + d * bn * 2 + bm * bn * jnp.dtype(out_dtype).itemsize + bm * d * 2
    return pl.pallas_call(
        _rms_matmul_kernel,
        out_shape=jax.ShapeDtypeStruct((m, n), out_dtype),
        grid=(m // bm, n // bn),
        in_specs=[
            pl.BlockSpec((bm, d), lambda i, j: (i, 0)),
            pl.BlockSpec((1, d), lambda i, j: (0, 0)),
            pl.BlockSpec((d, bn), lambda i, j: (0, j)),
        ],
        out_specs=pl.BlockSpec((bm, bn), lambda i, j: (i, j)),
        scratch_shapes=[pltpu.VMEM((bm, d), BF16)],
        compiler_params=pltpu.CompilerParams(
            dimension_semantics=("parallel", "arbitrary"),
            vmem_limit_bytes=_vmem_limit(blk)),
        name=name,
    )(x, g, w)


def _wkv_intra_kernel(r_ref, k_ref, v_ref, xw_ref, xa_ref, xg_ref,
                      rp_ref, kp_ref, vp_ref, xwp_ref, xap_ref, xgp_ref,
                      rs_ref, ks_ref, vs_ref, xws_ref, xas_ref, xgs_ref,
                      mur_ref, muk_ref, muv_ref, muw_ref, mua_ref, mug_ref,
                      w0_ref, a0_ref, w2_ref, a2_ref, g2_ref,
                      kkw_ref, kaw_ref, rkw_ref,
                      g_out, h_out, rp_out, yl_out, gate_out, bonus_out,
                      *, t_real, n_pairs):
    c_idx = pl.program_id(1)
    C = CHUNK
    first = c_idx == 0

    def mixed(x_ref, p_ref, s_ref, mu_ref):
        x = x_ref[0]
        prev_last = jnp.where(first, s_ref[0], p_ref[0][SUBLANES - 1:SUBLANES])
        rolled = pltpu.roll(x, 1, 0)
        prev = jnp.where(_iota2(x.shape, 0) == 0, prev_last, rolled)
        y = x + (prev - x) * mu_ref[...]
        if t_real < C:
            y = jnp.concatenate([y, jnp.zeros((C - t_real, y.shape[1]), F32)], axis=0)
        return y

    r_all = mixed(r_ref, rp_ref, rs_ref, mur_ref)
    k_all = mixed(k_ref, kp_ref, ks_ref, muk_ref)
    v_all = mixed(v_ref, vp_ref, vs_ref, muv_ref)
    xw = jnp.tanh(mixed(xw_ref, xwp_ref, xws_ref, muw_ref)).astype(BF16)
    xa = mixed(xa_ref, xap_ref, xas_ref, mua_ref).astype(BF16)
    xg = jax.nn.sigmoid(mixed(xg_ref, xgp_ref, xgs_ref, mug_ref)).astype(BF16)

    lw_all = -DECAY_SCALE * jax.nn.sigmoid(w0_ref[...] + _dot(xw, w2_ref[...]))
    if t_real < C:
        lw_all = jnp.where(_iota2(lw_all.shape, 0) < t_real, lw_all, 0.0)
    iclr_all = jax.nn.sigmoid(a0_ref[...] + _dot(xa, a2_ref[...]))
    gate_all = _dot(xg, g2_ref[...])
    gate_out[0] = gate_all.astype(gate_out.dtype)

    e2 = _same_head_matrix()
    row = _iota2((C, C), 0)
    col = _iota2((C, C), 1)
    strict_lower = row > col
    lower = row >= col
    l_incl = jnp.where(lower, 1.0, 0.0).astype(BF16)
    l3 = jnp.concatenate([l_incl, l_incl, l_incl], axis=1)
    lane = _iota2((C, LANES), 1)
    head_masks = (lane < HEAD_DIM, lane >= HEAD_DIM)
    rr = _iota2((LANES, LANES), 0)
    cc = _iota2((LANES, LANES), 1)
    block_mask = _div_pow2(rr, HEAD_DIM) == _div_pow2(cc, HEAD_DIM)
    eye_mask = rr == cc

    pairs = range(n_pairs)
    sls = [slice(p * LANES, (p + 1) * LANES) for p in pairs]
    r = [r_all[:, sl] for sl in sls]
    k = [k_all[:, sl] for sl in sls]
    v = [v_all[:, sl] for sl in sls]
    lw = [lw_all[:, sl] for sl in sls]
    iclr = [iclr_all[:, sl] for sl in sls]

    kkr = [k[p] * kkw_ref[:, sls[p]] for p in pairs]
    k_mod = [k[p] * (1.0 + (iclr[p] - 1.0) * kaw_ref[:, sls[p]]) for p in pairs]
    kk_ss = [_head_sum(kkr[p] * kkr[p], e2) for p in pairs]
    rk_sum = [_head_sum(r[p] * k_mod[p] * rkw_ref[:, sls[p]], e2) for p in pairs]
    cum = [_dot(l3, jnp.concatenate(_split3(lw[p]), axis=0)) for p in pairs]

    vb, em, ecl, kbar, bbar, rhs_scores, lhs_scores, am_b, rm_f = [], [], [], [], [], [], [], [], []
    for p in pairs:
        bonus_out[0, :, sls[p]] = rk_sum[p] * v[p]
        kk = kkr[p] * lax.rsqrt(jnp.maximum(kk_ss[p], KK_EPS))
        b = kk * iclr[p]
        m_row = cum[p][C // 2 - 1:C // 2]
        cum_last = cum[p][C - 1:C]
        g = cum[p] - m_row
        eng = jnp.exp(-g)
        ebar = jnp.exp(cum_last - cum[p])
        at = -kk * jnp.exp(g - lw[p])
        rt = r[p] * jnp.exp(g)
        kbar.append((k_mod[p] * ebar).astype(BF16))
        bbar.append((b * ebar).astype(BF16))
        em.append(jnp.exp(m_row))
        ecl.append(jnp.exp(cum_last))
        vb.append(v[p].astype(BF16))
        rhs_scores.append(jnp.concatenate([(k_mod[p] * eng).astype(BF16), (b * eng).astype(BF16)], axis=0))
        stack = []
        for mh in head_masks:
            am_b.append(jnp.where(mh, at, 0.0).astype(BF16))
            rm_f.append(jnp.where(mh, rt, 0.0))
            stack += [am_b[-1], rm_f[-1].astype(BF16)]
        lhs_scores.append(jnp.concatenate(stack, axis=0))

    sc = [_dot_nt(lhs_scores[p], rhs_scores[p]) for p in pairs]
    heads = range(2 * n_pairs)
    m_ak, m_ab, p_rk_b, p_rb_b = [], [], [], []
    for i in heads:
        s = sc[i // 2][(i % 2) * 2 * C:(i % 2 + 1) * 2 * C]
        m_ak.append(jnp.where(strict_lower, s[:C, :C], 0.0).astype(BF16))
        m_ab.append(jnp.where(strict_lower, s[:C, C:], 0.0))
        p_rk_b.append(jnp.where(lower, s[C:, :C], 0.0).astype(BF16))
        p_rb_b.append(jnp.where(lower, s[C:, C:], 0.0).astype(BF16))

    eye = jnp.where(row == col, 1.0, 0.0).astype(F32)
    t = [eye + m_ab[i] for i in heads]
    qb = [m_ab[i].astype(BF16) for i in heads]
    q = [_dot(qb[i], qb[i]) for i in heads]
    w1 = [_dot(m_ak[i], vb[i // 2]).astype(BF16) for i in heads]
    for _ in range(int(math.log2(C)) - 2):
        qb = [q[i].astype(BF16) for i in heads]
        st = [_dot(jnp.concatenate([t[i].astype(BF16), qb[i]], axis=0), qb[i]) for i in heads]
        t = [t[i] + st[i][:C] for i in heads]
        q = [st[i][C:] for i in heads]
    corr = [_dot(t[i].astype(BF16), q[i].astype(BF16)) for i in heads]
    t_inv = [(t[i] + corr[i]).astype(BF16) for i in heads]
    tu = [_dot(t_inv[i], jnp.concatenate([w1[i], am_b[i]], axis=1)) for i in heads]
    ul = [tu[i][:, :LANES] for i in heads]
    ap = [tu[i][:, LANES:] for i in heads]
    yl = [_dot(jnp.concatenate([p_rk_b[i], p_rb_b[i]], axis=1),
               jnp.concatenate([vb[i // 2], ul[i].astype(BF16)], axis=0)) for i in heads]
    rp_add = [_dot(p_rb_b[i], ap[i].astype(BF16)) for i in heads]

    m1 = head_masks[1]
    ul_pair = [jnp.where(m1, ul[2 * p + 1], ul[2 * p]).astype(BF16) for p in pairs]
    ap_pair = [(ap[2 * p] + ap[2 * p + 1]).astype(BF16) for p in pairs]
    ba = [_dot_tn(bbar[p], ap_pair[p]) for p in pairs]
    hh = [_dot_tn(jnp.concatenate([kbar[p], bbar[p]], axis=0),
                  jnp.concatenate([vb[p], ul_pair[p]], axis=0)) for p in pairs]
    for p in pairs:
        sl = sls[p]
        g_out[0, 0, :, sl] = jnp.where(block_mask, ba[p] * em[p], 0.0) + jnp.where(eye_mask, ecl[p], 0.0)
        h_out[0, 0, :, sl] = jnp.where(block_mask, hh[p], 0.0)
        rp_pair = rm_f[2 * p] + rp_add[2 * p] + rm_f[2 * p + 1] + rp_add[2 * p + 1]
        rp_out[0, :, sl] = (rp_pair * em[p]).astype(rp_out.dtype)
        yl_out[0, :, sl] = jnp.where(m1, yl[2 * p + 1], yl[2 * p])


def _wkv_intra(p3, shift, mu, w0, a0, w2, a2, g2, kkw, kaw, rkw, *, rw, dlp, ilp, glp):
    bsz, t, npc = p3.shape
    C = CHUNK
    t_real = min(t, C)
    assert t % t_real == 0 and t_real % SUBLANES == 0
    nch = t // t_real
    tp = nch * C
    n_pairs = _pick(rw // LANES, INTRA_PAIRS_PER_STEP, 1)
    lw_ = n_pairs * LANES
    ng = rw // lw_
    o_w, o_a, o_g = 3 * rw, 3 * rw + dlp, 3 * rw + dlp + ilp
    assert o_w % dlp == 0 and o_a % ilp == 0 and o_g % glp == 0
    rows_prev = t_real // SUBLANES

    def cur(width, off):
        return pl.BlockSpec((1, t_real, width), lambda b, c, g, o=off // width: (b, c, o))

    def cur_g(width, off):
        return pl.BlockSpec((1, t_real, width), lambda b, c, g, o=off // width: (b, c, o + g))

    def prev(width, off):
        return pl.BlockSpec((1, SUBLANES, width),
                            lambda b, c, g, o=off // width: (b, jnp.maximum(c * rows_prev - 1, 0), o))

    def prev_g(width, off):
        return pl.BlockSpec((1, SUBLANES, width),
                            lambda b, c, g, o=off // width: (b, jnp.maximum(c * rows_prev - 1, 0), o + g))

    def sh(width, off):
        return pl.BlockSpec((1, 1, width), lambda b, c, g, o=off // width: (b, 0, o))

    def sh_g(width, off):
        return pl.BlockSpec((1, 1, width), lambda b, c, g, o=off // width: (b, 0, o + g))

    def vec(width, off):
        return pl.BlockSpec((1, width), lambda b, c, g, o=off // width: (0, o))

    def vec_g(width, off=0):
        return pl.BlockSpec((1, width), lambda b, c, g, o=off // width: (0, o + g))

    def mat_g(rows):
        return pl.BlockSpec((rows, lw_), lambda b, c, g: (0, g))

    in_specs = (
        [cur_g(lw_, 0), cur_g(lw_, rw), cur_g(lw_, 2 * rw), cur(dlp, o_w), cur(ilp, o_a), cur(glp, o_g)]
        + [prev_g(lw_, 0), prev_g(lw_, rw), prev_g(lw_, 2 * rw), prev(dlp, o_w), prev(ilp, o_a), prev(glp, o_g)]
        + [sh_g(lw_, 0), sh_g(lw_, rw), sh_g(lw_, 2 * rw), sh(dlp, o_w), sh(ilp, o_a), sh(glp, o_g)]
        + [vec_g(lw_, 0), vec_g(lw_, rw), vec_g(lw_, 2 * rw), vec(dlp, o_w), vec(ilp, o_a), vec(glp, o_g)]
        + [vec_g(lw_), vec_g(lw_), mat_g(dlp), mat_g(ilp), mat_g(glp)]
        + [vec_g(lw_), vec_g(lw_), vec_g(lw_)]
    )
    args = ([p3] * 6 + [p3] * 6 + [shift] * 6 + [mu] * 6 + [w0, a0, w2, a2, g2, kkw, kaw, rkw])
    out_shape = (
        jax.ShapeDtypeStruct((bsz, nch, LANES, rw), F32),
        jax.ShapeDtypeStruct((bsz, nch, LANES, rw), F32),
        jax.ShapeDtypeStruct((bsz, tp, rw), BF16),
        jax.ShapeDtypeStruct((bsz, tp, rw), F32),
        jax.ShapeDtypeStruct((bsz, tp, rw), BF16),
        jax.ShapeDtypeStruct((bsz, tp, rw), F32),
    )
    gh_spec = pl.BlockSpec((1, 1, LANES, lw_), lambda b, c, g: (b, c, 0, g))
    tok_spec = pl.BlockSpec((1, C, lw_), lambda b, c, g: (b, c, g))
    blk = (t_real * (3 * lw_ + dlp + ilp + glp) * 4 + (dlp + ilp + glp) * lw_ * 2
           + 2 * LANES * lw_ * 4 + C * lw_ * 12)
    return pl.pallas_call(
        functools.partial(_wkv_intra_kernel, t_real=t_real, n_pairs=n_pairs),
        out_shape=out_shape,
        grid=(bsz, nch, ng),
        in_specs=in_specs,
        out_specs=(gh_spec, gh_spec, tok_spec, tok_spec, tok_spec, tok_spec),
        compiler_params=pltpu.CompilerParams(
            dimension_semantics=("parallel", "parallel", "parallel"),
            vmem_limit_bytes=_vmem_limit(blk)),
        name="wkv_intra",
    )(*args)


def _wkv_scan_kernel(g_ref, h_ref, rp_ref, yl_ref, gate_ref, bonus_ref, s0_ref, lng_ref, lnb_ref,
                     y_ref, s_out_ref, *, n_chunks, n_pairs):
    C = CHUNK
    e2 = _same_head_matrix()
    inv_n = 1.0 / HEAD_DIM
    lanes = [slice(p * LANES, (p + 1) * LANES) for p in range(n_pairs)]

    s_out_ref[...] = s0_ref[...]

    def body(c, carry):
        rows = pl.ds(pl.multiple_of(c * C, C), C)
        sb = [s_out_ref[0, :, sl].astype(BF16) for sl in lanes]
        y = [_dot(rp_ref[0, rows, sl], s) for sl, s in zip(lanes, sb)]
        s_new = [_dot(g_ref[0, c, :, sl].astype(BF16), s) for sl, s in zip(lanes, sb)]
        for sl, s in zip(lanes, s_new):
            s_out_ref[0, :, sl] = s + h_ref[0, c, :, sl]
        y = [yy + yl_ref[0, rows, sl] for sl, yy in zip(lanes, y)]
        mu = [_head_sum(yy, e2) * inv_n for yy in y]
        yc = [yy - m for yy, m in zip(y, mu)]
        var = [_head_sum(x * x, e2) * inv_n for x in yc]
        for sl, x, vv in zip(lanes, yc, var):
            yn = x * lax.rsqrt(vv + GN_EPS) * lng_ref[:, sl] + lnb_ref[:, sl]
            out = (yn + bonus_ref[0, rows, sl]) * gate_ref[0, rows, sl].astype(F32)
            y_ref[0, rows, sl] = out.astype(y_ref.dtype)
        return carry

    lax.fori_loop(0, n_chunks, body, 0)


def _wkv_scan(g, h, rp, yl, gate, bonus, s0, lng, lnb):
    bsz, nch, _, rw = g.shape
    tp = rp.shape[1]
    n_pairs = _pick(rw // LANES, SCAN_PAIRS_PER_STEP, 1)
    lw_ = n_pairs * LANES
    gh_spec = pl.BlockSpec((1, nch, LANES, lw_), lambda b, q: (b, 0, 0, q))
    tok_spec = pl.BlockSpec((1, tp, lw_), lambda b, q: (b, 0, q))
    st_spec = pl.BlockSpec((1, LANES, lw_), lambda b, q: (b, 0, q))
    vec_spec = pl.BlockSpec((1, lw_), lambda b, q: (0, q))
    blk = 2 * nch * LANES * lw_ * 4 + tp * lw_ * (2 + 4 + 2 + 4 + 2) + 2 * LANES * lw_ * 4
    return pl.pallas_call(
        functools.partial(_wkv_scan_kernel, n_chunks=nch, n_pairs=n_pairs),
        out_shape=(jax.ShapeDtypeStruct((bsz, tp, rw), BF16),
                   jax.ShapeDtypeStruct((bsz, LANES, rw), F32)),
        grid=(bsz, rw // lw_),
        in_specs=[gh_spec, gh_spec, tok_spec, tok_spec, tok_spec, tok_spec, st_spec, vec_spec, vec_spec],
        out_specs=(tok_spec, st_spec),
        compiler_params=pltpu.CompilerParams(
            dimension_semantics=("parallel", "parallel"),
            vmem_limit_bytes=_vmem_limit(blk)),
        name="wkv_scan",
    )(g, h, rp, yl, gate, bonus, s0, lng, lnb)


def _cumsum_rhs():
    r = _mod_pow2(_iota2((2 * LANES, 2 * LANES), 0), LANES)
    c = _iota2((2 * LANES, 2 * LANES), 1)
    return jnp.where((c >= LANES) | (r > c), -1.0, 0.0).astype(BF16)


SOFTPLUS_LINEAR_ABOVE = 30.0


def _softplus(z, mask=None):
    sp = jnp.log(1.0 + jnp.exp(jnp.minimum(z, SOFTPLUS_LINEAR_ABOVE)))
    sp = jnp.where(z > SOFTPLUS_LINEAR_ABOVE, z, sp)
    return sp if mask is None else jnp.where(mask, sp, 0.0)


def _sb_tile(z, v_tile, run, acc, w2, mask):
    sp = _softplus(z, mask)
    cs2 = _dot(jnp.concatenate(_split2(sp), axis=1), w2)
    att = jnp.exp((z - sp) + cs2[:, :LANES] + run)
    if mask is not None:
        att = jnp.where(mask, att, 0.0)
    acc = acc + _dot(att.astype(BF16), v_tile)
    return run + cs2[:, LANES:], acc


def _attn_prompt_kernel(q_ref, k_ref, v_ref, bias_ref, o_ref, kb_ref, vb_ref, run_ref, acc_ref):
    qi = pl.program_id(2)
    qb = q_ref.shape[1]
    kb = 2 * LANES
    assert qb == kb

    @pl.when(qi == 0)
    def _():
        kb_ref[...] = k_ref[0].astype(BF16)
        vb_ref[...] = v_ref[0].astype(BF16)

    q = q_ref[0]
    m0 = _iota2((qb, LANES), 1) < HEAD_DIM
    zero = jnp.zeros((), BF16)
    qs = jnp.concatenate([jnp.where(m0, q, zero), jnp.where(m0, zero, q)], axis=0)
    w2 = _cumsum_rhs()
    run_ref[...] = jnp.zeros(run_ref.shape, F32)
    acc_ref[...] = jnp.zeros(acc_ref.shape, F32)

    def step(j, n_blocks, mask):
        nk = 2 * n_blocks
        rows = pl.ds(pl.multiple_of(j * kb, kb), n_blocks * kb)
        z = _dot_nt(qs, kb_ref[rows, :])
        b0 = jnp.concatenate([bias_ref[0, 0:1, :]] * nk, axis=1)
        b1 = jnp.concatenate([bias_ref[0, 1:2, :]] * nk, axis=1)
        z = jnp.concatenate([z[:qb] + b0, z[qb:] + b1], axis=0)
        sp = _softplus(z, mask)
        hi, lo = _split2(sp)
        tiles = [slice(c * LANES, (c + 1) * LANES) for c in range(nk)]
        cs = [_dot(jnp.concatenate([hi[:, c], lo[:, c]], axis=1), w2) for c in tiles]
        zs = z - sp
        run = run_ref[...]
        e = [None] * nk
        for c in reversed(range(nk)):
            e[c] = zs[:, tiles[c]] + cs[c][:, :LANES] + run
            run = run + cs[c][:, LANES:]
        att = jnp.exp(jnp.concatenate(e, axis=1))
        if mask is not None:
            att = jnp.where(mask, att, 0.0)
        acc_ref[...] += _dot(att.astype(BF16), vb_ref[rows, :])
        run_ref[...] = run

    qpos = _mod_pow2(_iota2((2 * qb, kb), 0), qb)
    step(qi, 1, _iota2((2 * qb, kb), 1) < qpos)

    def body(jj, carry):
        step(qi - 2 - 2 * jj, 2, None)
        return carry

    lax.fori_loop(0, qi // 2, body, 0)

    @pl.when(qi % 2 == 1)
    def _():
        step(0, 1, None)

    o_ref[0] = jnp.where(m0, acc_ref[:qb], acc_ref[qb:]).astype(o_ref.dtype)


def _attn_prompt(q, k, v, bias2):
    bsz, t, w = q.shape
    qb = 2 * LANES
    assert t % qb == 0
    blk = qb * LANES * 2 * 2 + 2 * t * LANES * 4
    return pl.pallas_call(
        _attn_prompt_kernel,
        out_shape=jax.ShapeDtypeStruct((bsz, t, w), BF16),
        grid=(bsz, w // LANES, t // qb),
        in_specs=[
            pl.BlockSpec((1, qb, LANES), lambda b, p, i: (b, i, p)),
            pl.BlockSpec((1, t, LANES), lambda b, p, i: (b, 0, p)),
            pl.BlockSpec((1, t, LANES), lambda b, p, i: (b, 0, p)),
            pl.BlockSpec((1, 2, LANES), lambda b, p, i: (p, 0, 0)),
        ],
        out_specs=pl.BlockSpec((1, qb, LANES), lambda b, p, i: (b, i, p)),
        scratch_shapes=[pltpu.VMEM((t, LANES), BF16), pltpu.VMEM((t, LANES), BF16),
                        pltpu.VMEM((2 * qb, LANES), F32), pltpu.VMEM((2 * qb, LANES), F32)],
        compiler_params=pltpu.CompilerParams(
            dimension_semantics=("parallel", "parallel", "arbitrary"),
            vmem_limit_bytes=_vmem_limit(blk + 2 * t * LANES * 2)),
        name="attn_prompt",
    )(q, k, v, bias2)


PAGES_PER_STEP = 8


def _attn_paged_kernel(pt_ref, *refs, n_groups):
    del pt_ref
    pps = PAGES_PER_STEP
    k_refs = refs[:pps]
    v_refs = refs[pps:2 * pps]
    q_ref, kn_ref, vn_ref, bias_ref, o_ref, run_ref, acc_ref = refs[2 * pps:]
    g = pl.program_id(1)
    rows_n = q_ref.shape[1]
    t_new = o_ref.shape[1]
    w2 = _cumsum_rhs()
    q = q_ref[0]
    bias = bias_ref[...]

    @pl.when(g == 0)
    def _():
        lane = _iota2((rows_n, LANES), 1)
        row = _iota2((rows_n, LANES), 0)
        mask = lane < _mod_pow2(row, t_new)
        z = _dot_nt(q, kn_ref[0]) + bias
        run, acc = _sb_tile(z, vn_ref[0], jnp.zeros((rows_n, LANES), F32),
                            jnp.zeros(acc_ref.shape, F32), w2, mask)
        run_ref[...] = run
        acc_ref[...] = acc

    pages = range(pps)
    kt = [k_refs[p][0].astype(BF16) for p in pages]
    z2 = [_dot(q, jnp.concatenate(kt[i:i + 2], axis=1)) for i in range(0, pps, 2)]
    z = [z2[p // 2][:, (p % 2) * LANES:(p % 2 + 1) * LANES] + bias for p in pages]
    sp = [_softplus(zz) for zz in z]
    cs2 = [_dot(jnp.concatenate(_split2(x), axis=1), w2) for x in sp]
    run = run_ref[...]
    att = []
    for p in pages:
        att.append(jnp.exp((z[p] - sp[p]) + cs2[p][:, :LANES] + run).astype(BF16))
        run = run + cs2[p][:, LANES:]
    run_ref[...] = run
    vt = jnp.concatenate([v_refs[p][0].astype(BF16) for p in pages], axis=1)
    acc_ref[...] += _dot_nt(jnp.concatenate(att, axis=1), vt)

    @pl.when(g == n_groups - 1)
    def _():
        acc = acc_ref[...]
        rr = _iota2(acc.shape, 0)
        cc = _iota2(acc.shape, 1)
        picked = jnp.where(_div_pow2(rr, t_new) == _div_pow2(cc, HEAD_DIM), acc, 0.0)
        out = picked[0:t_new]
        for h in range(1, rows_n // t_new):
            out = out + picked[h * t_new:(h + 1) * t_new]
        o_ref[0] = out.astype(o_ref.dtype)


def _attn_paged(q_bd, k_new, v_new, bias_rows, cache_k, cache_v, page_table, t_new):
    bsz, rows_n, w = q_bd.shape
    n_pages = page_table.shape[1]
    page = cache_k.shape[2]
    pps = PAGES_PER_STEP
    assert page == LANES and n_pages % pps == 0 and pps % 2 == 0 and cache_k.shape[1] == w
    n_groups = n_pages // pps

    def page_spec(p):
        return pl.BlockSpec((1, w, page),
                            lambda b, g, pt, p=p % pps: (pt[b, n_pages - 1 - (g * pps + p)], 0, 0))

    in_specs = ([page_spec(p) for p in range(2 * pps)] + [
        pl.BlockSpec((1, rows_n, w), lambda b, g, pt: (b, 0, 0)),
        pl.BlockSpec((1, LANES, w), lambda b, g, pt: (b, 0, 0)),
        pl.BlockSpec((1, LANES, w), lambda b, g, pt: (b, 0, 0)),
        pl.BlockSpec((rows_n, LANES), lambda b, g, pt: (0, 0)),
    ])
    blk = 2 * pps * page * w * 4 + rows_n * w * 2 + 2 * LANES * w * 2 + rows_n * w * 4
    return pl.pallas_call(
        functools.partial(_attn_paged_kernel, n_groups=n_groups),
        out_shape=jax.ShapeDtypeStruct((bsz, t_new, w), BF16),
        grid_spec=pltpu.PrefetchScalarGridSpec(
            num_scalar_prefetch=1,
            grid=(bsz, n_groups),
            in_specs=in_specs,
            out_specs=pl.BlockSpec((1, t_new, w), lambda b, g, pt: (b, 0, 0)),
            scratch_shapes=[pltpu.VMEM((rows_n, LANES), F32), pltpu.VMEM((rows_n, w), F32)],
        ),
        compiler_params=pltpu.CompilerParams(
            dimension_semantics=("parallel", "arbitrary"),
            vmem_limit_bytes=_vmem_limit(blk)),
        name="attn_paged",
    )(page_table, *([cache_k] * pps), *([cache_v] * pps), q_bd, k_new, v_new, bias_rows)


def _rms(x, g):
    ms = jnp.mean(x * x, axis=-1, keepdims=True)
    return x * lax.rsqrt(ms + NORM_EPS) * g


def _merge_kernel(yr_ref, os_ref, gr_ref, gs_ref, x_ref, wr_ref, ws_ref, wo_ref, g_ref, h_ref, hn_ref):
    a = _dot(yr_ref[...], wr_ref[...])
    b = _dot(os_ref[...], ws_ref[...])
    mixed = (jax.nn.sigmoid(gr_ref[...].astype(F32)) * a
             + jax.nn.sigmoid(gs_ref[...].astype(F32)) * b)
    h = x_ref[...] + _dot(mixed.astype(BF16), wo_ref[...])
    h_ref[...] = h
    hn_ref[...] = _rms(h, g_ref[...]).astype(hn_ref.dtype)


def _resident(shape):
    return pl.BlockSpec(shape, lambda *_: (0,) * len(shape), pipeline_mode=pl.Buffered(1))


def _merge(yr, os_, gates, x, wr, ws, wo, g_ffn):
    m, d = x.shape
    rw = yr.shape[1]
    bm = _pick(m, 256, SUBLANES)
    blk = bm * (2 * rw * 2 + 2 * d * 2 + d * 4 + d * 4 + d * 2) + (2 * rw * d + d * d)
    return pl.pallas_call(
        _merge_kernel,
        out_shape=(jax.ShapeDtypeStruct((m, d), F32), jax.ShapeDtypeStruct((m, d), BF16)),
        grid=(m // bm,),
        in_specs=[
            pl.BlockSpec((bm, rw), lambda i: (i, 0)),
            pl.BlockSpec((bm, rw), lambda i: (i, 0)),
            pl.BlockSpec((bm, d), lambda i: (i, 0)),
            pl.BlockSpec((bm, d), lambda i: (i, 1)),
            pl.BlockSpec((bm, d), lambda i: (i, 0)),
            _resident(wr.shape), _resident(ws.shape), _resident(wo.shape),
            pl.BlockSpec((1, d), lambda i: (0, 0)),
        ],
        out_specs=(pl.BlockSpec((bm, d), lambda i: (i, 0)), pl.BlockSpec((bm, d), lambda i: (i, 0))),
        compiler_params=pltpu.CompilerParams(
            dimension_semantics=("parallel",),
            vmem_limit_bytes=_vmem_limit(blk + bm * d * 16)),
        name="merge",
    )(yr, os_, gates, gates, x, wr, ws, wo, g_ffn)


def _gelu_tanh(x):
    return 0.5 * x * (1.0 + jnp.tanh(math.sqrt(2.0 / math.pi) * (x + 0.044715 * x * x * x)))


def _ffn_kernel(hn_ref, wg_ref, wv_ref, cw_ref, wd_ref, prev_ref, o_ref, tail_ref, carry_ref,
                *, blocks_per_seq, seq_len):
    i = pl.program_id(0)
    f = pl.program_id(1)
    hn = hn_ref[...]
    ug = _dot(hn, wg_ref[...])
    uv = _dot(hn, wv_ref[...])
    bm = ug.shape[0]
    row = _iota2(ug.shape, 0)
    r1 = pltpu.roll(ug, 1, 0)
    r2 = pltpu.roll(ug, 2, 0)
    if seq_len == SUBLANES:
        p2 = prev_ref[...]
        p1 = pltpu.roll(p2, bm - 1, 0)
        t = _mod_pow2(row, SUBLANES)
        s1 = jnp.where(t == 0, p1, r1)
        s2 = jnp.where(t < 2, p2, r2)
        tail_ref[...] = ug
    else:
        first = (i % blocks_per_seq) == 0
        pv = jnp.where(first, prev_ref[0], carry_ref[f])
        head = _iota2((SUBLANES, ug.shape[1]), 0)
        s1 = jnp.concatenate(
            [jnp.where(head == 0, pv[7:8], r1[:SUBLANES]), r1[SUBLANES:]], axis=0)
        s2 = jnp.concatenate(
            [jnp.where(head == 0, pv[6:7], jnp.where(head == 1, pv[7:8], r2[:SUBLANES])),
             r2[SUBLANES:]], axis=0)
        carry_ref[f] = ug[bm - SUBLANES:]
        tail_ref[0] = ug[bm - SUBLANES:]
    conv = cw_ref[0:1, :] * s2 + cw_ref[1:2, :] * s1 + cw_ref[2:3, :] * ug
    act = (_gelu_tanh(conv) * uv).astype(BF16)
    part = _dot(act, wd_ref[...])

    @pl.when(f == 0)
    def _():
        o_ref[...] = part

    @pl.when(f > 0)
    def _():
        o_ref[...] += part


def _ffn(hn, w_up, conv_w, w_down, prev8, seq_len):
    m, d = hn.shape
    ff = w_down.shape[0]
    bf = _pick(ff, 512, 2 * LANES) if ff % (2 * LANES) == 0 else _pick(ff, 512, LANES)
    nf = ff // bf
    if seq_len == SUBLANES:
        bm = m
        blocks_per_seq = 1
        prev_spec = pl.BlockSpec((bm, bf), lambda i, f: (i, f))
        tail_shape = (m, ff)
        tail_spec = pl.BlockSpec((bm, bf), lambda i, f: (i, f))
    else:
        bm = _pick(seq_len, 1024, SUBLANES)
        blocks_per_seq = seq_len // bm
        prev_spec = pl.BlockSpec((1, SUBLANES, bf), lambda i, f: (i // blocks_per_seq, 0, f))
        tail_shape = (m // bm, SUBLANES, ff)
        tail_spec = pl.BlockSpec((1, SUBLANES, bf), lambda i, f: (i, 0, f))
    blk = bm * d * (2 + 4) + 3 * d * bf * 2 + 4 * bm * bf * 4
    out, tail = pl.pallas_call(
        functools.partial(_ffn_kernel, blocks_per_seq=blocks_per_seq, seq_len=seq_len),
        out_shape=(jax.ShapeDtypeStruct((m, d), F32), jax.ShapeDtypeStruct(tail_shape, F32)),
        grid=(m // bm, nf),
        in_specs=[
            pl.BlockSpec((bm, d), lambda i, f: (i, 0)),
            pl.BlockSpec((d, bf), lambda i, f: (0, f)),
            pl.BlockSpec((d, bf), lambda i, f: (0, nf + f)),
            pl.BlockSpec((3, bf), lambda i, f: (0, f)),
            pl.BlockSpec((bf, d), lambda i, f: (f, 0)),
            prev_spec,
        ],
        out_specs=(pl.BlockSpec((bm, d), lambda i, f: (i, 0)), tail_spec),
        scratch_shapes=[pltpu.VMEM((nf, SUBLANES, bf), F32)],
        compiler_params=pltpu.CompilerParams(
            dimension_semantics=("arbitrary", "arbitrary"),
            vmem_limit_bytes=_vmem_limit(blk)),
        name="ffn",
    )(hn, w_up, w_up, conv_w, w_down, prev8)
    n_seq = m // seq_len
    if seq_len == SUBLANES:
        return out, tail.reshape(n_seq, SUBLANES, ff)
    return out, tail.reshape(n_seq, blocks_per_seq, SUBLANES, ff)[:, -1]


def _ple_kernel(h_ref, f_ref, pe_ref, wple_ref, wpg_ref, gp_ref, gf_ref, y_ref, *, final_norm):
    h2 = h_ref[...] + f_ref[...]
    hn = _rms(h2, gp_ref[...]).astype(BF16)
    gate = jax.nn.sigmoid(_dot(hn, wpg_ref[...]))
    emb = _dot(pe_ref[...].astype(BF16), wple_ref[...])
    h3 = h2 + emb * gate
    y_ref[...] = _rms(h3, gf_ref[...]) if final_norm else h3


def _ple_out(h, ffn_out, pe, w_ple, w_pg, g_ple, g_final, final_norm):
    m, d = h.shape
    pd = pe.shape[1]
    bm = _pick(m, 512, SUBLANES)
    blk = bm * (3 * d * 4 + pd * 4) + pd * d * 2 + d * d * 2
    return pl.pallas_call(
        functools.partial(_ple_kernel, final_norm=final_norm),
        out_shape=jax.ShapeDtypeStruct((m, d), F32),
        grid=(m // bm,),
        in_specs=[
            pl.BlockSpec((bm, d), lambda i: (i, 0)),
            pl.BlockSpec((bm, d), lambda i: (i, 0)),
            pl.BlockSpec((bm, pd), lambda i: (i, 0)),
            _resident(w_ple.shape), _resident(w_pg.shape),
            pl.BlockSpec((1, d), lambda i: (0, 0)),
            pl.BlockSpec((1, d), lambda i: (0, 0)),
        ],
        out_specs=pl.BlockSpec((bm, d), lambda i: (i, 0)),
        compiler_params=pltpu.CompilerParams(
            dimension_semantics=("parallel",),
            vmem_limit_bytes=_vmem_limit(blk + bm * d * 8)),
        name="ple_out",
    )(h, ffn_out, pe, w_ple, w_pg, g_ple, g_final)


def _prep_weights(g_mix, w_in, mu_shift, w0, w2, a0, a2, g2, k_k, k_a, r_k, ln_x_g, ln_x_b,
                  w_br_r, w_br_s, w_o, g_ffn, w_up, conv_w, w_down, g_ple, w_ple, w_pg, sb_bias):
    d = w_in.shape[0]
    rw = w0.shape[0]
    dl, il, gl = w2.shape[0], a2.shape[0], g2.shape[0]
    dlp, ilp, glp = (_ceil_to(n, LANES) for n in (dl, il, gl))
    sw = w_br_s.shape[0]
    rc = 3 * rw + dl + il + gl
    c_w, c_a, c_g = 3 * rw, 3 * rw + dl, 3 * rw + dl + il

    def regroup(x):
        return jnp.concatenate([
            x[..., :c_w], _pad_to(x[..., c_w:c_a], -1, dlp), _pad_to(x[..., c_a:c_g], -1, ilp),
            _pad_to(x[..., c_g:rc], -1, glp)], axis=-1)

    w_r = regroup(w_in[:, :rc]).astype(BF16)
    scale = HEAD_DIM ** -0.5
    w_q = (w_in[:, rc:rc + sw] * scale).astype(BF16)
    w_k = w_in[:, rc + sw:rc + 2 * sw].astype(BF16)
    w_v = w_in[:, rc + 2 * sw:rc + 3 * sw].astype(BF16)
    w_g = w_in[:, rc + 3 * sw:].astype(BF16)
    nh_s = sw // HEAD_DIM
    return dict(
        d=d, rw=rw, sw=sw, dl=dl, il=il, gl=gl, dlp=dlp, ilp=ilp, glp=glp, rc=rc,
        regroup=regroup,
        g_mix=g_mix.reshape(1, d), w_r=w_r, w_q=w_q, w_k=w_k, w_v=w_v, w_g=w_g,
        mu=regroup(mu_shift).reshape(1, -1),
        w0=w0.reshape(1, rw), a0=a0.reshape(1, rw),
        w2=_pad_to(w2, 0, dlp).astype(BF16), a2=_pad_to(a2, 0, ilp).astype(BF16),
        g2=_pad_to(g2, 0, glp).astype(BF16),
        kkw=k_k.reshape(1, rw), kaw=k_a.reshape(1, rw), rkw=r_k.reshape(1, rw),
        lng=ln_x_g.reshape(1, rw), lnb=ln_x_b.reshape(1, rw),
        w_br_r=w_br_r.astype(BF16), w_br_s=w_br_s.astype(BF16), w_o=w_o.astype(BF16),
        g_ffn=g_ffn.reshape(1, d), w_up=w_up.astype(BF16), conv_w=conv_w,
        w_down=w_down.astype(BF16), g_ple=g_ple.reshape(1, d),
        w_ple=w_ple.astype(BF16), w_pg=w_pg.astype(BF16),
        bias2=jnp.broadcast_to(sb_bias.reshape(nh_s // 2, 2, 1), (nh_s // 2, 2, LANES)).astype(F32),
        sb_bias=sb_bias,
    )


def _pages_transposed(cache):
    n_pool, page, nh, hd = cache.shape
    return jnp.transpose(cache, (0, 2, 3, 1)).reshape(n_pool, nh * hd, page)


def _state_to_blockdiag(s):
    bsz, nh, n, _ = s.shape
    st = jnp.swapaxes(s, -1, -2).reshape(bsz, nh // 2, 2, n, n)
    eye = jnp.eye(2, dtype=s.dtype)
    bd = st[:, :, :, :, None, :] * eye[None, None, :, None, :, None]
    bd = bd.reshape(bsz, nh // 2, 2 * n, 2 * n)
    return jnp.swapaxes(bd, 1, 2).reshape(bsz, 2 * n, nh * n)


def _blockdiag_to_state(bd, nh):
    bsz = bd.shape[0]
    n = HEAD_DIM
    x = bd.reshape(bsz, 2, n, nh // 2, 2, n)
    heads = [x[:, h, :, :, h, :] for h in range(2)]
    st = jnp.stack(heads, axis=3)
    st = jnp.transpose(st, (0, 2, 3, 4, 1))
    return st.reshape(bsz, nh, n, n)


def _layer(x3, pe3, shift_prev, wkv_prev, conv_prev, past, wp, g_final, final_norm):
    bsz, t, d = x3.shape
    m = bsz * t
    rw, sw = wp["rw"], wp["sw"]
    nh = rw // HEAD_DIM
    x = x3.reshape(m, d)

    p_r = _rms_matmul(x, wp["g_mix"], wp["w_r"], F32, "proj_r")
    q = _rms_matmul(x, wp["g_mix"], wp["w_q"], BF16, "proj_q")
    k_s = _rms_matmul(x, wp["g_mix"], wp["w_k"], F32, "proj_k")
    v_s = _rms_matmul(x, wp["g_mix"], wp["w_v"], F32, "proj_v")
    gates = _rms_matmul(x, wp["g_mix"], wp["w_g"], BF16, "proj_g")

    p3 = p_r.reshape(bsz, t, -1)
    shift = wp["regroup"](shift_prev).reshape(bsz, 1, -1)
    g_c, h_c, rp, yl, gate, bonus = _wkv_intra(
        p3, shift, wp["mu"], wp["w0"], wp["a0"], wp["w2"], wp["a2"], wp["g2"],
        wp["kkw"], wp["kaw"], wp["rkw"], rw=rw, dlp=wp["dlp"], ilp=wp["ilp"], glp=wp["glp"])
    y_r, s_fin = _wkv_scan(g_c, h_c, rp, yl, gate, bonus, _state_to_blockdiag(wkv_prev),
                           wp["lng"], wp["lnb"])
    y_r = y_r[:, :t].reshape(m, rw)
    wkv_new = _blockdiag_to_state(s_fin, nh)
    last = p3[:, -1]
    dlp, ilp = wp["dlp"], wp["ilp"]
    c0 = 3 * rw
    shift_new = jnp.concatenate([
        last[:, :c0], last[:, c0:c0 + wp["dl"]], last[:, c0 + dlp:c0 + dlp + wp["il"]],
        last[:, c0 + dlp + ilp:c0 + dlp + ilp + wp["gl"]]], axis=-1)

    if past is None:
        o_s = _attn_prompt(q.reshape(bsz, t, sw), k_s.reshape(bsz, t, sw), v_s.reshape(bsz, t, sw),
                           wp["bias2"])
    else:
        cache_k, cache_v, page_table = past
        nh_s = sw // HEAD_DIM
        rows_n = nh_s * t
        q3 = q.reshape(bsz, t, sw)
        rr = jnp.arange(rows_n)[:, None] // t
        cc = jnp.arange(sw)[None, :] // HEAD_DIM
        q_bd = jnp.where(rr == cc, jnp.tile(q3, (1, nh_s, 1)), jnp.zeros((), BF16))
        k_new = _pad_to(k_s.reshape(bsz, t, sw).astype(BF16), 1, LANES)
        v_new = _pad_to(v_s.reshape(bsz, t, sw).astype(BF16), 1, LANES)
        bias_rows = jnp.broadcast_to(jnp.repeat(wp["sb_bias"].astype(F32), t)[:, None], (rows_n, LANES))
        o_s = _attn_paged(q_bd, k_new, v_new, bias_rows,
                          _pages_transposed(cache_k), _pages_transposed(cache_v), page_table, t)
    o_s = o_s.reshape(m, sw)

    h, hn = _merge(y_r, o_s, gates, x, wp["w_br_r"], wp["w_br_s"], wp["w_o"], wp["g_ffn"])
    ff = wp["w_down"].shape[0]
    nprev = conv_prev.shape[1]
    pad_rows = jnp.zeros((bsz, SUBLANES - nprev, ff), F32)
    if t == SUBLANES:
        prev8 = jnp.concatenate([conv_prev, pad_rows], axis=1).reshape(m, ff)
    else:
        prev8 = jnp.concatenate([pad_rows, conv_prev], axis=1)
    ffn_out, tail = _ffn(hn, wp["w_up"], wp["conv_w"], wp["w_down"], prev8, t)
    conv_new = tail[:, SUBLANES - nprev:]
    y = _ple_out(h, ffn_out, pe3.reshape(m, -1), wp["w_ple"], wp["w_pg"], wp["g_ple"],
                 g_final.reshape(1, d), final_norm)
    return (y.reshape(bsz, t, d), shift_new, wkv_new, conv_new,
            k_s.reshape(bsz, t, sw // HEAD_DIM, HEAD_DIM), v_s.reshape(bsz, t, sw // HEAD_DIM, HEAD_DIM))


def kernel(x_prompt, x_sample, state_shift, state_wkv, state_conv, cache_k, cache_v, page_table, p_prompt, p_sample, g_mix, w_in, sb_bias, mu_shift, w0, w2, a0, a2, g2, k_k, k_a, r_k, ln_x_g, ln_x_b, w_br_r, w_br_s, w_o, g_ffn, w_up, conv_w, w_down, g_ple, w_ple, w_pg, g_final):
    depth = w_in.shape[0]
    bsz = x_prompt.shape[0]
    rw = w0.shape[1]
    nh = rw // HEAD_DIM
    ff = w_down.shape[1]
    rc = state_shift.shape[-1]
    nprev = state_conv.shape[2]
    h_p, h_s = x_prompt, x_sample
    outs_p = [[] for _ in range(5)]
    outs_s = [[] for _ in range(5)]
    for i in range(depth):
        wp = _prep_weights(g_mix[i], w_in[i], mu_shift[i], w0[i], w2[i], a0[i], a2[i], g2[i], k_k[i],
                           k_a[i], r_k[i], ln_x_g[i], ln_x_b[i], w_br_r[i], w_br_s[i], w_o[i],
                           g_ffn[i], w_up[i], conv_w[i], w_down[i], g_ple[i], w_ple[i], w_pg[i],
                           sb_bias[i])
        last = i == depth - 1
        res_p = _layer(h_p, p_prompt[i], jnp.zeros((bsz, rc), F32),
                       jnp.zeros((bsz, nh, HEAD_DIM, HEAD_DIM), F32),
                       jnp.zeros((bsz, nprev, ff), F32), None, wp, g_final, last)
        res_s = _layer(h_s, p_sample[i], state_shift[i], state_wkv[i], state_conv[i],
                       (cache_k[i], cache_v[i], page_table), wp, g_final, last)
        h_p, h_s = res_p[0], res_s[0]
        for dst, res in ((outs_p, res_p), (outs_s, res_s)):
            for lst, val in zip(dst, res[1:]):
                lst.append(val)
    return (h_p, h_s, *(jnp.stack(o) for o in outs_p), *(jnp.stack(o) for o in outs_s))
```

```python
import functools
import math

import jax
import jax.numpy as jnp
from jax import lax
from jax.experimental import pallas as pl
from jax.experimental.pallas import tpu as pltpu

F32 = jnp.float32
BF16 = jnp.bfloat16

LANES = 128
SUBLANES = 8
V7X_SCOPED_VMEM_BYTES = 60000 * 1024

HEAD_DIM = 64
CHUNK = 128
INTRA_PAIRS_PER_STEP = 4
SCAN_PAIRS_PER_STEP = 4
DECAY_SCALE = 0.606531
NORM_EPS = 1e-6
GN_EPS = HEAD_DIM * 1e-5
KK_EPS = 1e-24


def _vmem_limit(block_bytes):
    return int(min(V7X_SCOPED_VMEM_BYTES, 2 * block_bytes + (16 << 20)))


def _pick(n, pref, align):
    if n <= pref:
        return n
    best = None
    for d in range(align, pref + 1, align):
        if n % d == 0:
            best = d
    assert best is not None, (n, pref, align)
    return best


def _pad_to(x, axis, size):
    pad = size - x.shape[axis]
    if pad == 0:
        return x
    widths = [(0, 0)] * x.ndim
    widths[axis] = (0, pad)
    return jnp.pad(x, widths)


def _ceil_to(n, m):
    return -(-n // m) * m


def _dot(a, b):
    return jnp.dot(a, b, preferred_element_type=F32)


def _dot_nt(a, b):
    return lax.dot_general(a, b, (((1,), (1,)), ((), ())), preferred_element_type=F32)


def _dot_tn(a, b):
    return lax.dot_general(a, b, (((0,), (0,)), ((), ())), preferred_element_type=F32)


def _split2(x):
    hi = x.astype(BF16)
    lo = (x - hi.astype(F32)).astype(BF16)
    return hi, lo


def _split3(x):
    hi = x.astype(BF16)
    r1 = x - hi.astype(F32)
    mid = r1.astype(BF16)
    lo = (r1 - mid.astype(F32)).astype(BF16)
    return hi, mid, lo


def _iota2(shape, dim):
    return lax.broadcasted_iota(jnp.int32, shape, dim)


def _div_pow2(x, n):
    assert n & (n - 1) == 0, n
    return x >> (n.bit_length() - 1)


def _mod_pow2(x, n):
    assert n & (n - 1) == 0, n
    return x & (n - 1)


def _head_sum(x, e2):
    hi, lo = _split2(x)
    return _dot(jnp.concatenate([hi, lo], axis=1), e2)


def _same_head_matrix():
    r = _mod_pow2(_iota2((2 * LANES, LANES), 0), LANES)
    c = _iota2((2 * LANES, LANES), 1)
    return jnp.where(_div_pow2(r, HEAD_DIM) == _div_pow2(c, HEAD_DIM), 1.0, 0.0).astype(BF16)


def _rms_proj_kernel(x_ref, g_ref, w_ref, *refs, bounds):
    o_refs, xn_ref = refs[:-1], refs[-1]
    j = pl.program_id(1)

    @pl.when(j == 0)
    def _():
        x = x_ref[...]
        ms = jnp.mean(x * x, axis=-1, keepdims=True)
        xn_ref[...] = (x * lax.rsqrt(ms + NORM_EPS) * g_ref[...]).astype(BF16)

    r = _dot(xn_ref[...], w_ref[...])
    for o_ref, (lo, hi) in zip(o_refs, bounds):
        @pl.when((j >= lo) & (j < hi))
        def _(o_ref=o_ref):
            o_ref[...] = r.astype(o_ref.dtype)


def _rms_proj(x, g, w, widths, dtypes):
    m, d = x.shape
    assert w.shape[1] == sum(widths)
    bm = _pick(m, 1024, SUBLANES)
    bn = _pick(math.gcd(*widths), 512, LANES)
    bounds, lo = [], 0
    for wd in widths:
        bounds.append((lo, lo + wd // bn))
        lo += wd // bn

    def out_spec(lo, hi):
        return pl.BlockSpec((bm, bn), lambda i, j: (i, jnp.clip(j - lo, 0, hi - lo - 1)))

    blk = bm * d * 4 + d * bn * 2 + bm * d * 2 + sum(bm * bn * jnp.dtype(t).itemsize for t in dtypes)
    return pl.pallas_call(
        functools.partial(_rms_proj_kernel, bounds=tuple(bounds)),
        out_shape=tuple(jax.ShapeDtypeStruct((m, wd), t) for wd, t in zip(widths, dtypes)),
        grid=(m // bm, lo),
        in_specs=[
            pl.BlockSpec((bm, d), lambda i, j: (i, 0)),
            pl.BlockSpec((1, d), lambda i, j: (0, 0)),
            pl.BlockSpec((d, bn), lambda i, j: (0, j)),
        ],
        out_specs=tuple(out_spec(a, b) for a, b in bounds),
        scratch_shapes=[pltpu.VMEM((bm, d), BF16)],
        compiler_params=pltpu.CompilerParams(
            dimension_semantics=("parallel", "arbitrary"),
            vmem_limit_bytes=_vmem_limit(blk)),
        name="rms_proj",
    )(x, g, w)


def _wkv_intra_kernel(r_ref, k_ref, v_ref, xw_ref, xa_ref, xg_ref,
                      rp_ref, kp_ref, vp_ref, xwp_ref, xap_ref, xgp_ref,
                      rs_ref, ks_ref, vs_ref, xws_ref, xas_ref, xgs_ref,
                      mur_ref, muk_ref, muv_ref, muw_ref, mua_ref, mug_ref,
                      w0_ref, a0_ref, w2_ref, a2_ref, g2_ref,
                      kkw_ref, kaw_ref, rkw_ref,
                      g_out, h_out, rp_out, yl_out, gate_out, bonus_out,
                      *, t_real, n_pairs):
    c_idx = pl.program_id(1)
    C = CHUNK
    first = c_idx == 0

    def mixed(x_ref, p_ref, s_ref, mu_ref):
        x = x_ref[0]
        prev_last = jnp.where(first, s_ref[0], p_ref[0][SUBLANES - 1:SUBLANES])
        rolled = pltpu.roll(x, 1, 0)
        prev = jnp.where(_iota2(x.shape, 0) == 0, prev_last, rolled)
        y = x + (prev - x) * mu_ref[...]
        if t_real < C:
            y = jnp.concatenate([y, jnp.zeros((C - t_real, y.shape[1]), F32)], axis=0)
        return y

    r_all = mixed(r_ref, rp_ref, rs_ref, mur_ref)
    k_all = mixed(k_ref, kp_ref, ks_ref, muk_ref)
    v_all = mixed(v_ref, vp_ref, vs_ref, muv_ref)
    xw = jnp.tanh(mixed(xw_ref, xwp_ref, xws_ref, muw_ref)).astype(BF16)
    xa = mixed(xa_ref, xap_ref, xas_ref, mua_ref).astype(BF16)
    xg = jax.nn.sigmoid(mixed(xg_ref, xgp_ref, xgs_ref, mug_ref)).astype(BF16)

    lw_all = -DECAY_SCALE * jax.nn.sigmoid(w0_ref[...] + _dot(xw, w2_ref[...]))
    if t_real < C:
        lw_all = jnp.where(_iota2(lw_all.shape, 0) < t_real, lw_all, 0.0)
    iclr_all = jax.nn.sigmoid(a0_ref[...] + _dot(xa, a2_ref[...]))
    gate_all = _dot(xg, g2_ref[...])
    gate_out[0] = gate_all.astype(gate_out.dtype)

    e2 = _same_head_matrix()
    row = _iota2((C, C), 0)
    col = _iota2((C, C), 1)
    strict_lower = row > col
    lower = row >= col
    l_incl = jnp.where(lower, 1.0, 0.0).astype(BF16)
    l3 = jnp.concatenate([l_incl, l_incl, l_incl], axis=1)
    lane = _iota2((C, LANES), 1)
    head_masks = (lane < HEAD_DIM, lane >= HEAD_DIM)
    rr = _iota2((LANES, LANES), 0)
    cc = _iota2((LANES, LANES), 1)
    block_mask = _div_pow2(rr, HEAD_DIM) == _div_pow2(cc, HEAD_DIM)
    eye_mask = rr == cc

    pairs = range(n_pairs)
    sls = [slice(p * LANES, (p + 1) * LANES) for p in pairs]
    r = [r_all[:, sl] for sl in sls]
    k = [k_all[:, sl] for sl in sls]
    v = [v_all[:, sl] for sl in sls]
    lw = [lw_all[:, sl] for sl in sls]
    iclr = [iclr_all[:, sl] for sl in sls]

    kkr = [k[p] * kkw_ref[:, sls[p]] for p in pairs]
    k_mod = [k[p] * (1.0 + (iclr[p] - 1.0) * kaw_ref[:, sls[p]]) for p in pairs]
    kk_ss = [_head_sum(kkr[p] * kkr[p], e2) for p in pairs]
    rk_sum = [_head_sum(r[p] * k_mod[p] * rkw_ref[:, sls[p]], e2) for p in pairs]
    cum = [_dot(l3, jnp.concatenate(_split3(lw[p]), axis=0)) for p in pairs]

    vb, em, ecl, kbar, bbar, rhs_scores, lhs_scores, am_b, rm_f = [], [], [], [], [], [], [], [], []
    for p in pairs:
        bonus_out[0, :, sls[p]] = rk_sum[p] * v[p]
        kk = kkr[p] * lax.rsqrt(jnp.maximum(kk_ss[p], KK_EPS))
        b = kk * iclr[p]
        m_row = cum[p][C // 2 - 1:C // 2]
        cum_last = cum[p][C - 1:C]
        g = cum[p] - m_row
        eng = jnp.exp(-g)
        ebar = jnp.exp(cum_last - cum[p])
        at = -kk * jnp.exp(g - lw[p])
        rt = r[p] * jnp.exp(g)
        kbar.append((k_mod[p] * ebar).astype(BF16))
        bbar.append((b * ebar).astype(BF16))
        em.append(jnp.exp(m_row))
        ecl.append(jnp.exp(cum_last))
        vb.append(v[p].astype(BF16))
        rhs_scores.append(jnp.concatenate([(k_mod[p] * eng).astype(BF16), (b * eng).astype(BF16)], axis=0))
        stack = []
        for mh in head_masks:
            am_b.append(jnp.where(mh, at, 0.0).astype(BF16))
            rm_f.append(jnp.where(mh, rt, 0.0))
            stack += [am_b[-1], rm_f[-1].astype(BF16)]
        lhs_scores.append(jnp.concatenate(stack, axis=0))

    sc = [_dot_nt(lhs_scores[p], rhs_scores[p]) for p in pairs]
    heads = range(2 * n_pairs)
    m_ak, m_ab, p_rk_b, p_rb_b = [], [], [], []
    for i in heads:
        s = sc[i // 2][(i % 2) * 2 * C:(i % 2 + 1) * 2 * C]
        m_ak.append(jnp.where(strict_lower, s[:C, :C], 0.0).astype(BF16))
        m_ab.append(jnp.where(strict_lower, s[:C, C:], 0.0))
        p_rk_b.append(jnp.where(lower, s[C:, :C], 0.0).astype(BF16))
        p_rb_b.append(jnp.where(lower, s[C:, C:], 0.0).astype(BF16))

    eye = jnp.where(row == col, 1.0, 0.0).astype(F32)
    t = [eye + m_ab[i] for i in heads]
    qb = [m_ab[i].astype(BF16) for i in heads]
    q = [_dot(qb[i], qb[i]) for i in heads]
    w1 = [_dot(m_ak[i], vb[i // 2]).astype(BF16) for i in heads]
    for _ in range(int(math.log2(C)) - 2):
        qb = [q[i].astype(BF16) for i in heads]
        st = [_dot(jnp.concatenate([t[i].astype(BF16), qb[i]], axis=0), qb[i]) for i in heads]
        t = [t[i] + st[i][:C] for i in heads]
        q = [st[i][C:] for i in heads]
    corr = [_dot(t[i].astype(BF16), q[i].astype(BF16)) for i in heads]
    t_inv = [(t[i] + corr[i]).astype(BF16) for i in heads]
    tu = [_dot(t_inv[i], jnp.concatenate([w1[i], am_b[i]], axis=1)) for i in heads]
    ul = [tu[i][:, :LANES] for i in heads]
    ap = [tu[i][:, LANES:] for i in heads]
    yl = [_dot(jnp.concatenate([p_rk_b[i], p_rb_b[i]], axis=1),
               jnp.concatenate([vb[i // 2], ul[i].astype(BF16)], axis=0)) for i in heads]
    rp_add = [_dot(p_rb_b[i], ap[i].astype(BF16)) for i in heads]

    m1 = head_masks[1]
    ul_pair = [jnp.where(m1, ul[2 * p + 1], ul[2 * p]).astype(BF16) for p in pairs]
    ap_pair = [((ap[2 * p] + ap[2 * p + 1]) * em[p]).astype(BF16) for p in pairs]
    ab = [_dot_tn(ap_pair[p], bbar[p]) for p in pairs]
    hh = [_dot_tn(jnp.concatenate([vb[p], ul_pair[p]], axis=0),
                  jnp.concatenate([kbar[p], bbar[p]], axis=0)) for p in pairs]
    for p in pairs:
        sl = sls[p]
        g_out[0, 0, :, sl] = jnp.where(block_mask, ab[p], 0.0) + jnp.where(eye_mask, ecl[p], 0.0)
        h_out[0, 0, :, sl] = jnp.where(block_mask, hh[p], 0.0)
        rp_pair = rm_f[2 * p] + rp_add[2 * p] + rm_f[2 * p + 1] + rp_add[2 * p + 1]
        rp_out[0, :, sl] = (rp_pair * em[p]).astype(rp_out.dtype)
        yl_out[0, :, sl] = jnp.where(m1, yl[2 * p + 1], yl[2 * p])


def _wkv_intra(p3, shift, mu, w0, a0, w2, a2, g2, kkw, kaw, rkw, *, rw, dlp, ilp, glp):
    bsz, t, npc = p3.shape
    C = CHUNK
    t_real = min(t, C)
    assert t % t_real == 0 and t_real % SUBLANES == 0
    nch = t // t_real
    tp = nch * C
    n_pairs = _pick(rw // LANES, INTRA_PAIRS_PER_STEP, 1)
    lw_ = n_pairs * LANES
    ng = rw // lw_
    o_w, o_a, o_g = 3 * rw, 3 * rw + dlp, 3 * rw + dlp + ilp
    assert o_w % dlp == 0 and o_a % ilp == 0 and o_g % glp == 0
    rows_prev = t_real // SUBLANES

    def cur(width, off):
        return pl.BlockSpec((1, t_real, width), lambda b, c, g, o=off // width: (b, c, o))

    def cur_g(width, off):
        return pl.BlockSpec((1, t_real, width), lambda b, c, g, o=off // width: (b, c, o + g))

    def prev(width, off):
        return pl.BlockSpec((1, SUBLANES, width),
                            lambda b, c, g, o=off // width: (b, jnp.maximum(c * rows_prev - 1, 0), o))

    def prev_g(width, off):
        return pl.BlockSpec((1, SUBLANES, width),
                            lambda b, c, g, o=off // width: (b, jnp.maximum(c * rows_prev - 1, 0), o + g))

    def sh(width, off):
        return pl.BlockSpec((1, 1, width), lambda b, c, g, o=off // width: (b, 0, o))

    def sh_g(width, off):
        return pl.BlockSpec((1, 1, width), lambda b, c, g, o=off // width: (b, 0, o + g))

    def vec(width, off):
        return pl.BlockSpec((1, width), lambda b, c, g, o=off // width: (0, o))

    def vec_g(width, off=0):
        return pl.BlockSpec((1, width), lambda b, c, g, o=off // width: (0, o + g))

    def mat_g(rows):
        return pl.BlockSpec((rows, lw_), lambda b, c, g: (0, g))

    in_specs = (
        [cur_g(lw_, 0), cur_g(lw_, rw), cur_g(lw_, 2 * rw), cur(dlp, o_w), cur(ilp, o_a), cur(glp, o_g)]
        + [prev_g(lw_, 0), prev_g(lw_, rw), prev_g(lw_, 2 * rw), prev(dlp, o_w), prev(ilp, o_a), prev(glp, o_g)]
        + [sh_g(lw_, 0), sh_g(lw_, rw), sh_g(lw_, 2 * rw), sh(dlp, o_w), sh(ilp, o_a), sh(glp, o_g)]
        + [vec_g(lw_, 0), vec_g(lw_, rw), vec_g(lw_, 2 * rw), vec(dlp, o_w), vec(ilp, o_a), vec(glp, o_g)]
        + [vec_g(lw_), vec_g(lw_), mat_g(dlp), mat_g(ilp), mat_g(glp)]
        + [vec_g(lw_), vec_g(lw_), vec_g(lw_)]
    )
    args = ([p3] * 6 + [p3] * 6 + [shift] * 6 + [mu] * 6 + [w0, a0, w2, a2, g2, kkw, kaw, rkw])
    out_shape = (
        jax.ShapeDtypeStruct((bsz, nch, LANES, rw), F32),
        jax.ShapeDtypeStruct((bsz, nch, LANES, rw), F32),
        jax.ShapeDtypeStruct((bsz, tp, rw), BF16),
        jax.ShapeDtypeStruct((bsz, tp, rw), F32),
        jax.ShapeDtypeStruct((bsz, tp, rw), BF16),
        jax.ShapeDtypeStruct((bsz, tp, rw), F32),
    )
    gh_spec = pl.BlockSpec((1, 1, LANES, lw_), lambda b, c, g: (b, c, 0, g))
    tok_spec = pl.BlockSpec((1, C, lw_), lambda b, c, g: (b, c, g))
    blk = (t_real * (3 * lw_ + dlp + ilp + glp) * 4 + (dlp + ilp + glp) * lw_ * 2
           + 2 * LANES * lw_ * 4 + C * lw_ * 12)
    return pl.pallas_call(
        functools.partial(_wkv_intra_kernel, t_real=t_real, n_pairs=n_pairs),
        out_shape=out_shape,
        grid=(bsz, nch, ng),
        in_specs=in_specs,
        out_specs=(gh_spec, gh_spec, tok_spec, tok_spec, tok_spec, tok_spec),
        compiler_params=pltpu.CompilerParams(
            dimension_semantics=("parallel", "parallel", "parallel"),
            vmem_limit_bytes=_vmem_limit(blk)),
        name="wkv_intra",
    )(*args)


def _wkv_scan_kernel(g_ref, h_ref, rp_ref, yl_ref, gate_ref, bonus_ref, s0_ref, lng_ref, lnb_ref,
                     y_ref, s_out_ref, st_ref, *, n_chunks, n_pairs):
    C = CHUNK
    N = HEAD_DIM
    e2 = _same_head_matrix()
    inv_n = 1.0 / HEAD_DIM
    lanes = [slice(p * LANES, (p + 1) * LANES) for p in range(n_pairs)]

    zero = jnp.zeros((N, N), F32)
    for p, sl in enumerate(lanes):
        st_ref[:, sl] = jnp.concatenate(
            [jnp.concatenate([s0_ref[0, 2 * p], zero], axis=1),
             jnp.concatenate([zero, s0_ref[0, 2 * p + 1]], axis=1)], axis=0)

    def body(c, carry):
        rows = pl.ds(pl.multiple_of(c * C, C), C)
        sb = [st_ref[:, sl].astype(BF16) for sl in lanes]
        y = [_dot_nt(rp_ref[0, rows, sl], s) for sl, s in zip(lanes, sb)]
        s_new = [_dot(s, g_ref[0, c, :, sl].astype(BF16)) for sl, s in zip(lanes, sb)]
        for sl, s in zip(lanes, s_new):
            st_ref[:, sl] = s + h_ref[0, c, :, sl]
        y = [yy + yl_ref[0, rows, sl] for sl, yy in zip(lanes, y)]
        mu = [_head_sum(yy, e2) * inv_n for yy in y]
        yc = [yy - m for yy, m in zip(y, mu)]
        var = [_head_sum(x * x, e2) * inv_n for x in yc]
        for sl, x, vv in zip(lanes, yc, var):
            yn = x * lax.rsqrt(vv + GN_EPS) * lng_ref[:, sl] + lnb_ref[:, sl]
            out = (yn + bonus_ref[0, rows, sl]) * gate_ref[0, rows, sl].astype(F32)
            y_ref[0, rows, sl] = out.astype(y_ref.dtype)
        return carry

    lax.fori_loop(0, n_chunks, body, 0)
    for p, sl in enumerate(lanes):
        s = st_ref[:, sl]
        s_out_ref[0, 2 * p] = s[:N, :N]
        s_out_ref[0, 2 * p + 1] = s[N:, N:]


def _wkv_scan(g, h, rp, yl, gate, bonus, s0, lng, lnb):
    bsz, nch, _, rw = g.shape
    tp = rp.shape[1]
    n_pairs = _pick(rw // LANES, SCAN_PAIRS_PER_STEP, 1)
    lw_ = n_pairs * LANES
    gh_spec = pl.BlockSpec((1, nch, LANES, lw_), lambda b, q: (b, 0, 0, q))
    tok_spec = pl.BlockSpec((1, tp, lw_), lambda b, q: (b, 0, q))
    st_spec = pl.BlockSpec((1, 2 * n_pairs, HEAD_DIM, HEAD_DIM), lambda b, q: (b, q, 0, 0))
    vec_spec = pl.BlockSpec((1, lw_), lambda b, q: (0, q))
    blk = 2 * nch * LANES * lw_ * 4 + tp * lw_ * (2 + 4 + 2 + 4 + 2) + 4 * LANES * lw_ * 4
    return pl.pallas_call(
        functools.partial(_wkv_scan_kernel, n_chunks=nch, n_pairs=n_pairs),
        out_shape=(jax.ShapeDtypeStruct((bsz, tp, rw), BF16),
                   jax.ShapeDtypeStruct(s0.shape, F32)),
        grid=(bsz, rw // lw_),
        in_specs=[gh_spec, gh_spec, tok_spec, tok_spec, tok_spec, tok_spec, st_spec, vec_spec, vec_spec],
        out_specs=(tok_spec, st_spec),
        scratch_shapes=[pltpu.VMEM((LANES, lw_), F32)],
        compiler_params=pltpu.CompilerParams(
            dimension_semantics=("parallel", "parallel"),
            vmem_limit_bytes=_vmem_limit(blk)),
        name="wkv_scan",
    )(g, h, rp, yl, gate, bonus, s0, lng, lnb)


def _cumsum_rhs():
    r = _mod_pow2(_iota2((2 * LANES, 2 * LANES), 0), LANES)
    c = _iota2((2 * LANES, 2 * LANES), 1)
    return jnp.where((c >= LANES) | (r > c), -1.0, 0.0).astype(BF16)


SOFTPLUS_LINEAR_ABOVE = 30.0


def _softplus(z, mask=None):
    sp = jnp.log(1.0 + jnp.exp(jnp.minimum(z, SOFTPLUS_LINEAR_ABOVE)))
    sp = jnp.where(z > SOFTPLUS_LINEAR_ABOVE, z, sp)
    return sp if mask is None else jnp.where(mask, sp, 0.0)


def _sb_tile(z, v_tile, run, acc, w2, mask):
    sp = _softplus(z, mask)
    cs2 = _dot(jnp.concatenate(_split2(sp), axis=1), w2)
    att = jnp.exp((z - sp) + cs2[:, :LANES] + run)
    if mask is not None:
        att = jnp.where(mask, att, 0.0)
    acc = acc + _dot(att.astype(BF16), v_tile)
    return run + cs2[:, LANES:], acc


def _attn_prompt_kernel(q_ref, k_ref, v_ref, bias_ref, o_ref, kb_ref, vb_ref, run_ref, acc_ref):
    qi = pl.program_id(2)
    qb = q_ref.shape[1]
    kb = 2 * LANES
    assert qb == kb

    @pl.when(qi == 0)
    def _():
        kb_ref[...] = k_ref[0].astype(BF16)
        vb_ref[...] = v_ref[0].astype(BF16)

    q = q_ref[0]
    m0 = _iota2((qb, LANES), 1) < HEAD_DIM
    zero = jnp.zeros((), BF16)
    qs = jnp.concatenate([jnp.where(m0, q, zero), jnp.where(m0, zero, q)], axis=0)
    w2 = _cumsum_rhs()
    run_ref[...] = jnp.zeros(run_ref.shape, F32)
    acc_ref[...] = jnp.zeros(acc_ref.shape, F32)

    def step(j, n_blocks, mask):
        nk = 2 * n_blocks
        rows = pl.ds(pl.multiple_of(j * kb, kb), n_blocks * kb)
        z = _dot_nt(qs, kb_ref[rows, :])
        b0 = jnp.concatenate([bias_ref[0, 0:1, :]] * nk, axis=1)
        b1 = jnp.concatenate([bias_ref[0, 1:2, :]] * nk, axis=1)
        z = jnp.concatenate([z[:qb] + b0, z[qb:] + b1], axis=0)
        sp = _softplus(z, mask)
        hi, lo = _split2(sp)
        tiles = [slice(c * LANES, (c + 1) * LANES) for c in range(nk)]
        cs = [_dot(jnp.concatenate([hi[:, c], lo[:, c]], axis=1), w2) for c in tiles]
        zs = z - sp
        run = run_ref[...]
        e = [None] * nk
        for c in reversed(range(nk)):
            e[c] = zs[:, tiles[c]] + cs[c][:, :LANES] + run
            run = run + cs[c][:, LANES:]
        att = jnp.exp(jnp.concatenate(e, axis=1))
        if mask is not None:
            att = jnp.where(mask, att, 0.0)
        acc_ref[...] += _dot(att.astype(BF16), vb_ref[rows, :])
        run_ref[...] = run

    qpos = _mod_pow2(_iota2((2 * qb, kb), 0), qb)
    step(qi, 1, _iota2((2 * qb, kb), 1) < qpos)

    def body(jj, carry):
        step(qi - 2 - 2 * jj, 2, None)
        return carry

    lax.fori_loop(0, qi // 2, body, 0)

    @pl.when(qi % 2 == 1)
    def _():
        step(0, 1, None)

    o_ref[0] = jnp.where(m0, acc_ref[:qb], acc_ref[qb:]).astype(o_ref.dtype)


def _attn_prompt(q, k, v, bias2):
    bsz, t, w = q.shape
    qb = 2 * LANES
    assert t % qb == 0
    blk = qb * LANES * 2 * 2 + 2 * t * LANES * 4
    return pl.pallas_call(
        _attn_prompt_kernel,
        out_shape=jax.ShapeDtypeStruct((bsz, t, w), BF16),
        grid=(bsz, w // LANES, t // qb),
        in_specs=[
            pl.BlockSpec((1, qb, LANES), lambda b, p, i: (b, i, p)),
            pl.BlockSpec((1, t, LANES), lambda b, p, i: (b, 0, p)),
            pl.BlockSpec((1, t, LANES), lambda b, p, i: (b, 0, p)),
            pl.BlockSpec((1, 2, LANES), lambda b, p, i: (p, 0, 0)),
        ],
        out_specs=pl.BlockSpec((1, qb, LANES), lambda b, p, i: (b, i, p)),
        scratch_shapes=[pltpu.VMEM((t, LANES), BF16), pltpu.VMEM((t, LANES), BF16),
                        pltpu.VMEM((2 * qb, LANES), F32), pltpu.VMEM((2 * qb, LANES), F32)],
        compiler_params=pltpu.CompilerParams(
            dimension_semantics=("parallel", "parallel", "arbitrary"),
            vmem_limit_bytes=_vmem_limit(blk + 2 * t * LANES * 2)),
        name="attn_prompt",
    )(q, k, v, bias2)


PAGES_PER_STEP = 8


def _attn_paged_kernel(pt_ref, *refs, n_groups):
    del pt_ref
    pps = PAGES_PER_STEP
    k_refs = refs[:pps]
    v_refs = refs[pps:2 * pps]
    q_ref, kn_ref, vn_ref, bias_ref, o_ref, run_ref, acc_ref = refs[2 * pps:]
    g = pl.program_id(1)
    rows_n = q_ref.shape[1]
    t_new = o_ref.shape[1]
    w2 = _cumsum_rhs()
    q = q_ref[0]
    bias = bias_ref[...]

    @pl.when(g == 0)
    def _():
        lane = _iota2((rows_n, LANES), 1)
        row = _iota2((rows_n, LANES), 0)
        mask = lane < _mod_pow2(row, t_new)
        z = _dot_nt(q, kn_ref[0]) + bias
        run, acc = _sb_tile(z, vn_ref[0], jnp.zeros((rows_n, LANES), F32),
                            jnp.zeros(acc_ref.shape, F32), w2, mask)
        run_ref[...] = run
        acc_ref[...] = acc

    pages = range(pps)
    kt = [k_refs[p][0].astype(BF16) for p in pages]
    z2 = [_dot(q, jnp.concatenate(kt[i:i + 2], axis=1)) for i in range(0, pps, 2)]
    z = [z2[p // 2][:, (p % 2) * LANES:(p % 2 + 1) * LANES] + bias for p in pages]
    sp = [_softplus(zz) for zz in z]
    cs2 = [_dot(jnp.concatenate(_split2(x), axis=1), w2) for x in sp]
    run = run_ref[...]
    att = []
    for p in pages:
        att.append(jnp.exp((z[p] - sp[p]) + cs2[p][:, :LANES] + run).astype(BF16))
        run = run + cs2[p][:, LANES:]
    run_ref[...] = run
    vt = jnp.concatenate([v_refs[p][0].astype(BF16) for p in pages], axis=1)
    acc_ref[...] += _dot_nt(jnp.concatenate(att, axis=1), vt)

    @pl.when(g == n_groups - 1)
    def _():
        acc = acc_ref[...]
        rr = _iota2(acc.shape, 0)
        cc = _iota2(acc.shape, 1)
        picked = jnp.where(_div_pow2(rr, t_new) == _div_pow2(cc, HEAD_DIM), acc, 0.0)
        out = picked[0:t_new]
        for h in range(1, rows_n // t_new):
            out = out + picked[h * t_new:(h + 1) * t_new]
        o_ref[0] = out.astype(o_ref.dtype)


def _attn_paged(q_bd, k_new, v_new, bias_rows, cache_k, cache_v, page_table, t_new):
    bsz, rows_n, w = q_bd.shape
    n_pages = page_table.shape[1]
    page = cache_k.shape[2]
    pps = PAGES_PER_STEP
    assert page == LANES and n_pages % pps == 0 and pps % 2 == 0 and cache_k.shape[1] == w
    n_groups = n_pages // pps

    def page_spec(p):
        return pl.BlockSpec((1, w, page),
                            lambda b, g, pt, p=p % pps: (pt[b, n_pages - 1 - (g * pps + p)], 0, 0))

    in_specs = ([page_spec(p) for p in range(2 * pps)] + [
        pl.BlockSpec((1, rows_n, w), lambda b, g, pt: (b, 0, 0)),
        pl.BlockSpec((1, LANES, w), lambda b, g, pt: (b, 0, 0)),
        pl.BlockSpec((1, LANES, w), lambda b, g, pt: (b, 0, 0)),
        pl.BlockSpec((rows_n, LANES), lambda b, g, pt: (0, 0)),
    ])
    blk = 2 * pps * page * w * 4 + rows_n * w * 2 + 2 * LANES * w * 2 + rows_n * w * 4
    return pl.pallas_call(
        functools.partial(_attn_paged_kernel, n_groups=n_groups),
        out_shape=jax.ShapeDtypeStruct((bsz, t_new, w), BF16),
        grid_spec=pltpu.PrefetchScalarGridSpec(
            num_scalar_prefetch=1,
            grid=(bsz, n_groups),
            in_specs=in_specs,
            out_specs=pl.BlockSpec((1, t_new, w), lambda b, g, pt: (b, 0, 0)),
            scratch_shapes=[pltpu.VMEM((rows_n, LANES), F32), pltpu.VMEM((rows_n, w), F32)],
        ),
        compiler_params=pltpu.CompilerParams(
            dimension_semantics=("parallel", "arbitrary"),
            vmem_limit_bytes=_vmem_limit(blk)),
        name="attn_paged",
    )(page_table, *([cache_k] * pps), *([cache_v] * pps), q_bd, k_new, v_new, bias_rows)


def _rms(x, g):
    ms = jnp.mean(x * x, axis=-1, keepdims=True)
    return x * lax.rsqrt(ms + NORM_EPS) * g


def _merge_kernel(yr_ref, os_ref, gr_ref, gs_ref, x_ref, wr_ref, ws_ref, wo_ref, g_ref, h_ref, hn_ref):
    a = _dot(yr_ref[...], wr_ref[...])
    b = _dot(os_ref[...], ws_ref[...])
    mixed = (jax.nn.sigmoid(gr_ref[...].astype(F32)) * a
             + jax.nn.sigmoid(gs_ref[...].astype(F32)) * b)
    h = x_ref[...] + _dot(mixed.astype(BF16), wo_ref[...])
    h_ref[...] = h
    hn_ref[...] = _rms(h, g_ref[...]).astype(hn_ref.dtype)


def _resident(shape):
    return pl.BlockSpec(shape, lambda *_: (0,) * len(shape), pipeline_mode=pl.Buffered(1))


def _merge(yr, os_, gates, x, wr, ws, wo, g_ffn):
    m, d = x.shape
    rw = yr.shape[1]
    bm = _pick(m, 256, SUBLANES)
    blk = bm * (2 * rw * 2 + 2 * d * 2 + d * 4 + d * 4 + d * 2) + (2 * rw * d + d * d)
    return pl.pallas_call(
        _merge_kernel,
        out_shape=(jax.ShapeDtypeStruct((m, d), F32), jax.ShapeDtypeStruct((m, d), BF16)),
        grid=(m // bm,),
        in_specs=[
            pl.BlockSpec((bm, rw), lambda i: (i, 0)),
            pl.BlockSpec((bm, rw), lambda i: (i, 0)),
            pl.BlockSpec((bm, d), lambda i: (i, 0)),
            pl.BlockSpec((bm, d), lambda i: (i, 1)),
            pl.BlockSpec((bm, d), lambda i: (i, 0)),
            _resident(wr.shape), _resident(ws.shape), _resident(wo.shape),
            pl.BlockSpec((1, d), lambda i: (0, 0)),
        ],
        out_specs=(pl.BlockSpec((bm, d), lambda i: (i, 0)), pl.BlockSpec((bm, d), lambda i: (i, 0))),
        compiler_params=pltpu.CompilerParams(
            dimension_semantics=("parallel",),
            vmem_limit_bytes=_vmem_limit(blk + bm * d * 16)),
        name="merge",
    )(yr, os_, gates, gates, x, wr, ws, wo, g_ffn)


def _gelu_tanh(x):
    return 0.5 * x * (1.0 + jnp.tanh(math.sqrt(2.0 / math.pi) * (x + 0.044715 * x * x * x)))


def _ffn_kernel(hn_ref, wg_ref, wv_ref, cw_ref, wd_ref, prev_ref, o_ref, tail_ref, carry_ref,
                *, blocks_per_seq, seq_len):
    i = pl.program_id(0)
    f = pl.program_id(1)
    hn = hn_ref[...]
    ug = _dot(hn, wg_ref[...])
    uv = _dot(hn, wv_ref[...])
    bm = ug.shape[0]
    row = _iota2(ug.shape, 0)
    r1 = pltpu.roll(ug, 1, 0)
    r2 = pltpu.roll(ug, 2, 0)
    if seq_len == SUBLANES:
        p2 = prev_ref[...]
        p1 = pltpu.roll(p2, bm - 1, 0)
        t = _mod_pow2(row, SUBLANES)
        s1 = jnp.where(t == 0, p1, r1)
        s2 = jnp.where(t < 2, p2, r2)
        tail_ref[...] = ug
    else:
        first = (i % blocks_per_seq) == 0
        pv = jnp.where(first, prev_ref[0], carry_ref[f])
        head = _iota2((SUBLANES, ug.shape[1]), 0)
        s1 = jnp.concatenate(
            [jnp.where(head == 0, pv[7:8], r1[:SUBLANES]), r1[SUBLANES:]], axis=0)
        s2 = jnp.concatenate(
            [jnp.where(head == 0, pv[6:7], jnp.where(head == 1, pv[7:8], r2[:SUBLANES])),
             r2[SUBLANES:]], axis=0)
        carry_ref[f] = ug[bm - SUBLANES:]
        tail_ref[0] = ug[bm - SUBLANES:]
    conv = cw_ref[0:1, :] * s2 + cw_ref[1:2, :] * s1 + cw_ref[2:3, :] * ug
    act = (_gelu_tanh(conv) * uv).astype(BF16)
    part = _dot(act, wd_ref[...])

    @pl.when(f == 0)
    def _():
        o_ref[...] = part

    @pl.when(f > 0)
    def _():
        o_ref[...] += part


def _ffn(hn, w_up, conv_w, w_down, prev8, seq_len):
    m, d = hn.shape
    ff = w_down.shape[0]
    bf = _pick(ff, 512, 2 * LANES) if ff % (2 * LANES) == 0 else _pick(ff, 512, LANES)
    nf = ff // bf
    if seq_len == SUBLANES:
        bm = m
        blocks_per_seq = 1
        prev_spec = pl.BlockSpec((bm, bf), lambda i, f: (i, f))
        tail_shape = (m, ff)
        tail_spec = pl.BlockSpec((bm, bf), lambda i, f: (i, f))
    else:
        bm = _pick(seq_len, 1024, SUBLANES)
        blocks_per_seq = seq_len // bm
        prev_spec = pl.BlockSpec((1, SUBLANES, bf), lambda i, f: (i // blocks_per_seq, 0, f))
        tail_shape = (m // bm, SUBLANES, ff)
        tail_spec = pl.BlockSpec((1, SUBLANES, bf), lambda i, f: (i, 0, f))
    blk = bm * d * (2 + 4) + 3 * d * bf * 2 + 4 * bm * bf * 4
    out, tail = pl.pallas_call(
        functools.partial(_ffn_kernel, blocks_per_seq=blocks_per_seq, seq_len=seq_len),
        out_shape=(jax.ShapeDtypeStruct((m, d), F32), jax.ShapeDtypeStruct(tail_shape, F32)),
        grid=(m // bm, nf),
        in_specs=[
            pl.BlockSpec((bm, d), lambda i, f: (i, 0)),
            pl.BlockSpec((d, bf), lambda i, f: (0, f)),
            pl.BlockSpec((d, bf), lambda i, f: (0, nf + f)),
            pl.BlockSpec((3, bf), lambda i, f: (0, f)),
            pl.BlockSpec((bf, d), lambda i, f: (f, 0)),
            prev_spec,
        ],
        out_specs=(pl.BlockSpec((bm, d), lambda i, f: (i, 0)), tail_spec),
        scratch_shapes=[pltpu.VMEM((nf, SUBLANES, bf), F32)],
        compiler_params=pltpu.CompilerParams(
            dimension_semantics=("arbitrary", "arbitrary"),
            vmem_limit_bytes=_vmem_limit(blk)),
        name="ffn",
    )(hn, w_up, w_up, conv_w, w_down, prev8)
    n_seq = m // seq_len
    if seq_len == SUBLANES:
        return out, tail.reshape(n_seq, SUBLANES, ff)
    return out, tail.reshape(n_seq, blocks_per_seq, SUBLANES, ff)[:, -1]


def _ple_kernel(h_ref, f_ref, pe_ref, wple_ref, wpg_ref, gp_ref, gf_ref, y_ref, *, final_norm):
    h2 = h_ref[...] + f_ref[...]
    hn = _rms(h2, gp_ref[...]).astype(BF16)
    gate = jax.nn.sigmoid(_dot(hn, wpg_ref[...]))
    emb = _dot(pe_ref[...].astype(BF16), wple_ref[...])
    h3 = h2 + emb * gate
    y_ref[...] = _rms(h3, gf_ref[...]) if final_norm else h3


def _ple_out(h, ffn_out, pe, w_ple, w_pg, g_ple, g_final, final_norm):
    m, d = h.shape
    pd = pe.shape[1]
    bm = _pick(m, 512, SUBLANES)
    blk = bm * (3 * d * 4 + pd * 4) + pd * d * 2 + d * d * 2
    return pl.pallas_call(
        functools.partial(_ple_kernel, final_norm=final_norm),
        out_shape=jax.ShapeDtypeStruct((m, d), F32),
        grid=(m // bm,),
        in_specs=[
            pl.BlockSpec((bm, d), lambda i: (i, 0)),
            pl.BlockSpec((bm, d), lambda i: (i, 0)),
            pl.BlockSpec((bm, pd), lambda i: (i, 0)),
            _resident(w_ple.shape), _resident(w_pg.shape),
            pl.BlockSpec((1, d), lambda i: (0, 0)),
            pl.BlockSpec((1, d), lambda i: (0, 0)),
        ],
        out_specs=pl.BlockSpec((bm, d), lambda i: (i, 0)),
        compiler_params=pltpu.CompilerParams(
            dimension_semantics=("parallel",),
            vmem_limit_bytes=_vmem_limit(blk + bm * d * 8)),
        name="ple_out",
    )(h, ffn_out, pe, w_ple, w_pg, g_ple, g_final)


def _prep_weights(g_mix, w_in, mu_shift, w0, w2, a0, a2, g2, k_k, k_a, r_k, ln_x_g, ln_x_b,
                  w_br_r, w_br_s, w_o, g_ffn, w_up, conv_w, w_down, g_ple, w_ple, w_pg, sb_bias):
    d = w_in.shape[0]
    rw = w0.shape[0]
    dl, il, gl = w2.shape[0], a2.shape[0], g2.shape[0]
    dlp, ilp, glp = (_ceil_to(n, LANES) for n in (dl, il, gl))
    sw = w_br_s.shape[0]
    rc = 3 * rw + dl + il + gl
    c_w, c_a, c_g = 3 * rw, 3 * rw + dl, 3 * rw + dl + il

    def regroup(x):
        return jnp.concatenate([
            x[..., :c_w], _pad_to(x[..., c_w:c_a], -1, dlp), _pad_to(x[..., c_a:c_g], -1, ilp),
            _pad_to(x[..., c_g:rc], -1, glp)], axis=-1)

    w_r = regroup(w_in[:, :rc]).astype(BF16)
    scale = HEAD_DIM ** -0.5
    w_q = (w_in[:, rc:rc + sw] * scale).astype(BF16)
    w_k = w_in[:, rc + sw:rc + 2 * sw].astype(BF16)
    w_v = w_in[:, rc + 2 * sw:rc + 3 * sw].astype(BF16)
    w_g = w_in[:, rc + 3 * sw:].astype(BF16)
    nh_s = sw // HEAD_DIM
    return dict(
        d=d, rw=rw, sw=sw, dl=dl, il=il, gl=gl, dlp=dlp, ilp=ilp, glp=glp, rc=rc,
        regroup=regroup,
        g_mix=g_mix.reshape(1, d), w_all=jnp.concatenate([w_r, w_q, w_k, w_v, w_g], axis=1),
        proj_widths=(w_r.shape[1], sw, sw, sw, w_g.shape[1]),
        mu=regroup(mu_shift).reshape(1, -1),
        w0=w0.reshape(1, rw), a0=a0.reshape(1, rw),
        w2=_pad_to(w2, 0, dlp).astype(BF16), a2=_pad_to(a2, 0, ilp).astype(BF16),
        g2=_pad_to(g2, 0, glp).astype(BF16),
        kkw=k_k.reshape(1, rw), kaw=k_a.reshape(1, rw), rkw=r_k.reshape(1, rw),
        lng=ln_x_g.reshape(1, rw), lnb=ln_x_b.reshape(1, rw),
        w_br_r=w_br_r.astype(BF16), w_br_s=w_br_s.astype(BF16), w_o=w_o.astype(BF16),
        g_ffn=g_ffn.reshape(1, d), w_up=w_up.astype(BF16), conv_w=conv_w,
        w_down=w_down.astype(BF16), g_ple=g_ple.reshape(1, d),
        w_ple=w_ple.astype(BF16), w_pg=w_pg.astype(BF16),
        bias2=jnp.broadcast_to(sb_bias.reshape(nh_s // 2, 2, 1), (nh_s // 2, 2, LANES)).astype(F32),
        sb_bias=sb_bias,
    )


def _pages_transposed(cache):
    n_pool, page, nh, hd = cache.shape
    return jnp.transpose(cache, (0, 2, 3, 1)).reshape(n_pool, nh * hd, page)


def _layer(x3, pe3, shift_prev, wkv_prev, conv_prev, past, wp, g_final, final_norm):
    bsz, t, d = x3.shape
    m = bsz * t
    rw, sw = wp["rw"], wp["sw"]
    nh = rw // HEAD_DIM
    x = x3.reshape(m, d)

    p_r, q, k_s, v_s, gates = _rms_proj(
        x, wp["g_mix"], wp["w_all"], wp["proj_widths"], (F32, BF16, F32, F32, BF16))

    p3 = p_r.reshape(bsz, t, -1)
    shift = wp["regroup"](shift_prev).reshape(bsz, 1, -1)
    g_c, h_c, rp, yl, gate, bonus = _wkv_intra(
        p3, shift, wp["mu"], wp["w0"], wp["a0"], wp["w2"], wp["a2"], wp["g2"],
        wp["kkw"], wp["kaw"], wp["rkw"], rw=rw, dlp=wp["dlp"], ilp=wp["ilp"], glp=wp["glp"])
    y_r, wkv_new = _wkv_scan(g_c, h_c, rp, yl, gate, bonus, wkv_prev, wp["lng"], wp["lnb"])
    y_r = y_r[:, :t].reshape(m, rw)
    last = p3[:, -1]
    dlp, ilp = wp["dlp"], wp["ilp"]
    c0 = 3 * rw
    shift_new = jnp.concatenate([
        last[:, :c0], last[:, c0:c0 + wp["dl"]], last[:, c0 + dlp:c0 + dlp + wp["il"]],
        last[:, c0 + dlp + ilp:c0 + dlp + ilp + wp["gl"]]], axis=-1)

    if past is None:
        o_s = _attn_prompt(q.reshape(bsz, t, sw), k_s.reshape(bsz, t, sw), v_s.reshape(bsz, t, sw),
                           wp["bias2"])
    else:
        cache_k, cache_v, page_table = past
        nh_s = sw // HEAD_DIM
        rows_n = nh_s * t
        q3 = q.reshape(bsz, t, sw)
        rr = jnp.arange(rows_n)[:, None] // t
        cc = jnp.arange(sw)[None, :] // HEAD_DIM
        q_bd = jnp.where(rr == cc, jnp.tile(q3, (1, nh_s, 1)), jnp.zeros((), BF16))
        k_new = _pad_to(k_s.reshape(bsz, t, sw).astype(BF16), 1, LANES)
        v_new = _pad_to(v_s.reshape(bsz, t, sw).astype(BF16), 1, LANES)
        bias_rows = jnp.broadcast_to(jnp.repeat(wp["sb_bias"].astype(F32), t)[:, None], (rows_n, LANES))
        o_s = _attn_paged(q_bd, k_new, v_new, bias_rows,
                          _pages_transposed(cache_k), _pages_transposed(cache_v), page_table, t)
    o_s = o_s.reshape(m, sw)

    h, hn = _merge(y_r, o_s, gates, x, wp["w_br_r"], wp["w_br_s"], wp["w_o"], wp["g_ffn"])
    ff = wp["w_down"].shape[0]
    nprev = conv_prev.shape[1]
    pad_rows = jnp.zeros((bsz, SUBLANES - nprev, ff), F32)
    if t == SUBLANES:
        prev8 = jnp.concatenate([conv_prev, pad_rows], axis=1).reshape(m, ff)
    else:
        prev8 = jnp.concatenate([pad_rows, conv_prev], axis=1)
    ffn_out, tail = _ffn(hn, wp["w_up"], wp["conv_w"], wp["w_down"], prev8, t)
    conv_new = tail[:, SUBLANES - nprev:]
    y = _ple_out(h, ffn_out, pe3.reshape(m, -1), wp["w_ple"], wp["w_pg"], wp["g_ple"],
                 g_final.reshape(1, d), final_norm)
    return (y.reshape(bsz, t, d), shift_new, wkv_new, conv_new,
            k_s.reshape(bsz, t, sw // HEAD_DIM, HEAD_DIM), v_s.reshape(bsz, t, sw // HEAD_DIM, HEAD_DIM))


def kernel(x_prompt, x_sample, state_shift, state_wkv, state_conv, cache_k, cache_v, page_table, p_prompt, p_sample, g_mix, w_in, sb_bias, mu_shift, w0, w2, a0, a2, g2, k_k, k_a, r_k, ln_x_g, ln_x_b, w_br_r, w_br_s, w_o, g_ffn, w_up, conv_w, w_down, g_ple, w_ple, w_pg, g_final):
    depth = w_in.shape[0]
    bsz = x_prompt.shape[0]
    rw = w0.shape[1]
    nh = rw // HEAD_DIM
    ff = w_down.shape[1]
    rc = state_shift.shape[-1]
    nprev = state_conv.shape[2]
    h_p, h_s = x_prompt, x_sample
    outs_p = [[] for _ in range(5)]
    outs_s = [[] for _ in range(5)]
    for i in range(depth):
        wp = _prep_weights(g_mix[i], w_in[i], mu_shift[i], w0[i], w2[i], a0[i], a2[i], g2[i], k_k[i],
                           k_a[i], r_k[i], ln_x_g[i], ln_x_b[i], w_br_r[i], w_br_s[i], w_o[i],
                           g_ffn[i], w_up[i], conv_w[i], w_down[i], g_ple[i], w_ple[i], w_pg[i],
                           sb_bias[i])
        last = i == depth - 1
        res_p = _layer(h_p, p_prompt[i], jnp.zeros((bsz, rc), F32),
                       jnp.zeros((bsz, nh, HEAD_DIM, HEAD_DIM), F32),
                       jnp.zeros((bsz, nprev, ff), F32), None, wp, g_final, last)
        res_s = _layer(h_s, p_sample[i], state_shift[i], state_wkv[i], state_conv[i],
                       (cache_k[i], cache_v[i], page_table), wp, g_final, last)
        h_p, h_s = res_p[0], res_s[0]
        for dst, res in ((outs_p, res_p), (outs_s, res_s)):
            for lst, val in zip(dst, res[1:]):
                lst.append(val)
    return (h_p, h_s, *(jnp.stack(o) for o in outs_p), *(jnp.stack(o) for o in outs_s))
```

```python
import functools
import math

import jax
import jax.numpy as jnp
from jax import lax
from jax.experimental import pallas as pl
from jax.experimental.pallas import tpu as pltpu

F32 = jnp.float32
BF16 = jnp.bfloat16

LANES = 128
SUBLANES = 8
V7X_SCOPED_VMEM_BYTES = 60000 * 1024

HEAD_DIM = 64
CHUNK = 128
INTRA_PAIRS_PER_STEP = 4
SCAN_PAIRS_PER_STEP = 4
DECAY_SCALE = 0.606531
NORM_EPS = 1e-6
GN_EPS = HEAD_DIM * 1e-5
KK_EPS = 1e-24


def _vmem_limit(block_bytes):
    return int(min(V7X_SCOPED_VMEM_BYTES, 2 * block_bytes + (16 << 20)))


def _pick(n, pref, align):
    if n <= pref:
        return n
    best = None
    for d in range(align, pref + 1, align):
        if n % d == 0:
            best = d
    assert best is not None, (n, pref, align)
    return best


def _pad_to(x, axis, size):
    pad = size - x.shape[axis]
    if pad == 0:
        return x
    widths = [(0, 0)] * x.ndim
    widths[axis] = (0, pad)
    return jnp.pad(x, widths)


def _ceil_to(n, m):
    return -(-n // m) * m


def _dot(a, b):
    return jnp.dot(a, b, preferred_element_type=F32)


def _dot_nt(a, b):
    return lax.dot_general(a, b, (((1,), (1,)), ((), ())), preferred_element_type=F32)


def _dot_tn(a, b):
    return lax.dot_general(a, b, (((0,), (0,)), ((), ())), preferred_element_type=F32)


def _split2(x):
    hi = x.astype(BF16)
    lo = (x - hi.astype(F32)).astype(BF16)
    return hi, lo


def _split3(x):
    hi = x.astype(BF16)
    r1 = x - hi.astype(F32)
    mid = r1.astype(BF16)
    lo = (r1 - mid.astype(F32)).astype(BF16)
    return hi, mid, lo


def _iota2(shape, dim):
    return lax.broadcasted_iota(jnp.int32, shape, dim)


def _div_pow2(x, n):
    assert n & (n - 1) == 0, n
    return x >> (n.bit_length() - 1)


def _mod_pow2(x, n):
    assert n & (n - 1) == 0, n
    return x & (n - 1)


def _head_sum(x, e2):
    hi, lo = _split2(x)
    return _dot(jnp.concatenate([hi, lo], axis=1), e2)


def _same_head_matrix():
    r = _mod_pow2(_iota2((2 * LANES, LANES), 0), LANES)
    c = _iota2((2 * LANES, LANES), 1)
    return jnp.where(_div_pow2(r, HEAD_DIM) == _div_pow2(c, HEAD_DIM), 1.0, 0.0).astype(BF16)


def _rms_proj_kernel(x_ref, g_ref, w_ref, wt_ref, *refs, bounds, transposed):
    o_refs, xn_ref = refs[:-1], refs[-1]
    j = pl.program_id(1)

    @pl.when(j == 0)
    def _():
        x = x_ref[...]
        ms = jnp.mean(x * x, axis=-1, keepdims=True)
        xn_ref[...] = (x * lax.rsqrt(ms + NORM_EPS) * g_ref[...]).astype(BF16)

    for o_ref, (lo, hi), tr in zip(o_refs, bounds, transposed):
        @pl.when((j >= lo) & (j < hi))
        def _(o_ref=o_ref, tr=tr):
            if tr:
                o_ref[0] = _dot_nt(wt_ref[...], xn_ref[...]).astype(o_ref.dtype)
            else:
                o_ref[...] = _dot(xn_ref[...], w_ref[...]).astype(o_ref.dtype)


def _rms_proj(x, g, w, w_t, widths, dtypes, transposed, seq_len):
    m, d = x.shape
    assert w.shape[1] == sum(widths)
    bm = _pick(seq_len, 1024, SUBLANES) if any(transposed) else _pick(m, 1024, SUBLANES)
    bps = seq_len // bm
    bn = _pick(math.gcd(*widths), 512, LANES)
    bounds, lo = [], 0
    for wd in widths:
        bounds.append((lo, lo + wd // bn))
        lo += wd // bn
    t_ranges = [b for b, tr in zip(bounds, transposed) if tr]
    t_lo = t_ranges[0][0] if t_ranges else 0
    t_hi = t_ranges[-1][1] if t_ranges else 0
    assert sum(hi - a for a, hi in t_ranges) == t_hi - t_lo
    if not t_ranges:
        w_t = w_t[:bn]
    assert w_t.shape == (max(t_hi - t_lo, 1) * bn, d)

    def w_index(i, j):
        return (0, jnp.where((j >= t_lo) & (j < t_hi), max(t_lo - 1, 0), j))

    def out_spec(lo, hi, tr):
        if tr:
            return pl.BlockSpec((1, bn, bm),
                                lambda i, j: (i // bps, jnp.clip(j - lo, 0, hi - lo - 1), i % bps))
        return pl.BlockSpec((bm, bn), lambda i, j: (i, jnp.clip(j - lo, 0, hi - lo - 1)))

    out_shape = tuple(
        jax.ShapeDtypeStruct((m // seq_len, wd, seq_len) if tr else (m, wd), t)
        for wd, t, tr in zip(widths, dtypes, transposed))
    blk = (bm * d * 4 + 2 * d * bn * 2 + bm * d * 2
           + sum(bm * bn * jnp.dtype(t).itemsize for t in dtypes))
    return pl.pallas_call(
        functools.partial(_rms_proj_kernel, bounds=tuple(bounds), transposed=tuple(transposed)),
        out_shape=out_shape,
        grid=(m // bm, lo),
        in_specs=[
            pl.BlockSpec((bm, d), lambda i, j: (i, 0)),
            pl.BlockSpec((1, d), lambda i, j: (0, 0)),
            pl.BlockSpec((d, bn), w_index),
            pl.BlockSpec((bn, d), lambda i, j: (jnp.clip(j - t_lo, 0, max(t_hi - t_lo, 1) - 1), 0)),
        ],
        out_specs=tuple(out_spec(a, b, tr) for (a, b), tr in zip(bounds, transposed)),
        scratch_shapes=[pltpu.VMEM((bm, d), BF16)],
        compiler_params=pltpu.CompilerParams(
            dimension_semantics=("parallel", "arbitrary"),
            vmem_limit_bytes=_vmem_limit(blk)),
        name="rms_proj",
    )(x, g, w, w_t)


def _wkv_intra_kernel(r_ref, k_ref, v_ref, xw_ref, xa_ref, xg_ref,
                      rp_ref, kp_ref, vp_ref, xwp_ref, xap_ref, xgp_ref,
                      rs_ref, ks_ref, vs_ref, xws_ref, xas_ref, xgs_ref,
                      mur_ref, muk_ref, muv_ref, muw_ref, mua_ref, mug_ref,
                      w0_ref, a0_ref, w2_ref, a2_ref, g2_ref,
                      kkw_ref, kaw_ref, rkw_ref,
                      g_out, h_out, rp_out, yl_out, gate_out, bonus_out,
                      *, t_real, n_pairs):
    c_idx = pl.program_id(1)
    C = CHUNK
    first = c_idx == 0

    def mixed(x_ref, p_ref, s_ref, mu_ref):
        x = x_ref[0]
        prev_last = jnp.where(first, s_ref[0], p_ref[0][SUBLANES - 1:SUBLANES])
        rolled = pltpu.roll(x, 1, 0)
        prev = jnp.where(_iota2(x.shape, 0) == 0, prev_last, rolled)
        y = x + (prev - x) * mu_ref[...]
        if t_real < C:
            y = jnp.concatenate([y, jnp.zeros((C - t_real, y.shape[1]), F32)], axis=0)
        return y

    r_all = mixed(r_ref, rp_ref, rs_ref, mur_ref)
    k_all = mixed(k_ref, kp_ref, ks_ref, muk_ref)
    v_all = mixed(v_ref, vp_ref, vs_ref, muv_ref)
    xw = jnp.tanh(mixed(xw_ref, xwp_ref, xws_ref, muw_ref)).astype(BF16)
    xa = mixed(xa_ref, xap_ref, xas_ref, mua_ref).astype(BF16)
    xg = jax.nn.sigmoid(mixed(xg_ref, xgp_ref, xgs_ref, mug_ref)).astype(BF16)

    lw_all = -DECAY_SCALE * jax.nn.sigmoid(w0_ref[...] + _dot(xw, w2_ref[...]))
    if t_real < C:
        lw_all = jnp.where(_iota2(lw_all.shape, 0) < t_real, lw_all, 0.0)
    iclr_all = jax.nn.sigmoid(a0_ref[...] + _dot(xa, a2_ref[...]))
    gate_all = _dot(xg, g2_ref[...])
    gate_out[0] = gate_all.astype(gate_out.dtype)

    e2 = _same_head_matrix()
    row = _iota2((C, C), 0)
    col = _iota2((C, C), 1)
    strict_lower = row > col
    lower = row >= col
    l_incl = jnp.where(lower, 1.0, 0.0).astype(BF16)
    l3 = jnp.concatenate([l_incl, l_incl, l_incl], axis=1)
    lane = _iota2((C, LANES), 1)
    head_masks = (lane < HEAD_DIM, lane >= HEAD_DIM)
    rr = _iota2((LANES, LANES), 0)
    cc = _iota2((LANES, LANES), 1)
    block_mask = _div_pow2(rr, HEAD_DIM) == _div_pow2(cc, HEAD_DIM)
    eye_mask = rr == cc

    pairs = range(n_pairs)
    sls = [slice(p * LANES, (p + 1) * LANES) for p in pairs]
    r = [r_all[:, sl] for sl in sls]
    k = [k_all[:, sl] for sl in sls]
    v = [v_all[:, sl] for sl in sls]
    lw = [lw_all[:, sl] for sl in sls]
    iclr = [iclr_all[:, sl] for sl in sls]

    kkr = [k[p] * kkw_ref[:, sls[p]] for p in pairs]
    k_mod = [k[p] * (1.0 + (iclr[p] - 1.0) * kaw_ref[:, sls[p]]) for p in pairs]
    kk_ss = [_head_sum(kkr[p] * kkr[p], e2) for p in pairs]
    rk_sum = [_head_sum(r[p] * k_mod[p] * rkw_ref[:, sls[p]], e2) for p in pairs]
    cum = [_dot(l3, jnp.concatenate(_split3(lw[p]), axis=0)) for p in pairs]

    vb, em, ecl, kbar, bbar, rhs_scores, lhs_scores, am_b, rm_f = [], [], [], [], [], [], [], [], []
    for p in pairs:
        bonus_out[0, :, sls[p]] = rk_sum[p] * v[p]
        kk = kkr[p] * lax.rsqrt(jnp.maximum(kk_ss[p], KK_EPS))
        b = kk * iclr[p]
        m_row = cum[p][C // 2 - 1:C // 2]
        cum_last = cum[p][C - 1:C]
        g = cum[p] - m_row
        eng = jnp.exp(-g)
        ebar = jnp.exp(cum_last - cum[p])
        at = -kk * jnp.exp(g - lw[p])
        rt = r[p] * jnp.exp(g)
        kbar.append((k_mod[p] * ebar).astype(BF16))
        bbar.append((b * ebar).astype(BF16))
        em.append(jnp.exp(m_row))
        ecl.append(jnp.exp(cum_last))
        vb.append(v[p].astype(BF16))
        rhs_scores.append(jnp.concatenate([(k_mod[p] * eng).astype(BF16), (b * eng).astype(BF16)], axis=0))
        stack = []
        for mh in head_masks:
            am_b.append(jnp.where(mh, at, 0.0).astype(BF16))
            rm_f.append(jnp.where(mh, rt, 0.0))
            stack += [am_b[-1], rm_f[-1].astype(BF16)]
        lhs_scores.append(jnp.concatenate(stack, axis=0))

    sc = [_dot_nt(lhs_scores[p], rhs_scores[p]) for p in pairs]
    heads = range(2 * n_pairs)
    m_ak, m_ab, p_rk_b, p_rb_b = [], [], [], []
    for i in heads:
        s = sc[i // 2][(i % 2) * 2 * C:(i % 2 + 1) * 2 * C]
        m_ak.append(jnp.where(strict_lower, s[:C, :C], 0.0).astype(BF16))
        m_ab.append(jnp.where(strict_lower, s[:C, C:], 0.0))
        p_rk_b.append(jnp.where(lower, s[C:, :C], 0.0).astype(BF16))
        p_rb_b.append(jnp.where(lower, s[C:, C:], 0.0).astype(BF16))

    eye = jnp.where(row == col, 1.0, 0.0).astype(F32)
    t = [eye + m_ab[i] for i in heads]
    qb = [m_ab[i].astype(BF16) for i in heads]
    q = [_dot(qb[i], qb[i]) for i in heads]
    w1 = [_dot(m_ak[i], vb[i // 2]).astype(BF16) for i in heads]
    for _ in range(int(math.log2(C)) - 2):
        qb = [q[i].astype(BF16) for i in heads]
        st = [_dot(jnp.concatenate([t[i].astype(BF16), qb[i]], axis=0), qb[i]) for i in heads]
        t = [t[i] + st[i][:C] for i in heads]
        q = [st[i][C:] for i in heads]
    corr = [_dot(t[i].astype(BF16), q[i].astype(BF16)) for i in heads]
    t_inv = [(t[i] + corr[i]).astype(BF16) for i in heads]
    tu = [_dot(t_inv[i], jnp.concatenate([w1[i], am_b[i]], axis=1)) for i in heads]
    ul = [tu[i][:, :LANES] for i in heads]
    ap = [tu[i][:, LANES:] for i in heads]
    yl = [_dot(jnp.concatenate([p_rk_b[i], p_rb_b[i]], axis=1),
               jnp.concatenate([vb[i // 2], ul[i].astype(BF16)], axis=0)) for i in heads]
    rp_add = [_dot(p_rb_b[i], ap[i].astype(BF16)) for i in heads]

    m1 = head_masks[1]
    ul_pair = [jnp.where(m1, ul[2 * p + 1], ul[2 * p]).astype(BF16) for p in pairs]
    ap_pair = [((ap[2 * p] + ap[2 * p + 1]) * em[p]).astype(BF16) for p in pairs]
    ab = [_dot_tn(ap_pair[p], bbar[p]) for p in pairs]
    hh = [_dot_tn(jnp.concatenate([vb[p], ul_pair[p]], axis=0),
                  jnp.concatenate([kbar[p], bbar[p]], axis=0)) for p in pairs]
    for p in pairs:
        sl = sls[p]
        g_out[0, 0, :, sl] = jnp.where(block_mask, ab[p], 0.0) + jnp.where(eye_mask, ecl[p], 0.0)
        h_out[0, 0, :, sl] = jnp.where(block_mask, hh[p], 0.0)
        rp_pair = rm_f[2 * p] + rp_add[2 * p] + rm_f[2 * p + 1] + rp_add[2 * p + 1]
        rp_out[0, :, sl] = (rp_pair * em[p]).astype(rp_out.dtype)
        yl_out[0, :, sl] = jnp.where(m1, yl[2 * p + 1], yl[2 * p])


def _wkv_intra(p3, shift, mu, w0, a0, w2, a2, g2, kkw, kaw, rkw, *, rw, dlp, ilp, glp):
    bsz, t, npc = p3.shape
    C = CHUNK
    t_real = min(t, C)
    assert t % t_real == 0 and t_real % SUBLANES == 0
    nch = t // t_real
    tp = nch * C
    n_pairs = _pick(rw // LANES, INTRA_PAIRS_PER_STEP, 1)
    lw_ = n_pairs * LANES
    ng = rw // lw_
    o_w, o_a, o_g = 3 * rw, 3 * rw + dlp, 3 * rw + dlp + ilp
    assert o_w % dlp == 0 and o_a % ilp == 0 and o_g % glp == 0
    rows_prev = t_real // SUBLANES

    def cur(width, off):
        return pl.BlockSpec((1, t_real, width), lambda b, c, g, o=off // width: (b, c, o))

    def cur_g(width, off):
        return pl.BlockSpec((1, t_real, width), lambda b, c, g, o=off // width: (b, c, o + g))

    def prev(width, off):
        return pl.BlockSpec((1, SUBLANES, width),
                            lambda b, c, g, o=off // width: (b, jnp.maximum(c * rows_prev - 1, 0), o))

    def prev_g(width, off):
        return pl.BlockSpec((1, SUBLANES, width),
                            lambda b, c, g, o=off // width: (b, jnp.maximum(c * rows_prev - 1, 0), o + g))

    def sh(width, off):
        return pl.BlockSpec((1, 1, width), lambda b, c, g, o=off // width: (b, 0, o))

    def sh_g(width, off):
        return pl.BlockSpec((1, 1, width), lambda b, c, g, o=off // width: (b, 0, o + g))

    def vec(width, off):
        return pl.BlockSpec((1, width), lambda b, c, g, o=off // width: (0, o))

    def vec_g(width, off=0):
        return pl.BlockSpec((1, width), lambda b, c, g, o=off // width: (0, o + g))

    def mat_g(rows):
        return pl.BlockSpec((rows, lw_), lambda b, c, g: (0, g))

    in_specs = (
        [cur_g(lw_, 0), cur_g(lw_, rw), cur_g(lw_, 2 * rw), cur(dlp, o_w), cur(ilp, o_a), cur(glp, o_g)]
        + [prev_g(lw_, 0), prev_g(lw_, rw), prev_g(lw_, 2 * rw), prev(dlp, o_w), prev(ilp, o_a), prev(glp, o_g)]
        + [sh_g(lw_, 0), sh_g(lw_, rw), sh_g(lw_, 2 * rw), sh(dlp, o_w), sh(ilp, o_a), sh(glp, o_g)]
        + [vec_g(lw_, 0), vec_g(lw_, rw), vec_g(lw_, 2 * rw), vec(dlp, o_w), vec(ilp, o_a), vec(glp, o_g)]
        + [vec_g(lw_), vec_g(lw_), mat_g(dlp), mat_g(ilp), mat_g(glp)]
        + [vec_g(lw_), vec_g(lw_), vec_g(lw_)]
    )
    args = ([p3] * 6 + [p3] * 6 + [shift] * 6 + [mu] * 6 + [w0, a0, w2, a2, g2, kkw, kaw, rkw])
    out_shape = (
        jax.ShapeDtypeStruct((bsz, nch, LANES, rw), F32),
        jax.ShapeDtypeStruct((bsz, nch, LANES, rw), F32),
        jax.ShapeDtypeStruct((bsz, tp, rw), BF16),
        jax.ShapeDtypeStruct((bsz, tp, rw), F32),
        jax.ShapeDtypeStruct((bsz, tp, rw), BF16),
        jax.ShapeDtypeStruct((bsz, tp, rw), F32),
    )
    gh_spec = pl.BlockSpec((1, 1, LANES, lw_), lambda b, c, g: (b, c, 0, g))
    tok_spec = pl.BlockSpec((1, C, lw_), lambda b, c, g: (b, c, g))
    blk = (t_real * (3 * lw_ + dlp + ilp + glp) * 4 + (dlp + ilp + glp) * lw_ * 2
           + 2 * LANES * lw_ * 4 + C * lw_ * 12)
    return pl.pallas_call(
        functools.partial(_wkv_intra_kernel, t_real=t_real, n_pairs=n_pairs),
        out_shape=out_shape,
        grid=(bsz, nch, ng),
        in_specs=in_specs,
        out_specs=(gh_spec, gh_spec, tok_spec, tok_spec, tok_spec, tok_spec),
        compiler_params=pltpu.CompilerParams(
            dimension_semantics=("parallel", "parallel", "parallel"),
            vmem_limit_bytes=_vmem_limit(blk)),
        name="wkv_intra",
    )(*args)


def _wkv_scan_kernel(g_ref, h_ref, rp_ref, yl_ref, gate_ref, bonus_ref, s0_ref, lng_ref, lnb_ref,
                     y_ref, s_out_ref, st_ref, *, n_chunks, n_pairs):
    C = CHUNK
    N = HEAD_DIM
    e2 = _same_head_matrix()
    inv_n = 1.0 / HEAD_DIM
    lanes = [slice(p * LANES, (p + 1) * LANES) for p in range(n_pairs)]

    zero = jnp.zeros((N, N), F32)
    for p, sl in enumerate(lanes):
        st_ref[:, sl] = jnp.concatenate(
            [jnp.concatenate([s0_ref[0, 2 * p], zero], axis=1),
             jnp.concatenate([zero, s0_ref[0, 2 * p + 1]], axis=1)], axis=0)

    def body(c, carry):
        rows = pl.ds(pl.multiple_of(c * C, C), C)
        sb = [st_ref[:, sl].astype(BF16) for sl in lanes]
        y = [_dot_nt(rp_ref[0, rows, sl], s) for sl, s in zip(lanes, sb)]
        s_new = [_dot(s, g_ref[0, c, :, sl].astype(BF16)) for sl, s in zip(lanes, sb)]
        for sl, s in zip(lanes, s_new):
            st_ref[:, sl] = s + h_ref[0, c, :, sl]
        y = [yy + yl_ref[0, rows, sl] for sl, yy in zip(lanes, y)]
        mu = [_head_sum(yy, e2) * inv_n for yy in y]
        yc = [yy - m for yy, m in zip(y, mu)]
        var = [_head_sum(x * x, e2) * inv_n for x in yc]
        for sl, x, vv in zip(lanes, yc, var):
            yn = x * lax.rsqrt(vv + GN_EPS) * lng_ref[:, sl] + lnb_ref[:, sl]
            out = (yn + bonus_ref[0, rows, sl]) * gate_ref[0, rows, sl].astype(F32)
            y_ref[0, rows, sl] = out.astype(y_ref.dtype)
        return carry

    lax.fori_loop(0, n_chunks, body, 0)
    for p, sl in enumerate(lanes):
        s = st_ref[:, sl]
        s_out_ref[0, 2 * p] = s[:N, :N]
        s_out_ref[0, 2 * p + 1] = s[N:, N:]


def _wkv_scan(g, h, rp, yl, gate, bonus, s0, lng, lnb):
    bsz, nch, _, rw = g.shape
    tp = rp.shape[1]
    n_pairs = _pick(rw // LANES, SCAN_PAIRS_PER_STEP, 1)
    lw_ = n_pairs * LANES
    gh_spec = pl.BlockSpec((1, nch, LANES, lw_), lambda b, q: (b, 0, 0, q))
    tok_spec = pl.BlockSpec((1, tp, lw_), lambda b, q: (b, 0, q))
    st_spec = pl.BlockSpec((1, 2 * n_pairs, HEAD_DIM, HEAD_DIM), lambda b, q: (b, q, 0, 0))
    vec_spec = pl.BlockSpec((1, lw_), lambda b, q: (0, q))
    blk = 2 * nch * LANES * lw_ * 4 + tp * lw_ * (2 + 4 + 2 + 4 + 2) + 4 * LANES * lw_ * 4
    return pl.pallas_call(
        functools.partial(_wkv_scan_kernel, n_chunks=nch, n_pairs=n_pairs),
        out_shape=(jax.ShapeDtypeStruct((bsz, tp, rw), BF16),
                   jax.ShapeDtypeStruct(s0.shape, F32)),
        grid=(bsz, rw // lw_),
        in_specs=[gh_spec, gh_spec, tok_spec, tok_spec, tok_spec, tok_spec, st_spec, vec_spec, vec_spec],
        out_specs=(tok_spec, st_spec),
        scratch_shapes=[pltpu.VMEM((LANES, lw_), F32)],
        compiler_params=pltpu.CompilerParams(
            dimension_semantics=("parallel", "parallel"),
            vmem_limit_bytes=_vmem_limit(blk)),
        name="wkv_scan",
    )(g, h, rp, yl, gate, bonus, s0, lng, lnb)


def _cumsum_rhs():
    r = _iota2((LANES, 2 * LANES), 0)
    c = _iota2((LANES, 2 * LANES), 1)
    return jnp.where((c >= LANES) | (r > c), -1.0, 0.0).astype(BF16)


def _keep_sums(sp, w2):
    return _dot(sp.astype(BF16), w2)


SOFTPLUS_LINEAR_ABOVE = 30.0


def _softplus(z, mask=None):
    sp = jnp.log(1.0 + jnp.exp(jnp.minimum(z, SOFTPLUS_LINEAR_ABOVE)))
    sp = jnp.where(z > SOFTPLUS_LINEAR_ABOVE, z, sp)
    return sp if mask is None else jnp.where(mask, sp, 0.0)


def _sb_tile(z, v_tile, run, acc, w2, mask):
    sp = _softplus(z, mask)
    cs2 = _keep_sums(sp, w2)
    att = jnp.exp((z - sp) + cs2[:, :LANES] + run)
    if mask is not None:
        att = jnp.where(mask, att, 0.0)
    acc = acc + _dot(att.astype(BF16), v_tile)
    return run + cs2[:, LANES:], acc


def _attn_prompt_kernel(q_ref, k_ref, v_ref, bias_ref, o_ref, kb_ref, vb_ref, run_ref, acc_ref):
    qi = pl.program_id(2)
    qb = q_ref.shape[1]
    kb = 2 * LANES
    n_diag = qb // kb
    assert qb == n_diag * kb and n_diag in (1, 2)

    @pl.when(qi == 0)
    def _():
        for c in range(kb_ref.shape[0]):
            kb_ref[c] = k_ref[0, :, c * LANES:(c + 1) * LANES].astype(BF16)
            vb_ref[c] = v_ref[0, :, c * LANES:(c + 1) * LANES].astype(BF16)

    q = q_ref[0]
    m0 = _iota2((qb, LANES), 1) < HEAD_DIM
    zero = jnp.zeros((), BF16)
    qs = jnp.concatenate([jnp.where(m0, q, zero), jnp.where(m0, zero, q)], axis=0)
    w2 = _cumsum_rhs()
    run_ref[...] = jnp.zeros(run_ref.shape, F32)
    acc_ref[...] = jnp.zeros(acc_ref.shape, F32)

    def step(j, n_blocks, mask):
        nk = 2 * n_blocks
        t0 = 2 * j
        z = _dot(qs, jnp.concatenate([kb_ref[t0 + c] for c in range(nk)], axis=1))
        b0 = jnp.concatenate([bias_ref[0, 0:1, :]] * nk, axis=1)
        b1 = jnp.concatenate([bias_ref[0, 1:2, :]] * nk, axis=1)
        z = jnp.concatenate([z[:qb] + b0, z[qb:] + b1], axis=0)
        sp = _softplus(z, mask)
        tiles = [slice(c * LANES, (c + 1) * LANES) for c in range(nk)]
        cs = [_keep_sums(sp[:, c], w2) for c in tiles]
        zs = z - sp
        run = run_ref[...]
        e = [None] * nk
        for c in reversed(range(nk)):
            e[c] = zs[:, tiles[c]] + cs[c][:, :LANES] + run
            run = run + cs[c][:, LANES:]
        att = jnp.exp(jnp.concatenate(e, axis=1))
        if mask is not None:
            att = jnp.where(mask, att, 0.0)
        vt = jnp.concatenate([vb_ref[t0 + c] for c in range(nk)], axis=1)
        acc_ref[...] += _dot_nt(att.astype(BF16), vt)
        run_ref[...] = run

    qpos = _mod_pow2(_iota2((2 * qb, qb), 0), qb)
    step(qi * n_diag, n_diag, _iota2((2 * qb, qb), 1) < qpos)

    n_off = qi * n_diag

    def body(jj, carry):
        step(n_off - 2 - 2 * jj, 2, None)
        return carry

    lax.fori_loop(0, n_off // 2, body, 0)

    if n_diag % 2 == 1:
        @pl.when(n_off % 2 == 1)
        def _():
            step(0, 1, None)

    o_ref[0] = jnp.where(m0, acc_ref[:qb], acc_ref[qb:]).astype(o_ref.dtype)


def _attn_prompt(q, k, v, bias2):
    bsz, t, w = q.shape
    qb = 4 * LANES if t % (4 * LANES) == 0 else 2 * LANES
    assert t % qb == 0
    blk = qb * LANES * 2 * 2 + 2 * t * LANES * 4
    return pl.pallas_call(
        _attn_prompt_kernel,
        out_shape=jax.ShapeDtypeStruct((bsz, t, w), BF16),
        grid=(bsz, w // LANES, t // qb),
        in_specs=[
            pl.BlockSpec((1, qb, LANES), lambda b, p, i: (b, i, p)),
            pl.BlockSpec((1, LANES, t), lambda b, p, i: (b, p, 0)),
            pl.BlockSpec((1, LANES, t), lambda b, p, i: (b, p, 0)),
            pl.BlockSpec((1, 2, LANES), lambda b, p, i: (p, 0, 0)),
        ],
        out_specs=pl.BlockSpec((1, qb, LANES), lambda b, p, i: (b, i, p)),
        scratch_shapes=[pltpu.VMEM((t // LANES, LANES, LANES), BF16),
                        pltpu.VMEM((t // LANES, LANES, LANES), BF16),
                        pltpu.VMEM((2 * qb, LANES), F32), pltpu.VMEM((2 * qb, LANES), F32)],
        compiler_params=pltpu.CompilerParams(
            dimension_semantics=("parallel", "parallel", "arbitrary"),
            vmem_limit_bytes=_vmem_limit(blk + 2 * t * LANES * 2)),
        name="attn_prompt",
    )(q, k, v, bias2)


PAGES_PER_STEP = 8


def _attn_paged_kernel(pt_ref, *refs, n_groups):
    del pt_ref
    pps = PAGES_PER_STEP
    k_refs = refs[:pps]
    v_refs = refs[pps:2 * pps]
    q_ref, kn_ref, vn_ref, bias_ref, o_ref, run_ref, acc_ref = refs[2 * pps:]
    g = pl.program_id(1)
    rows_n = q_ref.shape[1]
    t_new = o_ref.shape[1]
    w2 = _cumsum_rhs()
    q = q_ref[0]
    bias = bias_ref[...]

    @pl.when(g == 0)
    def _():
        lane = _iota2((rows_n, LANES), 1)
        row = _iota2((rows_n, LANES), 0)
        mask = lane < _mod_pow2(row, t_new)
        z = _dot_nt(q, kn_ref[0]) + bias
        run, acc = _sb_tile(z, vn_ref[0], jnp.zeros((rows_n, LANES), F32),
                            jnp.zeros(acc_ref.shape, F32), w2, mask)
        run_ref[...] = run
        acc_ref[...] = acc

    pages = range(pps)
    kt = [k_refs[p][0].astype(BF16) for p in pages]
    z2 = [_dot(q, jnp.concatenate(kt[i:i + 2], axis=1)) for i in range(0, pps, 2)]
    z = [z2[p // 2][:, (p % 2) * LANES:(p % 2 + 1) * LANES] + bias for p in pages]
    sp = [_softplus(zz) for zz in z]
    cs2 = [_keep_sums(x, w2) for x in sp]
    run = run_ref[...]
    att = []
    for p in pages:
        att.append(jnp.exp((z[p] - sp[p]) + cs2[p][:, :LANES] + run).astype(BF16))
        run = run + cs2[p][:, LANES:]
    run_ref[...] = run
    vt = jnp.concatenate([v_refs[p][0].astype(BF16) for p in pages], axis=1)
    acc_ref[...] += _dot_nt(jnp.concatenate(att, axis=1), vt)

    @pl.when(g == n_groups - 1)
    def _():
        acc = acc_ref[...]
        rr = _iota2(acc.shape, 0)
        cc = _iota2(acc.shape, 1)
        picked = jnp.where(_div_pow2(rr, t_new) == _div_pow2(cc, HEAD_DIM), acc, 0.0)
        out = picked[0:t_new]
        for h in range(1, rows_n // t_new):
            out = out + picked[h * t_new:(h + 1) * t_new]
        o_ref[0] = out.astype(o_ref.dtype)


def _attn_paged(q_bd, k_new, v_new, bias_rows, cache_k, cache_v, page_table, t_new):
    bsz, rows_n, w = q_bd.shape
    n_pages = page_table.shape[1]
    page = cache_k.shape[2]
    pps = PAGES_PER_STEP
    assert page == LANES and n_pages % pps == 0 and pps % 2 == 0 and cache_k.shape[1] == w
    n_groups = n_pages // pps

    def page_spec(p):
        return pl.BlockSpec((1, w, page),
                            lambda b, g, pt, p=p % pps: (pt[b, n_pages - 1 - (g * pps + p)], 0, 0))

    in_specs = ([page_spec(p) for p in range(2 * pps)] + [
        pl.BlockSpec((1, rows_n, w), lambda b, g, pt: (b, 0, 0)),
        pl.BlockSpec((1, LANES, w), lambda b, g, pt: (b, 0, 0)),
        pl.BlockSpec((1, LANES, w), lambda b, g, pt: (b, 0, 0)),
        pl.BlockSpec((rows_n, LANES), lambda b, g, pt: (0, 0)),
    ])
    blk = 2 * pps * page * w * 4 + rows_n * w * 2 + 2 * LANES * w * 2 + rows_n * w * 4
    return pl.pallas_call(
        functools.partial(_attn_paged_kernel, n_groups=n_groups),
        out_shape=jax.ShapeDtypeStruct((bsz, t_new, w), BF16),
        grid_spec=pltpu.PrefetchScalarGridSpec(
            num_scalar_prefetch=1,
            grid=(bsz, n_groups),
            in_specs=in_specs,
            out_specs=pl.BlockSpec((1, t_new, w), lambda b, g, pt: (b, 0, 0)),
            scratch_shapes=[pltpu.VMEM((rows_n, LANES), F32), pltpu.VMEM((rows_n, w), F32)],
        ),
        compiler_params=pltpu.CompilerParams(
            dimension_semantics=("parallel", "arbitrary"),
            vmem_limit_bytes=_vmem_limit(blk)),
        name="attn_paged",
    )(page_table, *([cache_k] * pps), *([cache_v] * pps), q_bd, k_new, v_new, bias_rows)


def _rms(x, g):
    ms = jnp.mean(x * x, axis=-1, keepdims=True)
    return x * lax.rsqrt(ms + NORM_EPS) * g


def _merge_kernel(yr_ref, os_ref, gr_ref, gs_ref, x_ref, wr_ref, ws_ref, wo_ref, g_ref, h_ref, hn_ref):
    a = _dot(yr_ref[...], wr_ref[...])
    b = _dot(os_ref[...], ws_ref[...])
    mixed = (jax.nn.sigmoid(gr_ref[...].astype(F32)) * a
             + jax.nn.sigmoid(gs_ref[...].astype(F32)) * b)
    h = x_ref[...] + _dot(mixed.astype(BF16), wo_ref[...])
    h_ref[...] = h
    hn_ref[...] = _rms(h, g_ref[...]).astype(hn_ref.dtype)


def _resident(shape):
    return pl.BlockSpec(shape, lambda *_: (0,) * len(shape), pipeline_mode=pl.Buffered(1))


def _merge(yr, os_, gates, x, wr, ws, wo, g_ffn):
    m, d = x.shape
    rw = yr.shape[1]
    bm = _pick(m, 256, SUBLANES)
    blk = bm * (2 * rw * 2 + 2 * d * 2 + d * 4 + d * 4 + d * 2) + (2 * rw * d + d * d)
    return pl.pallas_call(
        _merge_kernel,
        out_shape=(jax.ShapeDtypeStruct((m, d), F32), jax.ShapeDtypeStruct((m, d), BF16)),
        grid=(m // bm,),
        in_specs=[
            pl.BlockSpec((bm, rw), lambda i: (i, 0)),
            pl.BlockSpec((bm, rw), lambda i: (i, 0)),
            pl.BlockSpec((bm, d), lambda i: (i, 0)),
            pl.BlockSpec((bm, d), lambda i: (i, 1)),
            pl.BlockSpec((bm, d), lambda i: (i, 0)),
            _resident(wr.shape), _resident(ws.shape), _resident(wo.shape),
            pl.BlockSpec((1, d), lambda i: (0, 0)),
        ],
        out_specs=(pl.BlockSpec((bm, d), lambda i: (i, 0)), pl.BlockSpec((bm, d), lambda i: (i, 0))),
        compiler_params=pltpu.CompilerParams(
            dimension_semantics=("parallel",),
            vmem_limit_bytes=_vmem_limit(blk + bm * d * 16)),
        name="merge",
    )(yr, os_, gates, gates, x, wr, ws, wo, g_ffn)


def _gelu_tanh(x):
    return 0.5 * x * (1.0 + jnp.tanh(math.sqrt(2.0 / math.pi) * (x + 0.044715 * x * x * x)))


def _ffn_kernel(hn_ref, wg_ref, wv_ref, cw_ref, wd_ref, prev_ref, o_ref, tail_ref, carry_ref,
                *, blocks_per_seq, seq_len):
    i = pl.program_id(0)
    f = pl.program_id(1)
    hn = hn_ref[...]
    ug = _dot(hn, wg_ref[...])
    uv = _dot(hn, wv_ref[...])
    bm = ug.shape[0]
    row = _iota2(ug.shape, 0)
    r1 = pltpu.roll(ug, 1, 0)
    r2 = pltpu.roll(ug, 2, 0)
    if seq_len == SUBLANES:
        p2 = prev_ref[...]
        p1 = pltpu.roll(p2, bm - 1, 0)
        t = _mod_pow2(row, SUBLANES)
        s1 = jnp.where(t == 0, p1, r1)
        s2 = jnp.where(t < 2, p2, r2)
        tail_ref[...] = ug
    else:
        first = (i % blocks_per_seq) == 0
        pv = jnp.where(first, prev_ref[0], carry_ref[f])
        head = _iota2((SUBLANES, ug.shape[1]), 0)
        s1 = jnp.concatenate(
            [jnp.where(head == 0, pv[7:8], r1[:SUBLANES]), r1[SUBLANES:]], axis=0)
        s2 = jnp.concatenate(
            [jnp.where(head == 0, pv[6:7], jnp.where(head == 1, pv[7:8], r2[:SUBLANES])),
             r2[SUBLANES:]], axis=0)
        carry_ref[f] = ug[bm - SUBLANES:]
        tail_ref[0] = ug[bm - SUBLANES:]
    conv = cw_ref[0:1, :] * s2 + cw_ref[1:2, :] * s1 + cw_ref[2:3, :] * ug
    act = (_gelu_tanh(conv) * uv).astype(BF16)
    part = _dot(act, wd_ref[...])

    @pl.when(f == 0)
    def _():
        o_ref[...] = part

    @pl.when(f > 0)
    def _():
        o_ref[...] += part


def _ffn(hn, w_up, conv_w, w_down, prev8, seq_len):
    m, d = hn.shape
    ff = w_down.shape[0]
    bf = _pick(ff, 512, 2 * LANES) if ff % (2 * LANES) == 0 else _pick(ff, 512, LANES)
    nf = ff // bf
    if seq_len == SUBLANES:
        bm = m
        blocks_per_seq = 1
        prev_spec = pl.BlockSpec((bm, bf), lambda i, f: (i, f))
        tail_shape = (m, ff)
        tail_spec = pl.BlockSpec((bm, bf), lambda i, f: (i, f))
    else:
        bm = _pick(seq_len, 1024, SUBLANES)
        blocks_per_seq = seq_len // bm
        prev_spec = pl.BlockSpec((1, SUBLANES, bf), lambda i, f: (i // blocks_per_seq, 0, f))
        tail_shape = (m // bm, SUBLANES, ff)
        tail_spec = pl.BlockSpec((1, SUBLANES, bf), lambda i, f: (i, 0, f))
    blk = bm * d * (2 + 4) + 3 * d * bf * 2 + 4 * bm * bf * 4
    out, tail = pl.pallas_call(
        functools.partial(_ffn_kernel, blocks_per_seq=blocks_per_seq, seq_len=seq_len),
        out_shape=(jax.ShapeDtypeStruct((m, d), F32), jax.ShapeDtypeStruct(tail_shape, F32)),
        grid=(m // bm, nf),
        in_specs=[
            pl.BlockSpec((bm, d), lambda i, f: (i, 0)),
            pl.BlockSpec((d, bf), lambda i, f: (0, f)),
            pl.BlockSpec((d, bf), lambda i, f: (0, nf + f)),
            pl.BlockSpec((3, bf), lambda i, f: (0, f)),
            pl.BlockSpec((bf, d), lambda i, f: (f, 0)),
            prev_spec,
        ],
        out_specs=(pl.BlockSpec((bm, d), lambda i, f: (i, 0)), tail_spec),
        scratch_shapes=[pltpu.VMEM((nf, SUBLANES, bf), F32)],
        compiler_params=pltpu.CompilerParams(
            dimension_semantics=("arbitrary", "arbitrary"),
            vmem_limit_bytes=_vmem_limit(blk)),
        name="ffn",
    )(hn, w_up, w_up, conv_w, w_down, prev8)
    n_seq = m // seq_len
    if seq_len == SUBLANES:
        return out, tail.reshape(n_seq, SUBLANES, ff)
    return out, tail.reshape(n_seq, blocks_per_seq, SUBLANES, ff)[:, -1]


def _ple_kernel(h_ref, f_ref, pe_ref, wple_ref, wpg_ref, gp_ref, gf_ref, y_ref, *, final_norm):
    h2 = h_ref[...] + f_ref[...]
    hn = _rms(h2, gp_ref[...]).astype(BF16)
    gate = jax.nn.sigmoid(_dot(hn, wpg_ref[...]))
    emb = _dot(pe_ref[...].astype(BF16), wple_ref[...])
    h3 = h2 + emb * gate
    y_ref[...] = _rms(h3, gf_ref[...]) if final_norm else h3


def _ple_out(h, ffn_out, pe, w_ple, w_pg, g_ple, g_final, final_norm):
    m, d = h.shape
    pd = pe.shape[1]
    bm = _pick(m, 512, SUBLANES)
    blk = bm * (3 * d * 4 + pd * 4) + pd * d * 2 + d * d * 2
    return pl.pallas_call(
        functools.partial(_ple_kernel, final_norm=final_norm),
        out_shape=jax.ShapeDtypeStruct((m, d), F32),
        grid=(m // bm,),
        in_specs=[
            pl.BlockSpec((bm, d), lambda i: (i, 0)),
            pl.BlockSpec((bm, d), lambda i: (i, 0)),
            pl.BlockSpec((bm, pd), lambda i: (i, 0)),
            _resident(w_ple.shape), _resident(w_pg.shape),
            pl.BlockSpec((1, d), lambda i: (0, 0)),
            pl.BlockSpec((1, d), lambda i: (0, 0)),
        ],
        out_specs=pl.BlockSpec((bm, d), lambda i: (i, 0)),
        compiler_params=pltpu.CompilerParams(
            dimension_semantics=("parallel",),
            vmem_limit_bytes=_vmem_limit(blk + bm * d * 8)),
        name="ple_out",
    )(h, ffn_out, pe, w_ple, w_pg, g_ple, g_final)


def _prep_weights(g_mix, w_in, mu_shift, w0, w2, a0, a2, g2, k_k, k_a, r_k, ln_x_g, ln_x_b,
                  w_br_r, w_br_s, w_o, g_ffn, w_up, conv_w, w_down, g_ple, w_ple, w_pg, sb_bias):
    d = w_in.shape[0]
    rw = w0.shape[0]
    dl, il, gl = w2.shape[0], a2.shape[0], g2.shape[0]
    dlp, ilp, glp = (_ceil_to(n, LANES) for n in (dl, il, gl))
    sw = w_br_s.shape[0]
    rc = 3 * rw + dl + il + gl
    c_w, c_a, c_g = 3 * rw, 3 * rw + dl, 3 * rw + dl + il

    def regroup(x):
        return jnp.concatenate([
            x[..., :c_w], _pad_to(x[..., c_w:c_a], -1, dlp), _pad_to(x[..., c_a:c_g], -1, ilp),
            _pad_to(x[..., c_g:rc], -1, glp)], axis=-1)

    w_r = regroup(w_in[:, :rc]).astype(BF16)
    scale = HEAD_DIM ** -0.5
    w_q = (w_in[:, rc:rc + sw] * scale).astype(BF16)
    w_k = w_in[:, rc + sw:rc + 2 * sw].astype(BF16)
    w_v = w_in[:, rc + 2 * sw:rc + 3 * sw].astype(BF16)
    w_g = w_in[:, rc + 3 * sw:].astype(BF16)
    nh_s = sw // HEAD_DIM
    return dict(
        d=d, rw=rw, sw=sw, dl=dl, il=il, gl=gl, dlp=dlp, ilp=ilp, glp=glp, rc=rc,
        regroup=regroup,
        g_mix=g_mix.reshape(1, d), w_all=jnp.concatenate([w_r, w_q, w_k, w_v, w_g], axis=1),
        w_kv_t=jnp.concatenate([w_k, w_v], axis=1).T,
        proj_widths=(w_r.shape[1], sw, sw, sw, w_g.shape[1]),
        mu=regroup(mu_shift).reshape(1, -1),
        w0=w0.reshape(1, rw), a0=a0.reshape(1, rw),
        w2=_pad_to(w2, 0, dlp).astype(BF16), a2=_pad_to(a2, 0, ilp).astype(BF16),
        g2=_pad_to(g2, 0, glp).astype(BF16),
        kkw=k_k.reshape(1, rw), kaw=k_a.reshape(1, rw), rkw=r_k.reshape(1, rw),
        lng=ln_x_g.reshape(1, rw), lnb=ln_x_b.reshape(1, rw),
        w_br_r=w_br_r.astype(BF16), w_br_s=w_br_s.astype(BF16), w_o=w_o.astype(BF16),
        g_ffn=g_ffn.reshape(1, d), w_up=w_up.astype(BF16), conv_w=conv_w,
        w_down=w_down.astype(BF16), g_ple=g_ple.reshape(1, d),
        w_ple=w_ple.astype(BF16), w_pg=w_pg.astype(BF16),
        bias2=jnp.broadcast_to(sb_bias.reshape(nh_s // 2, 2, 1), (nh_s // 2, 2, LANES)).astype(F32),
        sb_bias=sb_bias,
    )


def _pages_transposed(cache):
    n_pool, page, nh, hd = cache.shape
    return jnp.transpose(cache, (0, 2, 3, 1)).reshape(n_pool, nh * hd, page)


def _layer(x3, pe3, shift_prev, wkv_prev, conv_prev, past, wp, g_final, final_norm):
    bsz, t, d = x3.shape
    m = bsz * t
    rw, sw = wp["rw"], wp["sw"]
    nh = rw // HEAD_DIM
    x = x3.reshape(m, d)

    kv_t = past is None
    p_r, q, k_s, v_s, gates = _rms_proj(
        x, wp["g_mix"], wp["w_all"], wp["w_kv_t"], wp["proj_widths"], (F32, BF16, F32, F32, BF16),
        (False, False, kv_t, kv_t, False), t)
    nh_s = sw // HEAD_DIM
    if kv_t:
        k_out = jnp.transpose(k_s.reshape(bsz, nh_s, HEAD_DIM, t), (0, 3, 1, 2))
        v_out = jnp.transpose(v_s.reshape(bsz, nh_s, HEAD_DIM, t), (0, 3, 1, 2))
    else:
        k_out = k_s.reshape(bsz, t, nh_s, HEAD_DIM)
        v_out = v_s.reshape(bsz, t, nh_s, HEAD_DIM)

    p3 = p_r.reshape(bsz, t, -1)
    shift = wp["regroup"](shift_prev).reshape(bsz, 1, -1)
    g_c, h_c, rp, yl, gate, bonus = _wkv_intra(
        p3, shift, wp["mu"], wp["w0"], wp["a0"], wp["w2"], wp["a2"], wp["g2"],
        wp["kkw"], wp["kaw"], wp["rkw"], rw=rw, dlp=wp["dlp"], ilp=wp["ilp"], glp=wp["glp"])
    y_r, wkv_new = _wkv_scan(g_c, h_c, rp, yl, gate, bonus, wkv_prev, wp["lng"], wp["lnb"])
    y_r = y_r[:, :t].reshape(m, rw)
    last = p3[:, -1]
    dlp, ilp = wp["dlp"], wp["ilp"]
    c0 = 3 * rw
    shift_new = jnp.concatenate([
        last[:, :c0], last[:, c0:c0 + wp["dl"]], last[:, c0 + dlp:c0 + dlp + wp["il"]],
        last[:, c0 + dlp + ilp:c0 + dlp + ilp + wp["gl"]]], axis=-1)

    if past is None:
        o_s = _attn_prompt(q.reshape(bsz, t, sw), k_s, v_s, wp["bias2"])
    else:
        cache_k, cache_v, page_table = past
        nh_s = sw // HEAD_DIM
        rows_n = nh_s * t
        q3 = q.reshape(bsz, t, sw)
        rr = jnp.arange(rows_n)[:, None] // t
        cc = jnp.arange(sw)[None, :] // HEAD_DIM
        q_bd = jnp.where(rr == cc, jnp.tile(q3, (1, nh_s, 1)), jnp.zeros((), BF16))
        k_new = _pad_to(k_s.reshape(bsz, t, sw).astype(BF16), 1, LANES)
        v_new = _pad_to(v_s.reshape(bsz, t, sw).astype(BF16), 1, LANES)
        bias_rows = jnp.broadcast_to(jnp.repeat(wp["sb_bias"].astype(F32), t)[:, None], (rows_n, LANES))
        o_s = _attn_paged(q_bd, k_new, v_new, bias_rows,
                          _pages_transposed(cache_k), _pages_transposed(cache_v), page_table, t)
    o_s = o_s.reshape(m, sw)

    h, hn = _merge(y_r, o_s, gates, x, wp["w_br_r"], wp["w_br_s"], wp["w_o"], wp["g_ffn"])
    ff = wp["w_down"].shape[0]
    nprev = conv_prev.shape[1]
    pad_rows = jnp.zeros((bsz, SUBLANES - nprev, ff), F32)
    if t == SUBLANES:
        prev8 = jnp.concatenate([conv_prev, pad_rows], axis=1).reshape(m, ff)
    else:
        prev8 = jnp.concatenate([pad_rows, conv_prev], axis=1)
    ffn_out, tail = _ffn(hn, wp["w_up"], wp["conv_w"], wp["w_down"], prev8, t)
    conv_new = tail[:, SUBLANES - nprev:]
    y = _ple_out(h, ffn_out, pe3.reshape(m, -1), wp["w_ple"], wp["w_pg"], wp["g_ple"],
                 g_final.reshape(1, d), final_norm)
    return (y.reshape(bsz, t, d), shift_new, wkv_new, conv_new,
            k_out, v_out)


def kernel(x_prompt, x_sample, state_shift, state_wkv, state_conv, cache_k, cache_v, page_table, p_prompt, p_sample, g_mix, w_in, sb_bias, mu_shift, w0, w2, a0, a2, g2, k_k, k_a, r_k, ln_x_g, ln_x_b, w_br_r, w_br_s, w_o, g_ffn, w_up, conv_w, w_down, g_ple, w_ple, w_pg, g_final):
    depth = w_in.shape[0]
    bsz = x_prompt.shape[0]
    rw = w0.shape[1]
    nh = rw // HEAD_DIM
    ff = w_down.shape[1]
    rc = state_shift.shape[-1]
    nprev = state_conv.shape[2]
    h_p, h_s = x_prompt, x_sample
    outs_p = [[] for _ in range(5)]
    outs_s = [[] for _ in range(5)]
    for i in range(depth):
        wp = _prep_weights(g_mix[i], w_in[i], mu_shift[i], w0[i], w2[i], a0[i], a2[i], g2[i], k_k[i],
                           k_a[i], r_k[i], ln_x_g[i], ln_x_b[i], w_br_r[i], w_br_s[i], w_o[i],
                           g_ffn[i], w_up[i], conv_w[i], w_down[i], g_ple[i], w_ple[i], w_pg[i],
                           sb_bias[i])
        last = i == depth - 1
        res_p = _layer(h_p, p_prompt[i], jnp.zeros((bsz, rc), F32),
                       jnp.zeros((bsz, nh, HEAD_DIM, HEAD_DIM), F32),
                       jnp.zeros((bsz, nprev, ff), F32), None, wp, g_final, last)
        res_s = _layer(h_s, p_sample[i], state_shift[i], state_wkv[i], state_conv[i],
                       (cache_k[i], cache_v[i], page_table), wp, g_final, last)
        h_p, h_s = res_p[0], res_s[0]
        for dst, res in ((outs_p, res_p), (outs_s, res_s)):
            for lst, val in zip(dst, res[1:]):
                lst.append(val)
    return (h_p, h_s, *(jnp.stack(o) for o in outs_p), *(jnp.stack(o) for o in outs_s))
```

```python
import functools
import math

import jax
import jax.numpy as jnp
from jax import lax
from jax.experimental import pallas as pl
from jax.experimental.pallas import tpu as pltpu

F32 = jnp.float32
BF16 = jnp.bfloat16

LANES = 128
SUBLANES = 8
V7X_SCOPED_VMEM_BYTES = 60000 * 1024

HEAD_DIM = 64
CHUNK = 128
INTRA_PAIRS_PER_STEP = 4
SCAN_PAIRS_PER_STEP = 4
DECAY_SCALE = 0.606531
NORM_EPS = 1e-6
GN_EPS = HEAD_DIM * 1e-5
KK_EPS = 1e-24


def _vmem_limit(block_bytes):
    return int(min(V7X_SCOPED_VMEM_BYTES, 2 * block_bytes + (16 << 20)))


def _pick(n, pref, align):
    if n <= pref:
        return n
    best = None
    for d in range(align, pref + 1, align):
        if n % d == 0:
            best = d
    assert best is not None, (n, pref, align)
    return best


def _pad_to(x, axis, size):
    pad = size - x.shape[axis]
    if pad == 0:
        return x
    widths = [(0, 0)] * x.ndim
    widths[axis] = (0, pad)
    return jnp.pad(x, widths)


def _ceil_to(n, m):
    return -(-n // m) * m


def _dot(a, b):
    return jnp.dot(a, b, preferred_element_type=F32)


def _dot_nt(a, b):
    return lax.dot_general(a, b, (((1,), (1,)), ((), ())), preferred_element_type=F32)


def _dot_tn(a, b):
    return lax.dot_general(a, b, (((0,), (0,)), ((), ())), preferred_element_type=F32)


def _split2(x):
    hi = x.astype(BF16)
    lo = (x - hi.astype(F32)).astype(BF16)
    return hi, lo


def _split3(x):
    hi = x.astype(BF16)
    r1 = x - hi.astype(F32)
    mid = r1.astype(BF16)
    lo = (r1 - mid.astype(F32)).astype(BF16)
    return hi, mid, lo


def _iota2(shape, dim):
    return lax.broadcasted_iota(jnp.int32, shape, dim)


def _div_pow2(x, n):
    assert n & (n - 1) == 0, n
    return x >> (n.bit_length() - 1)


def _mod_pow2(x, n):
    assert n & (n - 1) == 0, n
    return x & (n - 1)


def _head_sum(x, e2):
    hi, lo = _split2(x)
    return _dot(jnp.concatenate([hi, lo], axis=1), e2)


def _same_head_matrix():
    r = _mod_pow2(_iota2((2 * LANES, LANES), 0), LANES)
    c = _iota2((2 * LANES, LANES), 1)
    return jnp.where(_div_pow2(r, HEAD_DIM) == _div_pow2(c, HEAD_DIM), 1.0, 0.0).astype(BF16)


def _rms_proj_kernel(x_ref, g_ref, w_ref, wt_ref, *refs, bounds, transposed):
    o_refs, xn_ref = refs[:-1], refs[-1]
    j = pl.program_id(1)

    @pl.when(j == 0)
    def _():
        x = x_ref[...]
        ms = jnp.mean(x * x, axis=-1, keepdims=True)
        xn_ref[...] = (x * lax.rsqrt(ms + NORM_EPS) * g_ref[...]).astype(BF16)

    for o_ref, (lo, hi), tr in zip(o_refs, bounds, transposed):
        @pl.when((j >= lo) & (j < hi))
        def _(o_ref=o_ref, tr=tr):
            if tr:
                o_ref[0] = _dot_nt(wt_ref[...], xn_ref[...]).astype(o_ref.dtype)
            else:
                o_ref[...] = _dot(xn_ref[...], w_ref[...]).astype(o_ref.dtype)


def _rms_proj(x, g, w, w_t, widths, dtypes, transposed, seq_len):
    m, d = x.shape
    assert w.shape[1] == sum(widths)
    bm = _pick(seq_len, 1024, SUBLANES) if any(transposed) else _pick(m, 1024, SUBLANES)
    bps = seq_len // bm
    bn = _pick(math.gcd(*widths), 512, LANES)
    bounds, lo = [], 0
    for wd in widths:
        bounds.append((lo, lo + wd // bn))
        lo += wd // bn
    t_ranges = [b for b, tr in zip(bounds, transposed) if tr]
    t_lo = t_ranges[0][0] if t_ranges else 0
    t_hi = t_ranges[-1][1] if t_ranges else 0
    assert sum(hi - a for a, hi in t_ranges) == t_hi - t_lo
    if not t_ranges:
        w_t = w_t[:bn]
    assert w_t.shape == (max(t_hi - t_lo, 1) * bn, d)

    def w_index(i, j):
        return (0, jnp.where((j >= t_lo) & (j < t_hi), max(t_lo - 1, 0), j))

    def out_spec(lo, hi, tr):
        if tr:
            return pl.BlockSpec((1, bn, bm),
                                lambda i, j: (i // bps, jnp.clip(j - lo, 0, hi - lo - 1), i % bps))
        return pl.BlockSpec((bm, bn), lambda i, j: (i, jnp.clip(j - lo, 0, hi - lo - 1)))

    out_shape = tuple(
        jax.ShapeDtypeStruct((m // seq_len, wd, seq_len) if tr else (m, wd), t)
        for wd, t, tr in zip(widths, dtypes, transposed))
    blk = (bm * d * 4 + 2 * d * bn * 2 + bm * d * 2
           + sum(bm * bn * jnp.dtype(t).itemsize for t in dtypes))
    return pl.pallas_call(
        functools.partial(_rms_proj_kernel, bounds=tuple(bounds), transposed=tuple(transposed)),
        out_shape=out_shape,
        grid=(m // bm, lo),
        in_specs=[
            pl.BlockSpec((bm, d), lambda i, j: (i, 0)),
            pl.BlockSpec((1, d), lambda i, j: (0, 0)),
            pl.BlockSpec((d, bn), w_index),
            pl.BlockSpec((bn, d), lambda i, j: (jnp.clip(j - t_lo, 0, max(t_hi - t_lo, 1) - 1), 0)),
        ],
        out_specs=tuple(out_spec(a, b, tr) for (a, b), tr in zip(bounds, transposed)),
        scratch_shapes=[pltpu.VMEM((bm, d), BF16)],
        compiler_params=pltpu.CompilerParams(
            dimension_semantics=("parallel", "arbitrary"),
            vmem_limit_bytes=_vmem_limit(blk)),
        name="rms_proj",
    )(x, g, w, w_t)


def _wkv_intra_kernel(r_ref, k_ref, v_ref, xw_ref, xa_ref, xg_ref,
                      rp_ref, kp_ref, vp_ref, xwp_ref, xap_ref, xgp_ref,
                      rs_ref, ks_ref, vs_ref, xws_ref, xas_ref, xgs_ref,
                      mur_ref, muk_ref, muv_ref, muw_ref, mua_ref, mug_ref,
                      w0_ref, a0_ref, w2_ref, a2_ref, g2_ref,
                      kkw_ref, kaw_ref, rkw_ref,
                      g_out, h_out, rp_out, yl_out, gate_out, bonus_out,
                      *, t_real, n_pairs):
    c_idx = pl.program_id(1)
    C = CHUNK
    first = c_idx == 0

    def mixed(x_ref, p_ref, s_ref, mu_ref):
        x = x_ref[0]
        prev_last = jnp.where(first, s_ref[0], p_ref[0][SUBLANES - 1:SUBLANES])
        rolled = pltpu.roll(x, 1, 0)
        prev = jnp.where(_iota2(x.shape, 0) == 0, prev_last, rolled)
        y = x + (prev - x) * mu_ref[...]
        if t_real < C:
            y = jnp.concatenate([y, jnp.zeros((C - t_real, y.shape[1]), F32)], axis=0)
        return y

    r_all = mixed(r_ref, rp_ref, rs_ref, mur_ref)
    k_all = mixed(k_ref, kp_ref, ks_ref, muk_ref)
    v_all = mixed(v_ref, vp_ref, vs_ref, muv_ref)
    xw = jnp.tanh(mixed(xw_ref, xwp_ref, xws_ref, muw_ref)).astype(BF16)
    xa = mixed(xa_ref, xap_ref, xas_ref, mua_ref).astype(BF16)
    xg = jax.nn.sigmoid(mixed(xg_ref, xgp_ref, xgs_ref, mug_ref)).astype(BF16)

    lw_all = -DECAY_SCALE * jax.nn.sigmoid(w0_ref[...] + _dot(xw, w2_ref[...]))
    if t_real < C:
        lw_all = jnp.where(_iota2(lw_all.shape, 0) < t_real, lw_all, 0.0)
    iclr_all = jax.nn.sigmoid(a0_ref[...] + _dot(xa, a2_ref[...]))
    gate_all = _dot(xg, g2_ref[...])
    gate_out[0] = gate_all.astype(gate_out.dtype)

    e2 = _same_head_matrix()
    row = _iota2((C, C), 0)
    col = _iota2((C, C), 1)
    strict_lower = row > col
    lower = row >= col
    l_incl = jnp.where(lower, 1.0, 0.0).astype(BF16)
    l3 = jnp.concatenate([l_incl, l_incl, l_incl], axis=1)
    lane = _iota2((C, LANES), 1)
    head_masks = (lane < HEAD_DIM, lane >= HEAD_DIM)
    rr = _iota2((LANES, LANES), 0)
    cc = _iota2((LANES, LANES), 1)
    block_mask = _div_pow2(rr, HEAD_DIM) == _div_pow2(cc, HEAD_DIM)
    eye_mask = rr == cc

    pairs = range(n_pairs)
    sls = [slice(p * LANES, (p + 1) * LANES) for p in pairs]
    r = [r_all[:, sl] for sl in sls]
    k = [k_all[:, sl] for sl in sls]
    v = [v_all[:, sl] for sl in sls]
    lw = [lw_all[:, sl] for sl in sls]
    iclr = [iclr_all[:, sl] for sl in sls]

    kkr = [k[p] * kkw_ref[:, sls[p]] for p in pairs]
    k_mod = [k[p] * (1.0 + (iclr[p] - 1.0) * kaw_ref[:, sls[p]]) for p in pairs]
    kk_ss = [_head_sum(kkr[p] * kkr[p], e2) for p in pairs]
    rk_sum = [_head_sum(r[p] * k_mod[p] * rkw_ref[:, sls[p]], e2) for p in pairs]
    cum = [_dot(l3, jnp.concatenate(_split3(lw[p]), axis=0)) for p in pairs]

    vb, em, ecl, kbar, bbar, rhs_scores, lhs_scores, am_b, rm_f = [], [], [], [], [], [], [], [], []
    for p in pairs:
        bonus_out[0, :, sls[p]] = rk_sum[p] * v[p]
        kk = kkr[p] * lax.rsqrt(jnp.maximum(kk_ss[p], KK_EPS))
        b = kk * iclr[p]
        m_row = cum[p][C // 2 - 1:C // 2]
        cum_last = cum[p][C - 1:C]
        g = cum[p] - m_row
        eng = jnp.exp(-g)
        ebar = jnp.exp(cum_last - cum[p])
        at = -kk * jnp.exp(g - lw[p])
        rt = r[p] * jnp.exp(g)
        kbar.append((k_mod[p] * ebar).astype(BF16))
        bbar.append((b * ebar).astype(BF16))
        em.append(jnp.exp(m_row))
        ecl.append(jnp.exp(cum_last))
        vb.append(v[p].astype(BF16))
        rhs_scores.append(jnp.concatenate([(k_mod[p] * eng).astype(BF16), (b * eng).astype(BF16)], axis=0))
        stack = []
        for mh in head_masks:
            am_b.append(jnp.where(mh, at, 0.0).astype(BF16))
            rm_f.append(jnp.where(mh, rt, 0.0))
            stack += [am_b[-1], rm_f[-1].astype(BF16)]
        lhs_scores.append(jnp.concatenate(stack, axis=0))

    sc = [_dot_nt(lhs_scores[p], rhs_scores[p]) for p in pairs]
    heads = range(2 * n_pairs)
    m_ak, m_ab, p_rk_b, p_rb_b = [], [], [], []
    for i in heads:
        s = sc[i // 2][(i % 2) * 2 * C:(i % 2 + 1) * 2 * C]
        m_ak.append(jnp.where(strict_lower, s[:C, :C], 0.0).astype(BF16))
        m_ab.append(jnp.where(strict_lower, s[:C, C:], 0.0))
        p_rk_b.append(jnp.where(lower, s[C:, :C], 0.0).astype(BF16))
        p_rb_b.append(jnp.where(lower, s[C:, C:], 0.0).astype(BF16))

    eye = jnp.where(row == col, 1.0, 0.0).astype(F32)
    t = [eye + m_ab[i] for i in heads]
    qb = [m_ab[i].astype(BF16) for i in heads]
    q = [_dot(qb[i], qb[i]) for i in heads]
    w1 = [_dot(m_ak[i], vb[i // 2]).astype(BF16) for i in heads]
    for _ in range(int(math.log2(C)) - 2):
        qb = [q[i].astype(BF16) for i in heads]
        st = [_dot(jnp.concatenate([t[i].astype(BF16), qb[i]], axis=0), qb[i]) for i in heads]
        t = [t[i] + st[i][:C] for i in heads]
        q = [st[i][C:] for i in heads]
    corr = [_dot(t[i].astype(BF16), q[i].astype(BF16)) for i in heads]
    t_inv = [(t[i] + corr[i]).astype(BF16) for i in heads]
    tu = [_dot(t_inv[i], jnp.concatenate([w1[i], am_b[i]], axis=1)) for i in heads]
    ul = [tu[i][:, :LANES] for i in heads]
    ap = [tu[i][:, LANES:] for i in heads]
    yl = [_dot(jnp.concatenate([p_rk_b[i], p_rb_b[i]], axis=1),
               jnp.concatenate([vb[i // 2], ul[i].astype(BF16)], axis=0)) for i in heads]
    rp_add = [_dot(p_rb_b[i], ap[i].astype(BF16)) for i in heads]

    m1 = head_masks[1]
    ul_pair = [jnp.where(m1, ul[2 * p + 1], ul[2 * p]).astype(BF16) for p in pairs]
    ap_pair = [((ap[2 * p] + ap[2 * p + 1]) * em[p]).astype(BF16) for p in pairs]
    ab = [_dot_tn(ap_pair[p], bbar[p]) for p in pairs]
    hh = [_dot_tn(jnp.concatenate([vb[p], ul_pair[p]], axis=0),
                  jnp.concatenate([kbar[p], bbar[p]], axis=0)) for p in pairs]
    for p in pairs:
        sl = sls[p]
        g_out[0, 0, :, sl] = jnp.where(block_mask, ab[p], 0.0) + jnp.where(eye_mask, ecl[p], 0.0)
        h_out[0, 0, :, sl] = jnp.where(block_mask, hh[p], 0.0)
        rp_pair = rm_f[2 * p] + rp_add[2 * p] + rm_f[2 * p + 1] + rp_add[2 * p + 1]
        rp_out[0, :, sl] = (rp_pair * em[p]).astype(rp_out.dtype)
        yl_out[0, :, sl] = jnp.where(m1, yl[2 * p + 1], yl[2 * p])


def _wkv_intra(p3, shift, mu, w0, a0, w2, a2, g2, kkw, kaw, rkw, *, rw, dlp, ilp, glp):
    bsz, t, npc = p3.shape
    C = CHUNK
    t_real = min(t, C)
    assert t % t_real == 0 and t_real % SUBLANES == 0
    nch = t // t_real
    tp = nch * C
    n_pairs = _pick(rw // LANES, INTRA_PAIRS_PER_STEP, 1)
    lw_ = n_pairs * LANES
    ng = rw // lw_
    o_w, o_a, o_g = 3 * rw, 3 * rw + dlp, 3 * rw + dlp + ilp
    assert o_w % dlp == 0 and o_a % ilp == 0 and o_g % glp == 0
    rows_prev = t_real // SUBLANES

    def cur(width, off):
        return pl.BlockSpec((1, t_real, width), lambda b, c, g, o=off // width: (b, c, o))

    def cur_g(width, off):
        return pl.BlockSpec((1, t_real, width), lambda b, c, g, o=off // width: (b, c, o + g))

    def prev(width, off):
        return pl.BlockSpec((1, SUBLANES, width),
                            lambda b, c, g, o=off // width: (b, jnp.maximum(c * rows_prev - 1, 0), o))

    def prev_g(width, off):
        return pl.BlockSpec((1, SUBLANES, width),
                            lambda b, c, g, o=off // width: (b, jnp.maximum(c * rows_prev - 1, 0), o + g))

    def sh(width, off):
        return pl.BlockSpec((1, 1, width), lambda b, c, g, o=off // width: (b, 0, o))

    def sh_g(width, off):
        return pl.BlockSpec((1, 1, width), lambda b, c, g, o=off // width: (b, 0, o + g))

    def vec(width, off):
        return pl.BlockSpec((1, width), lambda b, c, g, o=off // width: (0, o))

    def vec_g(width, off=0):
        return pl.BlockSpec((1, width), lambda b, c, g, o=off // width: (0, o + g))

    def mat_g(rows):
        return pl.BlockSpec((rows, lw_), lambda b, c, g: (0, g))

    in_specs = (
        [cur_g(lw_, 0), cur_g(lw_, rw), cur_g(lw_, 2 * rw), cur(dlp, o_w), cur(ilp, o_a), cur(glp, o_g)]
        + [prev_g(lw_, 0), prev_g(lw_, rw), prev_g(lw_, 2 * rw), prev(dlp, o_w), prev(ilp, o_a), prev(glp, o_g)]
        + [sh_g(lw_, 0), sh_g(lw_, rw), sh_g(lw_, 2 * rw), sh(dlp, o_w), sh(ilp, o_a), sh(glp, o_g)]
        + [vec_g(lw_, 0), vec_g(lw_, rw), vec_g(lw_, 2 * rw), vec(dlp, o_w), vec(ilp, o_a), vec(glp, o_g)]
        + [vec_g(lw_), vec_g(lw_), mat_g(dlp), mat_g(ilp), mat_g(glp)]
        + [vec_g(lw_), vec_g(lw_), vec_g(lw_)]
    )
    args = ([p3] * 6 + [p3] * 6 + [shift] * 6 + [mu] * 6 + [w0, a0, w2, a2, g2, kkw, kaw, rkw])
    out_shape = (
        jax.ShapeDtypeStruct((bsz, nch, LANES, rw), F32),
        jax.ShapeDtypeStruct((bsz, nch, LANES, rw), F32),
        jax.ShapeDtypeStruct((bsz, tp, rw), BF16),
        jax.ShapeDtypeStruct((bsz, tp, rw), F32),
        jax.ShapeDtypeStruct((bsz, tp, rw), BF16),
        jax.ShapeDtypeStruct((bsz, tp, rw), F32),
    )
    gh_spec = pl.BlockSpec((1, 1, LANES, lw_), lambda b, c, g: (b, c, 0, g))
    tok_spec = pl.BlockSpec((1, C, lw_), lambda b, c, g: (b, c, g))
    blk = (t_real * (3 * lw_ + dlp + ilp + glp) * 4 + (dlp + ilp + glp) * lw_ * 2
           + 2 * LANES * lw_ * 4 + C * lw_ * 12)
    return pl.pallas_call(
        functools.partial(_wkv_intra_kernel, t_real=t_real, n_pairs=n_pairs),
        out_shape=out_shape,
        grid=(bsz, nch, ng),
        in_specs=in_specs,
        out_specs=(gh_spec, gh_spec, tok_spec, tok_spec, tok_spec, tok_spec),
        compiler_params=pltpu.CompilerParams(
            dimension_semantics=("parallel", "parallel", "parallel"),
            vmem_limit_bytes=_vmem_limit(blk)),
        name="wkv_intra",
    )(*args)


def _wkv_scan_kernel(g_ref, h_ref, rp_ref, yl_ref, gate_ref, bonus_ref, s0_ref, lng_ref, lnb_ref,
                     y_ref, s_out_ref, st_ref, *, n_chunks, n_pairs):
    C = CHUNK
    N = HEAD_DIM
    e2 = _same_head_matrix()
    inv_n = 1.0 / HEAD_DIM
    lanes = [slice(p * LANES, (p + 1) * LANES) for p in range(n_pairs)]

    zero = jnp.zeros((N, N), F32)
    for p, sl in enumerate(lanes):
        st_ref[:, sl] = jnp.concatenate(
            [jnp.concatenate([s0_ref[0, 2 * p], zero], axis=1),
             jnp.concatenate([zero, s0_ref[0, 2 * p + 1]], axis=1)], axis=0)

    def body(c, carry):
        rows = pl.ds(pl.multiple_of(c * C, C), C)
        sb = [st_ref[:, sl].astype(BF16) for sl in lanes]
        y = [_dot_nt(rp_ref[0, rows, sl], s) for sl, s in zip(lanes, sb)]
        s_new = [_dot(s, g_ref[0, c, :, sl].astype(BF16)) for sl, s in zip(lanes, sb)]
        for sl, s in zip(lanes, s_new):
            st_ref[:, sl] = s + h_ref[0, c, :, sl]
        y = [yy + yl_ref[0, rows, sl] for sl, yy in zip(lanes, y)]
        mu = [_head_sum(yy, e2) * inv_n for yy in y]
        yc = [yy - m for yy, m in zip(y, mu)]
        var = [_head_sum(x * x, e2) * inv_n for x in yc]
        for sl, x, vv in zip(lanes, yc, var):
            yn = x * lax.rsqrt(vv + GN_EPS) * lng_ref[:, sl] + lnb_ref[:, sl]
            out = (yn + bonus_ref[0, rows, sl]) * gate_ref[0, rows, sl].astype(F32)
            y_ref[0, rows, sl] = out.astype(y_ref.dtype)
        return carry

    lax.fori_loop(0, n_chunks, body, 0)
    for p, sl in enumerate(lanes):
        s = st_ref[:, sl]
        s_out_ref[0, 2 * p] = s[:N, :N]
        s_out_ref[0, 2 * p + 1] = s[N:, N:]


def _wkv_scan(g, h, rp, yl, gate, bonus, s0, lng, lnb):
    bsz, nch, _, rw = g.shape
    tp = rp.shape[1]
    n_pairs = _pick(rw // LANES, SCAN_PAIRS_PER_STEP, 1)
    lw_ = n_pairs * LANES
    gh_spec = pl.BlockSpec((1, nch, LANES, lw_), lambda b, q: (b, 0, 0, q))
    tok_spec = pl.BlockSpec((1, tp, lw_), lambda b, q: (b, 0, q))
    st_spec = pl.BlockSpec((1, 2 * n_pairs, HEAD_DIM, HEAD_DIM), lambda b, q: (b, q, 0, 0))
    vec_spec = pl.BlockSpec((1, lw_), lambda b, q: (0, q))
    blk = 2 * nch * LANES * lw_ * 4 + tp * lw_ * (2 + 4 + 2 + 4 + 2) + 4 * LANES * lw_ * 4
    return pl.pallas_call(
        functools.partial(_wkv_scan_kernel, n_chunks=nch, n_pairs=n_pairs),
        out_shape=(jax.ShapeDtypeStruct((bsz, tp, rw), BF16),
                   jax.ShapeDtypeStruct(s0.shape, F32)),
        grid=(bsz, rw // lw_),
        in_specs=[gh_spec, gh_spec, tok_spec, tok_spec, tok_spec, tok_spec, st_spec, vec_spec, vec_spec],
        out_specs=(tok_spec, st_spec),
        scratch_shapes=[pltpu.VMEM((LANES, lw_), F32)],
        compiler_params=pltpu.CompilerParams(
            dimension_semantics=("parallel", "parallel"),
            vmem_limit_bytes=_vmem_limit(blk)),
        name="wkv_scan",
    )(g, h, rp, yl, gate, bonus, s0, lng, lnb)


def _cumsum_rhs():
    r = _iota2((LANES, 2 * LANES), 0)
    c = _iota2((LANES, 2 * LANES), 1)
    return jnp.where((c >= LANES) | (r > c), -1.0, 0.0).astype(BF16)


def _keep_sums(sp, w2):
    return _dot(sp.astype(BF16), w2)


SOFTPLUS_LINEAR_ABOVE = 30.0


def _softplus(z, mask=None):
    sp = jnp.log(1.0 + jnp.exp(jnp.minimum(z, SOFTPLUS_LINEAR_ABOVE)))
    sp = jnp.where(z > SOFTPLUS_LINEAR_ABOVE, z, sp)
    return sp if mask is None else jnp.where(mask, sp, 0.0)


def _sb_tile(z, v_tile, run, acc, w2, mask):
    sp = _softplus(z, mask)
    cs2 = _keep_sums(sp, w2)
    att = jnp.exp((z - sp) + cs2[:, :LANES] + run)
    if mask is not None:
        att = jnp.where(mask, att, 0.0)
    acc = acc + _dot(att.astype(BF16), v_tile)
    return run + cs2[:, LANES:], acc


def _attn_prompt_kernel(q_ref, k_ref, v_ref, bias_ref, o_ref, kb_ref, vb_ref, run_ref, acc_ref):
    qi = pl.program_id(2)
    qb = q_ref.shape[1]
    kb = 2 * LANES
    n_diag = qb // kb
    assert qb == n_diag * kb and n_diag in (1, 2)

    @pl.when(qi == 0)
    def _():
        for c in range(kb_ref.shape[0]):
            kb_ref[c] = k_ref[0, :, c * LANES:(c + 1) * LANES].astype(BF16)
            vb_ref[c] = v_ref[0, :, c * LANES:(c + 1) * LANES].astype(BF16)

    q = q_ref[0]
    m0 = _iota2((qb, LANES), 1) < HEAD_DIM
    zero = jnp.zeros((), BF16)
    qs = jnp.concatenate([jnp.where(m0, q, zero), jnp.where(m0, zero, q)], axis=0)
    w2 = _cumsum_rhs()
    run_ref[...] = jnp.zeros(run_ref.shape, F32)
    acc_ref[...] = jnp.zeros(acc_ref.shape, F32)

    def step(j, n_blocks, mask):
        nk = 2 * n_blocks
        t0 = 2 * j
        z = _dot(qs, jnp.concatenate([kb_ref[t0 + c] for c in range(nk)], axis=1))
        b0 = jnp.concatenate([bias_ref[0, 0:1, :]] * nk, axis=1)
        b1 = jnp.concatenate([bias_ref[0, 1:2, :]] * nk, axis=1)
        z = jnp.concatenate([z[:qb] + b0, z[qb:] + b1], axis=0)
        sp = _softplus(z, mask)
        tiles = [slice(c * LANES, (c + 1) * LANES) for c in range(nk)]
        cs = [_keep_sums(sp[:, c], w2) for c in tiles]
        zs = z - sp
        run = run_ref[...]
        e = [None] * nk
        for c in reversed(range(nk)):
            e[c] = zs[:, tiles[c]] + cs[c][:, :LANES] + run
            run = run + cs[c][:, LANES:]
        att = jnp.exp(jnp.concatenate(e, axis=1))
        if mask is not None:
            att = jnp.where(mask, att, 0.0)
        vt = jnp.concatenate([vb_ref[t0 + c] for c in range(nk)], axis=1)
        acc_ref[...] += _dot_nt(att.astype(BF16), vt)
        run_ref[...] = run

    qpos = _mod_pow2(_iota2((2 * qb, qb), 0), qb)
    step(qi * n_diag, n_diag, _iota2((2 * qb, qb), 1) < qpos)

    n_off = qi * n_diag

    def body(jj, carry):
        step(n_off - 2 - 2 * jj, 2, None)
        return carry

    lax.fori_loop(0, n_off // 2, body, 0)

    if n_diag % 2 == 1:
        @pl.when(n_off % 2 == 1)
        def _():
            step(0, 1, None)

    o_ref[0] = jnp.where(m0, acc_ref[:qb], acc_ref[qb:]).astype(o_ref.dtype)


def _attn_prompt(q, k, v, bias2):
    bsz, t, w = q.shape
    qb = 4 * LANES if t % (4 * LANES) == 0 else 2 * LANES
    assert t % qb == 0
    blk = qb * LANES * 2 * 2 + 2 * t * LANES * 4
    return pl.pallas_call(
        _attn_prompt_kernel,
        out_shape=jax.ShapeDtypeStruct((bsz, t, w), BF16),
        grid=(bsz, w // LANES, t // qb),
        in_specs=[
            pl.BlockSpec((1, qb, LANES), lambda b, p, i: (b, i, p)),
            pl.BlockSpec((1, LANES, t), lambda b, p, i: (b, p, 0)),
            pl.BlockSpec((1, LANES, t), lambda b, p, i: (b, p, 0)),
            pl.BlockSpec((1, 2, LANES), lambda b, p, i: (p, 0, 0)),
        ],
        out_specs=pl.BlockSpec((1, qb, LANES), lambda b, p, i: (b, i, p)),
        scratch_shapes=[pltpu.VMEM((t // LANES, LANES, LANES), BF16),
                        pltpu.VMEM((t // LANES, LANES, LANES), BF16),
                        pltpu.VMEM((2 * qb, LANES), F32), pltpu.VMEM((2 * qb, LANES), F32)],
        compiler_params=pltpu.CompilerParams(
            dimension_semantics=("parallel", "parallel", "arbitrary"),
            vmem_limit_bytes=_vmem_limit(blk + 2 * t * LANES * 2)),
        name="attn_prompt",
    )(q, k, v, bias2)


PAGES_PER_STEP = 16


def _attn_paged_kernel(pt_ref, *refs, n_groups, pps):
    del pt_ref
    k_refs = refs[:pps]
    v_refs = refs[pps:2 * pps]
    q_ref, kn_ref, vn_ref, bias_ref, o_ref, run_ref, acc_ref = refs[2 * pps:]
    g = pl.program_id(1)
    rows_n = q_ref.shape[1]
    t_new = o_ref.shape[1]
    w2 = _cumsum_rhs()
    q = q_ref[0]
    bias = bias_ref[...]

    @pl.when(g == 0)
    def _():
        lane = _iota2((rows_n, LANES), 1)
        row = _iota2((rows_n, LANES), 0)
        mask = lane < _mod_pow2(row, t_new)
        z = _dot_nt(q, kn_ref[0]) + bias
        run, acc = _sb_tile(z, vn_ref[0], jnp.zeros((rows_n, LANES), F32),
                            jnp.zeros(acc_ref.shape, F32), w2, mask)
        run_ref[...] = run
        acc_ref[...] = acc

    pages = range(pps)
    kt = [k_refs[p][0].astype(BF16) for p in pages]
    z2 = [_dot(q, jnp.concatenate(kt[i:i + 2], axis=1)) for i in range(0, pps, 2)]
    z = [z2[p // 2][:, (p % 2) * LANES:(p % 2 + 1) * LANES] + bias for p in pages]
    sp = [_softplus(zz) for zz in z]
    cs2 = [_keep_sums(x, w2) for x in sp]
    run = run_ref[...]
    att = []
    for p in pages:
        att.append(jnp.exp((z[p] - sp[p]) + cs2[p][:, :LANES] + run).astype(BF16))
        run = run + cs2[p][:, LANES:]
    run_ref[...] = run
    vt = jnp.concatenate([v_refs[p][0].astype(BF16) for p in pages], axis=1)
    acc_ref[...] += _dot_nt(jnp.concatenate(att, axis=1), vt)

    @pl.when(g == n_groups - 1)
    def _():
        acc = acc_ref[...]
        rr = _iota2(acc.shape, 0)
        cc = _iota2(acc.shape, 1)
        picked = jnp.where(_div_pow2(rr, t_new) == _div_pow2(cc, HEAD_DIM), acc, 0.0)
        out = picked[0:t_new]
        for h in range(1, rows_n // t_new):
            out = out + picked[h * t_new:(h + 1) * t_new]
        o_ref[0] = out.astype(o_ref.dtype)


def _attn_paged(q_bd, k_new, v_new, bias_rows, cache_k, cache_v, page_table, t_new):
    bsz, rows_n, w = q_bd.shape
    n_pages = page_table.shape[1]
    page = cache_k.shape[2]
    pps = _pick(n_pages, PAGES_PER_STEP, 2)
    assert page == LANES and cache_k.shape[1] == w
    n_groups = n_pages // pps

    def page_spec(p):
        return pl.BlockSpec((1, w, page),
                            lambda b, g, pt, p=p % pps: (pt[b, n_pages - 1 - (g * pps + p)], 0, 0))

    in_specs = ([page_spec(p) for p in range(2 * pps)] + [
        pl.BlockSpec((1, rows_n, w), lambda b, g, pt: (b, 0, 0)),
        pl.BlockSpec((1, LANES, w), lambda b, g, pt: (b, 0, 0)),
        pl.BlockSpec((1, LANES, w), lambda b, g, pt: (b, 0, 0)),
        pl.BlockSpec((rows_n, LANES), lambda b, g, pt: (0, 0)),
    ])
    blk = 2 * pps * page * w * 4 + rows_n * w * 2 + 2 * LANES * w * 2 + rows_n * w * 4
    return pl.pallas_call(
        functools.partial(_attn_paged_kernel, n_groups=n_groups, pps=pps),
        out_shape=jax.ShapeDtypeStruct((bsz, t_new, w), BF16),
        grid_spec=pltpu.PrefetchScalarGridSpec(
            num_scalar_prefetch=1,
            grid=(bsz, n_groups),
            in_specs=in_specs,
            out_specs=pl.BlockSpec((1, t_new, w), lambda b, g, pt: (b, 0, 0)),
            scratch_shapes=[pltpu.VMEM((rows_n, LANES), F32), pltpu.VMEM((rows_n, w), F32)],
        ),
        compiler_params=pltpu.CompilerParams(
            dimension_semantics=("parallel", "arbitrary"),
            vmem_limit_bytes=_vmem_limit(blk)),
        name="attn_paged",
    )(page_table, *([cache_k] * pps), *([cache_v] * pps), q_bd, k_new, v_new, bias_rows)


def _rms(x, g):
    ms = jnp.mean(x * x, axis=-1, keepdims=True)
    return x * lax.rsqrt(ms + NORM_EPS) * g


def _merge_kernel(yr_ref, os_ref, gr_ref, gs_ref, x_ref, wr_ref, ws_ref, wo_ref, g_ref, h_ref, hn_ref):
    a = _dot(yr_ref[...], wr_ref[...])
    b = _dot(os_ref[...], ws_ref[...])
    mixed = (jax.nn.sigmoid(gr_ref[...].astype(F32)) * a
             + jax.nn.sigmoid(gs_ref[...].astype(F32)) * b)
    h = x_ref[...] + _dot(mixed.astype(BF16), wo_ref[...])
    h_ref[...] = h
    hn_ref[...] = _rms(h, g_ref[...]).astype(hn_ref.dtype)


def _resident(shape):
    return pl.BlockSpec(shape, lambda *_: (0,) * len(shape), pipeline_mode=pl.Buffered(1))


def _merge(yr, os_, gates, x, wr, ws, wo, g_ffn):
    m, d = x.shape
    rw = yr.shape[1]
    bm = _pick(m, 256, SUBLANES)
    blk = bm * (2 * rw * 2 + 2 * d * 2 + d * 4 + d * 4 + d * 2) + (2 * rw * d + d * d)
    return pl.pallas_call(
        _merge_kernel,
        out_shape=(jax.ShapeDtypeStruct((m, d), F32), jax.ShapeDtypeStruct((m, d), BF16)),
        grid=(m // bm,),
        in_specs=[
            pl.BlockSpec((bm, rw), lambda i: (i, 0)),
            pl.BlockSpec((bm, rw), lambda i: (i, 0)),
            pl.BlockSpec((bm, d), lambda i: (i, 0)),
            pl.BlockSpec((bm, d), lambda i: (i, 1)),
            pl.BlockSpec((bm, d), lambda i: (i, 0)),
            _resident(wr.shape), _resident(ws.shape), _resident(wo.shape),
            pl.BlockSpec((1, d), lambda i: (0, 0)),
        ],
        out_specs=(pl.BlockSpec((bm, d), lambda i: (i, 0)), pl.BlockSpec((bm, d), lambda i: (i, 0))),
        compiler_params=pltpu.CompilerParams(
            dimension_semantics=("parallel",),
            vmem_limit_bytes=_vmem_limit(blk + bm * d * 16)),
        name="merge",
    )(yr, os_, gates, gates, x, wr, ws, wo, g_ffn)


def _gelu_tanh(x):
    return 0.5 * x * (1.0 + jnp.tanh(math.sqrt(2.0 / math.pi) * (x + 0.044715 * x * x * x)))


def _ffn_kernel(hn_ref, wg_ref, wv_ref, cw_ref, wd_ref, prev_ref, o_ref, tail_ref, carry_ref,
                *, blocks_per_seq, seq_len):
    i = pl.program_id(0)
    f = pl.program_id(1)

    @pl.when(f == 0)
    def _():
        o_ref[...] = jnp.zeros(o_ref.shape, F32)

    hn = hn_ref[...]
    ug = _dot(hn, wg_ref[...])
    uv = _dot(hn, wv_ref[...])
    bm = ug.shape[0]
    row = _iota2(ug.shape, 0)
    r1 = pltpu.roll(ug, 1, 0)
    r2 = pltpu.roll(ug, 2, 0)
    if seq_len == SUBLANES:
        p2 = prev_ref[...]
        p1 = pltpu.roll(p2, bm - 1, 0)
        t = _mod_pow2(row, SUBLANES)
        s1 = jnp.where(t == 0, p1, r1)
        s2 = jnp.where(t < 2, p2, r2)
        tail_ref[...] = ug
    else:
        first = (i % blocks_per_seq) == 0
        pv = jnp.where(first, prev_ref[0], carry_ref[f])
        head = _iota2((SUBLANES, ug.shape[1]), 0)
        s1 = jnp.concatenate(
            [jnp.where(head == 0, pv[7:8], r1[:SUBLANES]), r1[SUBLANES:]], axis=0)
        s2 = jnp.concatenate(
            [jnp.where(head == 0, pv[6:7], jnp.where(head == 1, pv[7:8], r2[:SUBLANES])),
             r2[SUBLANES:]], axis=0)
        carry_ref[f] = ug[bm - SUBLANES:]
        tail_ref[0] = ug[bm - SUBLANES:]
    conv = cw_ref[0:1, :] * s2 + cw_ref[1:2, :] * s1 + cw_ref[2:3, :] * ug
    act = (_gelu_tanh(conv) * uv).astype(BF16)
    o_ref[...] += _dot(act, wd_ref[...])


def _ffn(hn, w_up, conv_w, w_down, prev8, seq_len):
    m, d = hn.shape
    ff = w_down.shape[0]
    bf = _pick(ff, 512, 2 * LANES) if ff % (2 * LANES) == 0 else _pick(ff, 512, LANES)
    nf = ff // bf
    if seq_len == SUBLANES:
        bm = m
        blocks_per_seq = 1
        prev_spec = pl.BlockSpec((bm, bf), lambda i, f: (i, f))
        tail_shape = (m, ff)
        tail_spec = pl.BlockSpec((bm, bf), lambda i, f: (i, f))
    else:
        bm = _pick(seq_len, 1024, SUBLANES)
        blocks_per_seq = seq_len // bm
        prev_spec = pl.BlockSpec((1, SUBLANES, bf), lambda i, f: (i // blocks_per_seq, 0, f))
        tail_shape = (m // bm, SUBLANES, ff)
        tail_spec = pl.BlockSpec((1, SUBLANES, bf), lambda i, f: (i, 0, f))
    blk = bm * d * (2 + 4) + 3 * d * bf * 2 + 4 * bm * bf * 4
    out, tail = pl.pallas_call(
        functools.partial(_ffn_kernel, blocks_per_seq=blocks_per_seq, seq_len=seq_len),
        out_shape=(jax.ShapeDtypeStruct((m, d), F32), jax.ShapeDtypeStruct(tail_shape, F32)),
        grid=(m // bm, nf),
        in_specs=[
            pl.BlockSpec((bm, d), lambda i, f: (i, 0)),
            pl.BlockSpec((d, bf), lambda i, f: (0, f)),
            pl.BlockSpec((d, bf), lambda i, f: (0, nf + f)),
            pl.BlockSpec((3, bf), lambda i, f: (0, f)),
            pl.BlockSpec((bf, d), lambda i, f: (f, 0)),
            prev_spec,
        ],
        out_specs=(pl.BlockSpec((bm, d), lambda i, f: (i, 0)), tail_spec),
        scratch_shapes=[pltpu.VMEM((nf, SUBLANES, bf), F32)],
        compiler_params=pltpu.CompilerParams(
            dimension_semantics=("arbitrary", "arbitrary"),
            vmem_limit_bytes=_vmem_limit(blk)),
        name="ffn",
    )(hn, w_up, w_up, conv_w, w_down, prev8)
    n_seq = m // seq_len
    if seq_len == SUBLANES:
        return out, tail.reshape(n_seq, SUBLANES, ff)
    return out, tail.reshape(n_seq, blocks_per_seq, SUBLANES, ff)[:, -1]


def _ple_kernel(h_ref, f_ref, pe_ref, wple_ref, wpg_ref, gp_ref, gf_ref, y_ref, *, final_norm):
    h2 = h_ref[...] + f_ref[...]
    hn = _rms(h2, gp_ref[...]).astype(BF16)
    gate = jax.nn.sigmoid(_dot(hn, wpg_ref[...]))
    emb = _dot(pe_ref[...].astype(BF16), wple_ref[...])
    h3 = h2 + emb * gate
    y_ref[...] = _rms(h3, gf_ref[...]) if final_norm else h3


def _ple_out(h, ffn_out, pe, w_ple, w_pg, g_ple, g_final, final_norm):
    m, d = h.shape
    pd = pe.shape[1]
    bm = _pick(m, 512, SUBLANES)
    blk = bm * (3 * d * 4 + pd * 4) + pd * d * 2 + d * d * 2
    return pl.pallas_call(
        functools.partial(_ple_kernel, final_norm=final_norm),
        out_shape=jax.ShapeDtypeStruct((m, d), F32),
        grid=(m // bm,),
        in_specs=[
            pl.BlockSpec((bm, d), lambda i: (i, 0)),
            pl.BlockSpec((bm, d), lambda i: (i, 0)),
            pl.BlockSpec((bm, pd), lambda i: (i, 0)),
            _resident(w_ple.shape), _resident(w_pg.shape),
            pl.BlockSpec((1, d), lambda i: (0, 0)),
            pl.BlockSpec((1, d), lambda i: (0, 0)),
        ],
        out_specs=pl.BlockSpec((bm, d), lambda i: (i, 0)),
        compiler_params=pltpu.CompilerParams(
            dimension_semantics=("parallel",),
            vmem_limit_bytes=_vmem_limit(blk + bm * d * 8)),
        name="ple_out",
    )(h, ffn_out, pe, w_ple, w_pg, g_ple, g_final)


def _prep_weights(g_mix, w_in, mu_shift, w0, w2, a0, a2, g2, k_k, k_a, r_k, ln_x_g, ln_x_b,
                  w_br_r, w_br_s, w_o, g_ffn, w_up, conv_w, w_down, g_ple, w_ple, w_pg, sb_bias):
    d = w_in.shape[0]
    rw = w0.shape[0]
    dl, il, gl = w2.shape[0], a2.shape[0], g2.shape[0]
    dlp, ilp, glp = (_ceil_to(n, LANES) for n in (dl, il, gl))
    sw = w_br_s.shape[0]
    rc = 3 * rw + dl + il + gl
    c_w, c_a, c_g = 3 * rw, 3 * rw + dl, 3 * rw + dl + il

    def regroup(x):
        return jnp.concatenate([
            x[..., :c_w], _pad_to(x[..., c_w:c_a], -1, dlp), _pad_to(x[..., c_a:c_g], -1, ilp),
            _pad_to(x[..., c_g:rc], -1, glp)], axis=-1)

    w_r = regroup(w_in[:, :rc]).astype(BF16)
    scale = HEAD_DIM ** -0.5
    w_q = (w_in[:, rc:rc + sw] * scale).astype(BF16)
    w_k = w_in[:, rc + sw:rc + 2 * sw].astype(BF16)
    w_v = w_in[:, rc + 2 * sw:rc + 3 * sw].astype(BF16)
    w_g = w_in[:, rc + 3 * sw:].astype(BF16)
    nh_s = sw // HEAD_DIM
    return dict(
        d=d, rw=rw, sw=sw, dl=dl, il=il, gl=gl, dlp=dlp, ilp=ilp, glp=glp, rc=rc,
        regroup=regroup,
        g_mix=g_mix.reshape(1, d), w_all=jnp.concatenate([w_r, w_q, w_k, w_v, w_g], axis=1),
        w_kv_t=jnp.transpose(w_in)[rc + sw:rc + 3 * sw].astype(BF16),
        proj_widths=(w_r.shape[1], sw, sw, sw, w_g.shape[1]),
        mu=regroup(mu_shift).reshape(1, -1),
        w0=w0.reshape(1, rw), a0=a0.reshape(1, rw),
        w2=_pad_to(w2, 0, dlp).astype(BF16), a2=_pad_to(a2, 0, ilp).astype(BF16),
        g2=_pad_to(g2, 0, glp).astype(BF16),
        kkw=k_k.reshape(1, rw), kaw=k_a.reshape(1, rw), rkw=r_k.reshape(1, rw),
        lng=ln_x_g.reshape(1, rw), lnb=ln_x_b.reshape(1, rw),
        w_br_r=w_br_r.astype(BF16), w_br_s=w_br_s.astype(BF16), w_o=w_o.astype(BF16),
        g_ffn=g_ffn.reshape(1, d), w_up=w_up.astype(BF16), conv_w=conv_w,
        w_down=w_down.astype(BF16), g_ple=g_ple.reshape(1, d),
        w_ple=w_ple.astype(BF16), w_pg=w_pg.astype(BF16),
        bias2=jnp.broadcast_to(sb_bias.reshape(nh_s // 2, 2, 1), (nh_s // 2, 2, LANES)).astype(F32),
        sb_bias=sb_bias,
    )


def _pages_transposed(cache):
    n_pool, page, nh, hd = cache.shape
    return jnp.transpose(cache, (0, 2, 3, 1)).reshape(n_pool, nh * hd, page)


def _layer(x3, pe3, shift_prev, wkv_prev, conv_prev, past, wp, g_final, final_norm):
    bsz, t, d = x3.shape
    m = bsz * t
    rw, sw = wp["rw"], wp["sw"]
    nh = rw // HEAD_DIM
    x = x3.reshape(m, d)

    kv_t = past is None
    p_r, q, k_s, v_s, gates = _rms_proj(
        x, wp["g_mix"], wp["w_all"], wp["w_kv_t"], wp["proj_widths"], (F32, BF16, F32, F32, BF16),
        (False, False, kv_t, kv_t, False), t)
    nh_s = sw // HEAD_DIM
    if kv_t:
        k_out = jnp.transpose(k_s.reshape(bsz, nh_s, HEAD_DIM, t), (0, 3, 1, 2))
        v_out = jnp.transpose(v_s.reshape(bsz, nh_s, HEAD_DIM, t), (0, 3, 1, 2))
    else:
        k_out = k_s.reshape(bsz, t, nh_s, HEAD_DIM)
        v_out = v_s.reshape(bsz, t, nh_s, HEAD_DIM)

    p3 = p_r.reshape(bsz, t, -1)
    shift = wp["regroup"](shift_prev).reshape(bsz, 1, -1)
    g_c, h_c, rp, yl, gate, bonus = _wkv_intra(
        p3, shift, wp["mu"], wp["w0"], wp["a0"], wp["w2"], wp["a2"], wp["g2"],
        wp["kkw"], wp["kaw"], wp["rkw"], rw=rw, dlp=wp["dlp"], ilp=wp["ilp"], glp=wp["glp"])
    y_r, wkv_new = _wkv_scan(g_c, h_c, rp, yl, gate, bonus, wkv_prev, wp["lng"], wp["lnb"])
    y_r = y_r[:, :t].reshape(m, rw)
    last = p3[:, -1]
    dlp, ilp = wp["dlp"], wp["ilp"]
    c0 = 3 * rw
    shift_new = jnp.concatenate([
        last[:, :c0], last[:, c0:c0 + wp["dl"]], last[:, c0 + dlp:c0 + dlp + wp["il"]],
        last[:, c0 + dlp + ilp:c0 + dlp + ilp + wp["gl"]]], axis=-1)

    if past is None:
        o_s = _attn_prompt(q.reshape(bsz, t, sw), k_s, v_s, wp["bias2"])
    else:
        cache_k, cache_v, page_table = past
        nh_s = sw // HEAD_DIM
        rows_n = nh_s * t
        q3 = q.reshape(bsz, t, sw)
        rr = jnp.arange(rows_n)[:, None] // t
        cc = jnp.arange(sw)[None, :] // HEAD_DIM
        q_bd = jnp.where(rr == cc, jnp.tile(q3, (1, nh_s, 1)), jnp.zeros((), BF16))
        k_new = _pad_to(k_s.reshape(bsz, t, sw).astype(BF16), 1, LANES)
        v_new = _pad_to(v_s.reshape(bsz, t, sw).astype(BF16), 1, LANES)
        bias_rows = jnp.broadcast_to(jnp.repeat(wp["sb_bias"].astype(F32), t)[:, None], (rows_n, LANES))
        o_s = _attn_paged(q_bd, k_new, v_new, bias_rows,
                          _pages_transposed(cache_k), _pages_transposed(cache_v), page_table, t)
    o_s = o_s.reshape(m, sw)

    h, hn = _merge(y_r, o_s, gates, x, wp["w_br_r"], wp["w_br_s"], wp["w_o"], wp["g_ffn"])
    ff = wp["w_down"].shape[0]
    nprev = conv_prev.shape[1]
    pad_rows = jnp.zeros((bsz, SUBLANES - nprev, ff), F32)
    if t == SUBLANES:
        prev8 = jnp.concatenate([conv_prev, pad_rows], axis=1).reshape(m, ff)
    else:
        prev8 = jnp.concatenate([pad_rows, conv_prev], axis=1)
    ffn_out, tail = _ffn(hn, wp["w_up"], wp["conv_w"], wp["w_down"], prev8, t)
    conv_new = tail[:, SUBLANES - nprev:]
    y = _ple_out(h, ffn_out, pe3.reshape(m, -1), wp["w_ple"], wp["w_pg"], wp["g_ple"],
                 g_final.reshape(1, d), final_norm)
    return (y.reshape(bsz, t, d), shift_new, wkv_new, conv_new,
            k_out, v_out)


def kernel(x_prompt, x_sample, state_shift, state_wkv, state_conv, cache_k, cache_v, page_table, p_prompt, p_sample, g_mix, w_in, sb_bias, mu_shift, w0, w2, a0, a2, g2, k_k, k_a, r_k, ln_x_g, ln_x_b, w_br_r, w_br_s, w_o, g_ffn, w_up, conv_w, w_down, g_ple, w_ple, w_pg, g_final):
    depth = w_in.shape[0]
    bsz = x_prompt.shape[0]
    rw = w0.shape[1]
    nh = rw // HEAD_DIM
    ff = w_down.shape[1]
    rc = state_shift.shape[-1]
    nprev = state_conv.shape[2]
    h_p, h_s = x_prompt, x_sample
    outs_p = [[] for _ in range(5)]
    outs_s = [[] for _ in range(5)]
    for i in range(depth):
        wp = _prep_weights(g_mix[i], w_in[i], mu_shift[i], w0[i], w2[i], a0[i], a2[i], g2[i], k_k[i],
                           k_a[i], r_k[i], ln_x_g[i], ln_x_b[i], w_br_r[i], w_br_s[i], w_o[i],
                           g_ffn[i], w_up[i], conv_w[i], w_down[i], g_ple[i], w_ple[i], w_pg[i],
                           sb_bias[i])
        last = i == depth - 1
        res_p = _layer(h_p, p_prompt[i], jnp.zeros((bsz, rc), F32),
                       jnp.zeros((bsz, nh, HEAD_DIM, HEAD_DIM), F32),
                       jnp.zeros((bsz, nprev, ff), F32), None, wp, g_final, last)
        res_s = _layer(h_s, p_sample[i], state_shift[i], state_wkv[i], state_conv[i],
                       (cache_k[i], cache_v[i], page_table), wp, g_final, last)
        h_p, h_s = res_p[0], res_s[0]
        for dst, res in ((outs_p, res_p), (outs_s, res_s)):
            for lst, val in zip(dst, res[1:]):
                lst.append(val)
    return (h_p, h_s, *(jnp.stack(o) for o in outs_p), *(jnp.stack(o) for o in outs_s))
```

```python
import functools
import math

import jax
import jax.numpy as jnp
from jax import lax
from jax.experimental import pallas as pl
from jax.experimental.pallas import tpu as pltpu

F32 = jnp.float32
BF16 = jnp.bfloat16

LANES = 128
SUBLANES = 8
V7X_SCOPED_VMEM_BYTES = 60000 * 1024

HEAD_DIM = 64
CHUNK = 128
INTRA_PAIRS_PER_STEP = 8
SCAN_PAIRS_PER_STEP = 4
DECAY_SCALE = 0.606531
NORM_EPS = 1e-6
GN_EPS = HEAD_DIM * 1e-5
KK_EPS = 1e-24


def _vmem_limit(block_bytes):
    return int(min(V7X_SCOPED_VMEM_BYTES, 2 * block_bytes + (16 << 20)))


def _pick(n, pref, align):
    if n <= pref:
        return n
    best = None
    for d in range(align, pref + 1, align):
        if n % d == 0:
            best = d
    assert best is not None, (n, pref, align)
    return best


def _pad_to(x, axis, size):
    pad = size - x.shape[axis]
    if pad == 0:
        return x
    widths = [(0, 0)] * x.ndim
    widths[axis] = (0, pad)
    return jnp.pad(x, widths)


def _ceil_to(n, m):
    return -(-n // m) * m


def _dot(a, b):
    return jnp.dot(a, b, preferred_element_type=F32)


def _dot_nt(a, b):
    return lax.dot_general(a, b, (((1,), (1,)), ((), ())), preferred_element_type=F32)


def _dot_tn(a, b):
    return lax.dot_general(a, b, (((0,), (0,)), ((), ())), preferred_element_type=F32)


def _split2(x):
    hi = x.astype(BF16)
    lo = (x - hi.astype(F32)).astype(BF16)
    return hi, lo


def _split3(x):
    hi = x.astype(BF16)
    r1 = x - hi.astype(F32)
    mid = r1.astype(BF16)
    lo = (r1 - mid.astype(F32)).astype(BF16)
    return hi, mid, lo


def _iota2(shape, dim):
    return lax.broadcasted_iota(jnp.int32, shape, dim)


def _div_pow2(x, n):
    assert n & (n - 1) == 0, n
    return x >> (n.bit_length() - 1)


def _mod_pow2(x, n):
    assert n & (n - 1) == 0, n
    return x & (n - 1)


def _head_sum(x, e2):
    hi, lo = _split2(x)
    return _dot(jnp.concatenate([hi, lo], axis=1), e2)


def _same_head_matrix():
    r = _mod_pow2(_iota2((2 * LANES, LANES), 0), LANES)
    c = _iota2((2 * LANES, LANES), 1)
    return jnp.where(_div_pow2(r, HEAD_DIM) == _div_pow2(c, HEAD_DIM), 1.0, 0.0).astype(BF16)


def _rms_proj_kernel(x_ref, g_ref, w_ref, wt_ref, *refs, bounds, transposed):
    o_refs, xn_ref = refs[:-1], refs[-1]
    j = pl.program_id(1)

    @pl.when(j == 0)
    def _():
        x = x_ref[...]
        ms = jnp.mean(x * x, axis=-1, keepdims=True)
        xn_ref[...] = (x * lax.rsqrt(ms + NORM_EPS) * g_ref[...]).astype(BF16)

    for o_ref, (lo, hi), tr in zip(o_refs, bounds, transposed):
        @pl.when((j >= lo) & (j < hi))
        def _(o_ref=o_ref, tr=tr):
            if tr:
                o_ref[0] = _dot_nt(wt_ref[...], xn_ref[...]).astype(o_ref.dtype)
            else:
                o_ref[...] = _dot(xn_ref[...], w_ref[...]).astype(o_ref.dtype)


def _rms_proj(x, g, w, w_t, widths, dtypes, transposed, seq_len):
    m, d = x.shape
    assert w.shape[1] == sum(widths)
    bm = _pick(seq_len, 1024, SUBLANES) if any(transposed) else _pick(m, 1024, SUBLANES)
    bps = seq_len // bm
    bn = _pick(math.gcd(*widths), 512, LANES)
    bounds, lo = [], 0
    for wd in widths:
        bounds.append((lo, lo + wd // bn))
        lo += wd // bn
    t_ranges = [b for b, tr in zip(bounds, transposed) if tr]
    t_lo = t_ranges[0][0] if t_ranges else 0
    t_hi = t_ranges[-1][1] if t_ranges else 0
    assert sum(hi - a for a, hi in t_ranges) == t_hi - t_lo
    if not t_ranges:
        w_t = w_t[:bn]
    assert w_t.shape == (max(t_hi - t_lo, 1) * bn, d)

    def w_index(i, j):
        return (0, jnp.where((j >= t_lo) & (j < t_hi), max(t_lo - 1, 0), j))

    def out_spec(lo, hi, tr):
        if tr:
            return pl.BlockSpec((1, bn, bm),
                                lambda i, j: (i // bps, jnp.clip(j - lo, 0, hi - lo - 1), i % bps))
        return pl.BlockSpec((bm, bn), lambda i, j: (i, jnp.clip(j - lo, 0, hi - lo - 1)))

    out_shape = tuple(
        jax.ShapeDtypeStruct((m // seq_len, wd, seq_len) if tr else (m, wd), t)
        for wd, t, tr in zip(widths, dtypes, transposed))
    blk = (bm * d * 4 + 2 * d * bn * 2 + bm * d * 2
           + sum(bm * bn * jnp.dtype(t).itemsize for t in dtypes))
    return pl.pallas_call(
        functools.partial(_rms_proj_kernel, bounds=tuple(bounds), transposed=tuple(transposed)),
        out_shape=out_shape,
        grid=(m // bm, lo),
        in_specs=[
            pl.BlockSpec((bm, d), lambda i, j: (i, 0)),
            pl.BlockSpec((1, d), lambda i, j: (0, 0)),
            pl.BlockSpec((d, bn), w_index),
            pl.BlockSpec((bn, d), lambda i, j: (jnp.clip(j - t_lo, 0, max(t_hi - t_lo, 1) - 1), 0)),
        ],
        out_specs=tuple(out_spec(a, b, tr) for (a, b), tr in zip(bounds, transposed)),
        scratch_shapes=[pltpu.VMEM((bm, d), BF16)],
        compiler_params=pltpu.CompilerParams(
            dimension_semantics=("parallel", "arbitrary"),
            vmem_limit_bytes=_vmem_limit(blk)),
        name="rms_proj",
    )(x, g, w, w_t)


def _wkv_intra_kernel(r_ref, k_ref, v_ref, xw_ref, xa_ref, xg_ref,
                      rp_ref, kp_ref, vp_ref, xwp_ref, xap_ref, xgp_ref,
                      rs_ref, ks_ref, vs_ref, xws_ref, xas_ref, xgs_ref,
                      mur_ref, muk_ref, muv_ref, muw_ref, mua_ref, mug_ref,
                      w0_ref, a0_ref, w2_ref, a2_ref, g2_ref,
                      kkw_ref, kaw_ref, rkw_ref,
                      g_out, h_out, rp_out, yl_out, gate_out, bonus_out,
                      *, t_real, n_pairs):
    c_idx = pl.program_id(1)
    C = CHUNK
    first = c_idx == 0

    def mixed(x_ref, p_ref, s_ref, mu_ref):
        x = x_ref[0]
        prev_last = jnp.where(first, s_ref[0], p_ref[0][SUBLANES - 1:SUBLANES])
        rolled = pltpu.roll(x, 1, 0)
        prev = jnp.where(_iota2(x.shape, 0) == 0, prev_last, rolled)
        y = x + (prev - x) * mu_ref[...]
        if t_real < C:
            y = jnp.concatenate([y, jnp.zeros((C - t_real, y.shape[1]), F32)], axis=0)
        return y

    r_all = mixed(r_ref, rp_ref, rs_ref, mur_ref)
    k_all = mixed(k_ref, kp_ref, ks_ref, muk_ref)
    v_all = mixed(v_ref, vp_ref, vs_ref, muv_ref)
    xw = jnp.tanh(mixed(xw_ref, xwp_ref, xws_ref, muw_ref)).astype(BF16)
    xa = mixed(xa_ref, xap_ref, xas_ref, mua_ref).astype(BF16)
    xg = jax.nn.sigmoid(mixed(xg_ref, xgp_ref, xgs_ref, mug_ref)).astype(BF16)

    lw_all = -DECAY_SCALE * jax.nn.sigmoid(w0_ref[...] + _dot(xw, w2_ref[...]))
    if t_real < C:
        lw_all = jnp.where(_iota2(lw_all.shape, 0) < t_real, lw_all, 0.0)
    iclr_all = jax.nn.sigmoid(a0_ref[...] + _dot(xa, a2_ref[...]))
    gate_all = _dot(xg, g2_ref[...])
    gate_out[0] = gate_all.astype(gate_out.dtype)

    e2 = _same_head_matrix()
    row = _iota2((C, C), 0)
    col = _iota2((C, C), 1)
    strict_lower = row > col
    lower = row >= col
    l_incl = jnp.where(lower, 1.0, 0.0).astype(BF16)
    l3 = jnp.concatenate([l_incl, l_incl, l_incl], axis=1)
    lane = _iota2((C, LANES), 1)
    head_masks = (lane < HEAD_DIM, lane >= HEAD_DIM)
    rr = _iota2((LANES, LANES), 0)
    cc = _iota2((LANES, LANES), 1)
    block_mask = _div_pow2(rr, HEAD_DIM) == _div_pow2(cc, HEAD_DIM)
    eye_mask = rr == cc

    pairs = range(n_pairs)
    sls = [slice(p * LANES, (p + 1) * LANES) for p in pairs]
    r = [r_all[:, sl] for sl in sls]
    k = [k_all[:, sl] for sl in sls]
    v = [v_all[:, sl] for sl in sls]
    lw = [lw_all[:, sl] for sl in sls]
    iclr = [iclr_all[:, sl] for sl in sls]

    kkr = [k[p] * kkw_ref[:, sls[p]] for p in pairs]
    k_mod = [k[p] * (1.0 + (iclr[p] - 1.0) * kaw_ref[:, sls[p]]) for p in pairs]
    kk_ss = [_head_sum(kkr[p] * kkr[p], e2) for p in pairs]
    rk_sum = [_head_sum(r[p] * k_mod[p] * rkw_ref[:, sls[p]], e2) for p in pairs]
    cum = [_dot(l3, jnp.concatenate(_split3(lw[p]), axis=0)) for p in pairs]

    vb, em, ecl, kbar, bbar, rhs_scores, lhs_scores, am_b, rm_f = [], [], [], [], [], [], [], [], []
    for p in pairs:
        bonus_out[0, :, sls[p]] = rk_sum[p] * v[p]
        kk = kkr[p] * lax.rsqrt(jnp.maximum(kk_ss[p], KK_EPS))
        b = kk * iclr[p]
        m_row = cum[p][C // 2 - 1:C // 2]
        cum_last = cum[p][C - 1:C]
        g = cum[p] - m_row
        eng = jnp.exp(-g)
        ebar = jnp.exp(cum_last - cum[p])
        at = -kk * jnp.exp(g - lw[p])
        rt = r[p] * jnp.exp(g)
        kbar.append((k_mod[p] * ebar).astype(BF16))
        bbar.append((b * ebar).astype(BF16))
        em.append(jnp.exp(m_row))
        ecl.append(jnp.exp(cum_last))
        vb.append(v[p].astype(BF16))
        rhs_scores.append(jnp.concatenate([(k_mod[p] * eng).astype(BF16), (b * eng).astype(BF16)], axis=0))
        stack = []
        for mh in head_masks:
            am_b.append(jnp.where(mh, at, 0.0).astype(BF16))
            rm_f.append(jnp.where(mh, rt, 0.0))
            stack += [am_b[-1], rm_f[-1].astype(BF16)]
        lhs_scores.append(jnp.concatenate(stack, axis=0))

    sc = [_dot_nt(lhs_scores[p], rhs_scores[p]) for p in pairs]
    heads = range(2 * n_pairs)
    m_ak, m_ab, p_rk_b, p_rb_b = [], [], [], []
    for i in heads:
        s = sc[i // 2][(i % 2) * 2 * C:(i % 2 + 1) * 2 * C]
        m_ak.append(jnp.where(strict_lower, s[:C, :C], 0.0).astype(BF16))
        m_ab.append(jnp.where(strict_lower, s[:C, C:], 0.0))
        p_rk_b.append(jnp.where(lower, s[C:, :C], 0.0).astype(BF16))
        p_rb_b.append(jnp.where(lower, s[C:, C:], 0.0).astype(BF16))

    eye = jnp.where(row == col, 1.0, 0.0).astype(F32)
    t = [eye + m_ab[i] for i in heads]
    qb = [m_ab[i].astype(BF16) for i in heads]
    q = [_dot(qb[i], qb[i]) for i in heads]
    w1 = [_dot(m_ak[i], vb[i // 2]).astype(BF16) for i in heads]
    for _ in range(int(math.log2(C)) - 2):
        qb = [q[i].astype(BF16) for i in heads]
        st = [_dot(jnp.concatenate([t[i].astype(BF16), qb[i]], axis=0), qb[i]) for i in heads]
        t = [t[i] + st[i][:C] for i in heads]
        q = [st[i][C:] for i in heads]
    corr = [_dot(t[i].astype(BF16), q[i].astype(BF16)) for i in heads]
    t_inv = [(t[i] + corr[i]).astype(BF16) for i in heads]
    tu = [_dot(t_inv[i], jnp.concatenate([w1[i], am_b[i]], axis=1)) for i in heads]
    ul = [tu[i][:, :LANES] for i in heads]
    ap = [tu[i][:, LANES:] for i in heads]
    yl = [_dot(jnp.concatenate([p_rk_b[i], p_rb_b[i]], axis=1),
               jnp.concatenate([vb[i // 2], ul[i].astype(BF16)], axis=0)) for i in heads]
    rp_add = [_dot(p_rb_b[i], ap[i].astype(BF16)) for i in heads]

    m1 = head_masks[1]
    ul_pair = [jnp.where(m1, ul[2 * p + 1], ul[2 * p]).astype(BF16) for p in pairs]
    ap_pair = [((ap[2 * p] + ap[2 * p + 1]) * em[p]).astype(BF16) for p in pairs]
    ab = [_dot_tn(ap_pair[p], bbar[p]) for p in pairs]
    hh = [_dot_tn(jnp.concatenate([vb[p], ul_pair[p]], axis=0),
                  jnp.concatenate([kbar[p], bbar[p]], axis=0)) for p in pairs]
    for p in pairs:
        sl = sls[p]
        g_out[0, 0, :, sl] = jnp.where(block_mask, ab[p], 0.0) + jnp.where(eye_mask, ecl[p], 0.0)
        h_out[0, 0, :, sl] = jnp.where(block_mask, hh[p], 0.0)
        rp_pair = rm_f[2 * p] + rp_add[2 * p] + rm_f[2 * p + 1] + rp_add[2 * p + 1]
        rp_out[0, :, sl] = (rp_pair * em[p]).astype(rp_out.dtype)
        yl_out[0, :, sl] = jnp.where(m1, yl[2 * p + 1], yl[2 * p])


def _wkv_intra(p3, shift, mu, w0, a0, w2, a2, g2, kkw, kaw, rkw, *, rw, dlp, ilp, glp):
    bsz, t, npc = p3.shape
    C = CHUNK
    t_real = min(t, C)
    assert t % t_real == 0 and t_real % SUBLANES == 0
    nch = t // t_real
    tp = nch * C
    n_pairs = _pick(rw // LANES, INTRA_PAIRS_PER_STEP, 1)
    lw_ = n_pairs * LANES
    ng = rw // lw_
    o_w, o_a, o_g = 3 * rw, 3 * rw + dlp, 3 * rw + dlp + ilp
    assert o_w % dlp == 0 and o_a % ilp == 0 and o_g % glp == 0
    rows_prev = t_real // SUBLANES

    def cur(width, off):
        return pl.BlockSpec((1, t_real, width), lambda b, c, g, o=off // width: (b, c, o))

    def cur_g(width, off):
        return pl.BlockSpec((1, t_real, width), lambda b, c, g, o=off // width: (b, c, o + g))

    def prev(width, off):
        return pl.BlockSpec((1, SUBLANES, width),
                            lambda b, c, g, o=off // width: (b, jnp.maximum(c * rows_prev - 1, 0), o))

    def prev_g(width, off):
        return pl.BlockSpec((1, SUBLANES, width),
                            lambda b, c, g, o=off // width: (b, jnp.maximum(c * rows_prev - 1, 0), o + g))

    def sh(width, off):
        return pl.BlockSpec((1, 1, width), lambda b, c, g, o=off // width: (b, 0, o))

    def sh_g(width, off):
        return pl.BlockSpec((1, 1, width), lambda b, c, g, o=off // width: (b, 0, o + g))

    def vec(width, off):
        return pl.BlockSpec((1, width), lambda b, c, g, o=off // width: (0, o))

    def vec_g(width, off=0):
        return pl.BlockSpec((1, width), lambda b, c, g, o=off // width: (0, o + g))

    def mat_g(rows):
        return pl.BlockSpec((rows, lw_), lambda b, c, g: (0, g))

    in_specs = (
        [cur_g(lw_, 0), cur_g(lw_, rw), cur_g(lw_, 2 * rw), cur(dlp, o_w), cur(ilp, o_a), cur(glp, o_g)]
        + [prev_g(lw_, 0), prev_g(lw_, rw), prev_g(lw_, 2 * rw), prev(dlp, o_w), prev(ilp, o_a), prev(glp, o_g)]
        + [sh_g(lw_, 0), sh_g(lw_, rw), sh_g(lw_, 2 * rw), sh(dlp, o_w), sh(ilp, o_a), sh(glp, o_g)]
        + [vec_g(lw_, 0), vec_g(lw_, rw), vec_g(lw_, 2 * rw), vec(dlp, o_w), vec(ilp, o_a), vec(glp, o_g)]
        + [vec_g(lw_), vec_g(lw_), mat_g(dlp), mat_g(ilp), mat_g(glp)]
        + [vec_g(lw_), vec_g(lw_), vec_g(lw_)]
    )
    args = ([p3] * 6 + [p3] * 6 + [shift] * 6 + [mu] * 6 + [w0, a0, w2, a2, g2, kkw, kaw, rkw])
    out_shape = (
        jax.ShapeDtypeStruct((bsz, nch, LANES, rw), F32),
        jax.ShapeDtypeStruct((bsz, nch, LANES, rw), F32),
        jax.ShapeDtypeStruct((bsz, tp, rw), BF16),
        jax.ShapeDtypeStruct((bsz, tp, rw), F32),
        jax.ShapeDtypeStruct((bsz, tp, rw), BF16),
        jax.ShapeDtypeStruct((bsz, tp, rw), F32),
    )
    gh_spec = pl.BlockSpec((1, 1, LANES, lw_), lambda b, c, g: (b, c, 0, g))
    tok_spec = pl.BlockSpec((1, C, lw_), lambda b, c, g: (b, c, g))
    blk = (t_real * (3 * lw_ + dlp + ilp + glp) * 4 + (dlp + ilp + glp) * lw_ * 2
           + 2 * LANES * lw_ * 4 + C * lw_ * 12)
    return pl.pallas_call(
        functools.partial(_wkv_intra_kernel, t_real=t_real, n_pairs=n_pairs),
        out_shape=out_shape,
        grid=(bsz, nch, ng),
        in_specs=in_specs,
        out_specs=(gh_spec, gh_spec, tok_spec, tok_spec, tok_spec, tok_spec),
        compiler_params=pltpu.CompilerParams(
            dimension_semantics=("parallel", "parallel", "parallel"),
            vmem_limit_bytes=_vmem_limit(blk)),
        name="wkv_intra",
    )(*args)


def _wkv_scan_kernel(g_ref, h_ref, rp_ref, yl_ref, gate_ref, bonus_ref, s0_ref, lng_ref, lnb_ref,
                     y_ref, s_out_ref, st_ref, *, n_chunks, n_pairs):
    C = CHUNK
    N = HEAD_DIM
    e2 = _same_head_matrix()
    inv_n = 1.0 / HEAD_DIM
    lanes = [slice(p * LANES, (p + 1) * LANES) for p in range(n_pairs)]

    zero = jnp.zeros((N, N), F32)
    for p, sl in enumerate(lanes):
        st_ref[:, sl] = jnp.concatenate(
            [jnp.concatenate([s0_ref[0, 2 * p], zero], axis=1),
             jnp.concatenate([zero, s0_ref[0, 2 * p + 1]], axis=1)], axis=0)

    def body(c, carry):
        rows = pl.ds(pl.multiple_of(c * C, C), C)
        sb = [st_ref[:, sl].astype(BF16) for sl in lanes]
        y = [_dot_nt(rp_ref[0, rows, sl], s) for sl, s in zip(lanes, sb)]
        s_new = [_dot(s, g_ref[0, c, :, sl].astype(BF16)) for sl, s in zip(lanes, sb)]
        for sl, s in zip(lanes, s_new):
            st_ref[:, sl] = s + h_ref[0, c, :, sl]
        y = [yy + yl_ref[0, rows, sl] for sl, yy in zip(lanes, y)]
        mu = [_head_sum(yy, e2) * inv_n for yy in y]
        yc = [yy - m for yy, m in zip(y, mu)]
        var = [_head_sum(x * x, e2) * inv_n for x in yc]
        for sl, x, vv in zip(lanes, yc, var):
            yn = x * lax.rsqrt(vv + GN_EPS) * lng_ref[:, sl] + lnb_ref[:, sl]
            out = (yn + bonus_ref[0, rows, sl]) * gate_ref[0, rows, sl].astype(F32)
            y_ref[0, rows, sl] = out.astype(y_ref.dtype)
        return carry

    lax.fori_loop(0, n_chunks, body, 0)
    for p, sl in enumerate(lanes):
        s = st_ref[:, sl]
        s_out_ref[0, 2 * p] = s[:N, :N]
        s_out_ref[0, 2 * p + 1] = s[N:, N:]


def _wkv_scan(g, h, rp, yl, gate, bonus, s0, lng, lnb):
    bsz, nch, _, rw = g.shape
    tp = rp.shape[1]
    n_pairs = _pick(rw // LANES, SCAN_PAIRS_PER_STEP, 1)
    lw_ = n_pairs * LANES
    gh_spec = pl.BlockSpec((1, nch, LANES, lw_), lambda b, q: (b, 0, 0, q))
    tok_spec = pl.BlockSpec((1, tp, lw_), lambda b, q: (b, 0, q))
    st_spec = pl.BlockSpec((1, 2 * n_pairs, HEAD_DIM, HEAD_DIM), lambda b, q: (b, q, 0, 0))
    vec_spec = pl.BlockSpec((1, lw_), lambda b, q: (0, q))
    blk = 2 * nch * LANES * lw_ * 4 + tp * lw_ * (2 + 4 + 2 + 4 + 2) + 4 * LANES * lw_ * 4
    return pl.pallas_call(
        functools.partial(_wkv_scan_kernel, n_chunks=nch, n_pairs=n_pairs),
        out_shape=(jax.ShapeDtypeStruct((bsz, tp, rw), BF16),
                   jax.ShapeDtypeStruct(s0.shape, F32)),
        grid=(bsz, rw // lw_),
        in_specs=[gh_spec, gh_spec, tok_spec, tok_spec, tok_spec, tok_spec, st_spec, vec_spec, vec_spec],
        out_specs=(tok_spec, st_spec),
        scratch_shapes=[pltpu.VMEM((LANES, lw_), F32)],
        compiler_params=pltpu.CompilerParams(
            dimension_semantics=("parallel", "parallel"),
            vmem_limit_bytes=_vmem_limit(blk)),
        name="wkv_scan",
    )(g, h, rp, yl, gate, bonus, s0, lng, lnb)


def _cumsum_rhs():
    r = _iota2((LANES, 2 * LANES), 0)
    c = _iota2((LANES, 2 * LANES), 1)
    return jnp.where((c >= LANES) | (r > c), -1.0, 0.0).astype(BF16)


def _keep_sums(sp, w2):
    return _dot(sp.astype(BF16), w2)


SOFTPLUS_LINEAR_ABOVE = 30.0


def _softplus(z, mask=None):
    sp = jnp.log(1.0 + jnp.exp(jnp.minimum(z, SOFTPLUS_LINEAR_ABOVE)))
    sp = jnp.where(z > SOFTPLUS_LINEAR_ABOVE, z, sp)
    return sp if mask is None else jnp.where(mask, sp, 0.0)


def _sb_tile(z, v_tile, run, acc, w2, mask):
    sp = _softplus(z, mask)
    cs2 = _keep_sums(sp, w2)
    att = jnp.exp((z - sp) + cs2[:, :LANES] + run)
    if mask is not None:
        att = jnp.where(mask, att, 0.0)
    acc = acc + _dot(att.astype(BF16), v_tile)
    return run + cs2[:, LANES:], acc


def _attn_prompt_kernel(q_ref, k_ref, v_ref, bias_ref, o_ref, kb_ref, vb_ref, run_ref, acc_ref):
    qi = pl.program_id(2)
    qb = q_ref.shape[1]
    kb = 2 * LANES
    n_diag = qb // kb
    assert qb == n_diag * kb and n_diag in (1, 2)

    @pl.when(qi == 0)
    def _():
        for c in range(kb_ref.shape[0]):
            kb_ref[c] = k_ref[0, :, c * LANES:(c + 1) * LANES].astype(BF16)
            vb_ref[c] = v_ref[0, :, c * LANES:(c + 1) * LANES].astype(BF16)

    q = q_ref[0]
    m0 = _iota2((qb, LANES), 1) < HEAD_DIM
    zero = jnp.zeros((), BF16)
    qs = jnp.concatenate([jnp.where(m0, q, zero), jnp.where(m0, zero, q)], axis=0)
    w2 = _cumsum_rhs()
    run_ref[...] = jnp.zeros(run_ref.shape, F32)
    acc_ref[...] = jnp.zeros(acc_ref.shape, F32)

    def step(j, n_blocks, mask):
        nk = 2 * n_blocks
        t0 = 2 * j
        z = _dot(qs, jnp.concatenate([kb_ref[t0 + c] for c in range(nk)], axis=1))
        b0 = jnp.concatenate([bias_ref[0, 0:1, :]] * nk, axis=1)
        b1 = jnp.concatenate([bias_ref[0, 1:2, :]] * nk, axis=1)
        z = jnp.concatenate([z[:qb] + b0, z[qb:] + b1], axis=0)
        sp = _softplus(z, mask)
        tiles = [slice(c * LANES, (c + 1) * LANES) for c in range(nk)]
        cs = [_keep_sums(sp[:, c], w2) for c in tiles]
        zs = z - sp
        run = run_ref[...]
        e = [None] * nk
        for c in reversed(range(nk)):
            e[c] = zs[:, tiles[c]] + cs[c][:, :LANES] + run
            run = run + cs[c][:, LANES:]
        att = jnp.exp(jnp.concatenate(e, axis=1))
        if mask is not None:
            att = jnp.where(mask, att, 0.0)
        vt = jnp.concatenate([vb_ref[t0 + c] for c in range(nk)], axis=1)
        acc_ref[...] += _dot_nt(att.astype(BF16), vt)
        run_ref[...] = run

    qpos = _mod_pow2(_iota2((2 * qb, qb), 0), qb)
    step(qi * n_diag, n_diag, _iota2((2 * qb, qb), 1) < qpos)

    n_off = qi * n_diag

    def body(jj, carry):
        step(n_off - 2 - 2 * jj, 2, None)
        return carry

    lax.fori_loop(0, n_off // 2, body, 0)

    if n_diag % 2 == 1:
        @pl.when(n_off % 2 == 1)
        def _():
            step(0, 1, None)

    o_ref[0] = jnp.where(m0, acc_ref[:qb], acc_ref[qb:]).astype(o_ref.dtype)


def _attn_prompt(q, k, v, bias2):
    bsz, t, w = q.shape
    qb = 4 * LANES if t % (4 * LANES) == 0 else 2 * LANES
    assert t % qb == 0
    blk = qb * LANES * 2 * 2 + 2 * t * LANES * 4
    return pl.pallas_call(
        _attn_prompt_kernel,
        out_shape=jax.ShapeDtypeStruct((bsz, t, w), BF16),
        grid=(bsz, w // LANES, t // qb),
        in_specs=[
            pl.BlockSpec((1, qb, LANES), lambda b, p, i: (b, i, p)),
            pl.BlockSpec((1, LANES, t), lambda b, p, i: (b, p, 0)),
            pl.BlockSpec((1, LANES, t), lambda b, p, i: (b, p, 0)),
            pl.BlockSpec((1, 2, LANES), lambda b, p, i: (p, 0, 0)),
        ],
        out_specs=pl.BlockSpec((1, qb, LANES), lambda b, p, i: (b, i, p)),
        scratch_shapes=[pltpu.VMEM((t // LANES, LANES, LANES), BF16),
                        pltpu.VMEM((t // LANES, LANES, LANES), BF16),
                        pltpu.VMEM((2 * qb, LANES), F32), pltpu.VMEM((2 * qb, LANES), F32)],
        compiler_params=pltpu.CompilerParams(
            dimension_semantics=("parallel", "parallel", "arbitrary"),
            vmem_limit_bytes=_vmem_limit(blk + 2 * t * LANES * 2)),
        name="attn_prompt",
    )(q, k, v, bias2)


PAGES_PER_STEP = 16


def _attn_paged_kernel(pt_ref, *refs, n_groups, pps):
    del pt_ref
    k_refs = refs[:pps]
    v_refs = refs[pps:2 * pps]
    q_ref, kn_ref, vn_ref, bias_ref, o_ref, run_ref, acc_ref = refs[2 * pps:]
    g = pl.program_id(1)
    rows_n = q_ref.shape[1]
    t_new = o_ref.shape[1]
    w2 = _cumsum_rhs()
    q = q_ref[0]
    bias = bias_ref[...]

    @pl.when(g == 0)
    def _():
        lane = _iota2((rows_n, LANES), 1)
        row = _iota2((rows_n, LANES), 0)
        mask = lane < _mod_pow2(row, t_new)
        z = _dot_nt(q, kn_ref[0]) + bias
        run, acc = _sb_tile(z, vn_ref[0], jnp.zeros((rows_n, LANES), F32),
                            jnp.zeros(acc_ref.shape, F32), w2, mask)
        run_ref[...] = run
        acc_ref[...] = acc

    pages = range(pps)
    kt = [k_refs[p][0].astype(BF16) for p in pages]
    z2 = [_dot(q, jnp.concatenate(kt[i:i + 2], axis=1)) for i in range(0, pps, 2)]
    z = [z2[p // 2][:, (p % 2) * LANES:(p % 2 + 1) * LANES] + bias for p in pages]
    sp = [_softplus(zz) for zz in z]
    cs2 = [_keep_sums(x, w2) for x in sp]
    run = run_ref[...]
    att = []
    for p in pages:
        att.append(jnp.exp((z[p] - sp[p]) + cs2[p][:, :LANES] + run).astype(BF16))
        run = run + cs2[p][:, LANES:]
    run_ref[...] = run
    vt = jnp.concatenate([v_refs[p][0].astype(BF16) for p in pages], axis=1)
    acc_ref[...] += _dot_nt(jnp.concatenate(att, axis=1), vt)

    @pl.when(g == n_groups - 1)
    def _():
        acc = acc_ref[...]
        rr = _iota2(acc.shape, 0)
        cc = _iota2(acc.shape, 1)
        picked = jnp.where(_div_pow2(rr, t_new) == _div_pow2(cc, HEAD_DIM), acc, 0.0)
        out = picked[0:t_new]
        for h in range(1, rows_n // t_new):
            out = out + picked[h * t_new:(h + 1) * t_new]
        o_ref[0] = out.astype(o_ref.dtype)


def _attn_paged(q_bd, k_new, v_new, bias_rows, cache_k, cache_v, page_table, t_new):
    bsz, rows_n, w = q_bd.shape
    n_pages = page_table.shape[1]
    page = cache_k.shape[2]
    pps = _pick(n_pages, PAGES_PER_STEP, 2)
    assert page == LANES and cache_k.shape[1] == w
    n_groups = n_pages // pps

    def page_spec(p):
        return pl.BlockSpec((1, w, page),
                            lambda b, g, pt, p=p % pps: (pt[b, n_pages - 1 - (g * pps + p)], 0, 0))

    in_specs = ([page_spec(p) for p in range(2 * pps)] + [
        pl.BlockSpec((1, rows_n, w), lambda b, g, pt: (b, 0, 0)),
        pl.BlockSpec((1, LANES, w), lambda b, g, pt: (b, 0, 0)),
        pl.BlockSpec((1, LANES, w), lambda b, g, pt: (b, 0, 0)),
        pl.BlockSpec((rows_n, LANES), lambda b, g, pt: (0, 0)),
    ])
    blk = 2 * pps * page * w * 4 + rows_n * w * 2 + 2 * LANES * w * 2 + rows_n * w * 4
    return pl.pallas_call(
        functools.partial(_attn_paged_kernel, n_groups=n_groups, pps=pps),
        out_shape=jax.ShapeDtypeStruct((bsz, t_new, w), BF16),
        grid_spec=pltpu.PrefetchScalarGridSpec(
            num_scalar_prefetch=1,
            grid=(bsz, n_groups),
            in_specs=in_specs,
            out_specs=pl.BlockSpec((1, t_new, w), lambda b, g, pt: (b, 0, 0)),
            scratch_shapes=[pltpu.VMEM((rows_n, LANES), F32), pltpu.VMEM((rows_n, w), F32)],
        ),
        compiler_params=pltpu.CompilerParams(
            dimension_semantics=("parallel", "arbitrary"),
            vmem_limit_bytes=_vmem_limit(blk)),
        name="attn_paged",
    )(page_table, *([cache_k] * pps), *([cache_v] * pps), q_bd, k_new, v_new, bias_rows)


def _rms(x, g):
    ms = jnp.mean(x * x, axis=-1, keepdims=True)
    return x * lax.rsqrt(ms + NORM_EPS) * g


def _merge_kernel(yr_ref, os_ref, gr_ref, gs_ref, x_ref, wr_ref, ws_ref, wo_ref, g_ref, h_ref, hn_ref):
    a = _dot(yr_ref[...], wr_ref[...])
    b = _dot(os_ref[...], ws_ref[...])
    mixed = (jax.nn.sigmoid(gr_ref[...].astype(F32)) * a
             + jax.nn.sigmoid(gs_ref[...].astype(F32)) * b)
    h = x_ref[...] + _dot(mixed.astype(BF16), wo_ref[...])
    h_ref[...] = h
    hn_ref[...] = _rms(h, g_ref[...]).astype(hn_ref.dtype)


def _resident(shape):
    return pl.BlockSpec(shape, lambda *_: (0,) * len(shape), pipeline_mode=pl.Buffered(1))


def _merge(yr, os_, gates, x, wr, ws, wo, g_ffn):
    m, d = x.shape
    rw = yr.shape[1]
    bm = _pick(m, 256, SUBLANES)
    blk = bm * (2 * rw * 2 + 2 * d * 2 + d * 4 + d * 4 + d * 2) + (2 * rw * d + d * d)
    return pl.pallas_call(
        _merge_kernel,
        out_shape=(jax.ShapeDtypeStruct((m, d), F32), jax.ShapeDtypeStruct((m, d), BF16)),
        grid=(m // bm,),
        in_specs=[
            pl.BlockSpec((bm, rw), lambda i: (i, 0)),
            pl.BlockSpec((bm, rw), lambda i: (i, 0)),
            pl.BlockSpec((bm, d), lambda i: (i, 0)),
            pl.BlockSpec((bm, d), lambda i: (i, 1)),
            pl.BlockSpec((bm, d), lambda i: (i, 0)),
            _resident(wr.shape), _resident(ws.shape), _resident(wo.shape),
            pl.BlockSpec((1, d), lambda i: (0, 0)),
        ],
        out_specs=(pl.BlockSpec((bm, d), lambda i: (i, 0)), pl.BlockSpec((bm, d), lambda i: (i, 0))),
        compiler_params=pltpu.CompilerParams(
            dimension_semantics=("parallel",),
            vmem_limit_bytes=_vmem_limit(blk + bm * d * 16)),
        name="merge",
    )(yr, os_, gates, gates, x, wr, ws, wo, g_ffn)


def _gelu_tanh(x):
    return 0.5 * x * (1.0 + jnp.tanh(math.sqrt(2.0 / math.pi) * (x + 0.044715 * x * x * x)))


def _ffn_kernel(hn_ref, wg_ref, wv_ref, cw_ref, wd_ref, prev_ref, o_ref, tail_ref, carry_ref,
                *, blocks_per_seq, seq_len):
    i = pl.program_id(0)
    f = pl.program_id(1)

    @pl.when(f == 0)
    def _():
        o_ref[...] = jnp.zeros(o_ref.shape, F32)

    hn = hn_ref[...]
    ug = _dot(hn, wg_ref[...])
    uv = _dot(hn, wv_ref[...])
    bm = ug.shape[0]
    row = _iota2(ug.shape, 0)
    r1 = pltpu.roll(ug, 1, 0)
    r2 = pltpu.roll(ug, 2, 0)
    if seq_len == SUBLANES:
        p2 = prev_ref[...]
        p1 = pltpu.roll(p2, bm - 1, 0)
        t = _mod_pow2(row, SUBLANES)
        s1 = jnp.where(t == 0, p1, r1)
        s2 = jnp.where(t < 2, p2, r2)
        tail_ref[...] = ug
    else:
        first = (i % blocks_per_seq) == 0
        pv = jnp.where(first, prev_ref[0], carry_ref[f])
        head = _iota2((SUBLANES, ug.shape[1]), 0)
        s1 = jnp.concatenate(
            [jnp.where(head == 0, pv[7:8], r1[:SUBLANES]), r1[SUBLANES:]], axis=0)
        s2 = jnp.concatenate(
            [jnp.where(head == 0, pv[6:7], jnp.where(head == 1, pv[7:8], r2[:SUBLANES])),
             r2[SUBLANES:]], axis=0)
        carry_ref[f] = ug[bm - SUBLANES:]
        tail_ref[0] = ug[bm - SUBLANES:]
    conv = cw_ref[0:1, :] * s2 + cw_ref[1:2, :] * s1 + cw_ref[2:3, :] * ug
    act = (_gelu_tanh(conv) * uv).astype(BF16)
    o_ref[...] += _dot(act, wd_ref[...])


def _ffn(hn, w_up, conv_w, w_down, prev8, seq_len):
    m, d = hn.shape
    ff = w_down.shape[0]
    bf = _pick(ff, 512, 2 * LANES) if ff % (2 * LANES) == 0 else _pick(ff, 512, LANES)
    nf = ff // bf
    if seq_len == SUBLANES:
        bm = m
        blocks_per_seq = 1
        prev_spec = pl.BlockSpec((bm, bf), lambda i, f: (i, f))
        tail_shape = (m, ff)
        tail_spec = pl.BlockSpec((bm, bf), lambda i, f: (i, f))
    else:
        bm = _pick(seq_len, 1024, SUBLANES)
        blocks_per_seq = seq_len // bm
        prev_spec = pl.BlockSpec((1, SUBLANES, bf), lambda i, f: (i // blocks_per_seq, 0, f))
        tail_shape = (m // bm, SUBLANES, ff)
        tail_spec = pl.BlockSpec((1, SUBLANES, bf), lambda i, f: (i, 0, f))
    blk = bm * d * (2 + 4) + 3 * d * bf * 2 + 4 * bm * bf * 4
    out, tail = pl.pallas_call(
        functools.partial(_ffn_kernel, blocks_per_seq=blocks_per_seq, seq_len=seq_len),
        out_shape=(jax.ShapeDtypeStruct((m, d), F32), jax.ShapeDtypeStruct(tail_shape, F32)),
        grid=(m // bm, nf),
        in_specs=[
            pl.BlockSpec((bm, d), lambda i, f: (i, 0)),
            pl.BlockSpec((d, bf), lambda i, f: (0, f)),
            pl.BlockSpec((d, bf), lambda i, f: (0, nf + f)),
            pl.BlockSpec((3, bf), lambda i, f: (0, f)),
            pl.BlockSpec((bf, d), lambda i, f: (f, 0)),
            prev_spec,
        ],
        out_specs=(pl.BlockSpec((bm, d), lambda i, f: (i, 0)), tail_spec),
        scratch_shapes=[pltpu.VMEM((nf, SUBLANES, bf), F32)],
        compiler_params=pltpu.CompilerParams(
            dimension_semantics=("arbitrary", "arbitrary"),
            vmem_limit_bytes=_vmem_limit(blk)),
        name="ffn",
    )(hn, w_up, w_up, conv_w, w_down, prev8)
    n_seq = m // seq_len
    if seq_len == SUBLANES:
        return out, tail.reshape(n_seq, SUBLANES, ff)
    return out, tail.reshape(n_seq, blocks_per_seq, SUBLANES, ff)[:, -1]


def _ple_kernel(h_ref, f_ref, pe_ref, wple_ref, wpg_ref, gp_ref, gf_ref, y_ref, *, final_norm):
    h2 = h_ref[...] + f_ref[...]
    hn = _rms(h2, gp_ref[...]).astype(BF16)
    gate = jax.nn.sigmoid(_dot(hn, wpg_ref[...]))
    emb = _dot(pe_ref[...].astype(BF16), wple_ref[...])
    h3 = h2 + emb * gate
    y_ref[...] = _rms(h3, gf_ref[...]) if final_norm else h3


def _ple_out(h, ffn_out, pe, w_ple, w_pg, g_ple, g_final, final_norm):
    m, d = h.shape
    pd = pe.shape[1]
    bm = _pick(m, 512, SUBLANES)
    blk = bm * (3 * d * 4 + pd * 4) + pd * d * 2 + d * d * 2
    return pl.pallas_call(
        functools.partial(_ple_kernel, final_norm=final_norm),
        out_shape=jax.ShapeDtypeStruct((m, d), F32),
        grid=(m // bm,),
        in_specs=[
            pl.BlockSpec((bm, d), lambda i: (i, 0)),
            pl.BlockSpec((bm, d), lambda i: (i, 0)),
            pl.BlockSpec((bm, pd), lambda i: (i, 0)),
            _resident(w_ple.shape), _resident(w_pg.shape),
            pl.BlockSpec((1, d), lambda i: (0, 0)),
            pl.BlockSpec((1, d), lambda i: (0, 0)),
        ],
        out_specs=pl.BlockSpec((bm, d), lambda i: (i, 0)),
        compiler_params=pltpu.CompilerParams(
            dimension_semantics=("parallel",),
            vmem_limit_bytes=_vmem_limit(blk + bm * d * 8)),
        name="ple_out",
    )(h, ffn_out, pe, w_ple, w_pg, g_ple, g_final)


def _prep_weights(g_mix, w_in, mu_shift, w0, w2, a0, a2, g2, k_k, k_a, r_k, ln_x_g, ln_x_b,
                  w_br_r, w_br_s, w_o, g_ffn, w_up, conv_w, w_down, g_ple, w_ple, w_pg, sb_bias):
    d = w_in.shape[0]
    rw = w0.shape[0]
    dl, il, gl = w2.shape[0], a2.shape[0], g2.shape[0]
    dlp, ilp, glp = (_ceil_to(n, LANES) for n in (dl, il, gl))
    sw = w_br_s.shape[0]
    rc = 3 * rw + dl + il + gl
    c_w, c_a, c_g = 3 * rw, 3 * rw + dl, 3 * rw + dl + il

    def regroup(x):
        return jnp.concatenate([
            x[..., :c_w], _pad_to(x[..., c_w:c_a], -1, dlp), _pad_to(x[..., c_a:c_g], -1, ilp),
            _pad_to(x[..., c_g:rc], -1, glp)], axis=-1)

    w_r = regroup(w_in[:, :rc]).astype(BF16)
    scale = HEAD_DIM ** -0.5
    w_q = (w_in[:, rc:rc + sw] * scale).astype(BF16)
    w_k = w_in[:, rc + sw:rc + 2 * sw].astype(BF16)
    w_v = w_in[:, rc + 2 * sw:rc + 3 * sw].astype(BF16)
    w_g = w_in[:, rc + 3 * sw:].astype(BF16)
    nh_s = sw // HEAD_DIM
    return dict(
        d=d, rw=rw, sw=sw, dl=dl, il=il, gl=gl, dlp=dlp, ilp=ilp, glp=glp, rc=rc,
        regroup=regroup,
        g_mix=g_mix.reshape(1, d), w_all=jnp.concatenate([w_r, w_q, w_k, w_v, w_g], axis=1),
        w_kv_t=jnp.transpose(w_in)[rc + sw:rc + 3 * sw].astype(BF16),
        proj_widths=(w_r.shape[1], sw, sw, sw, w_g.shape[1]),
        mu=regroup(mu_shift).reshape(1, -1),
        w0=w0.reshape(1, rw), a0=a0.reshape(1, rw),
        w2=_pad_to(w2, 0, dlp).astype(BF16), a2=_pad_to(a2, 0, ilp).astype(BF16),
        g2=_pad_to(g2, 0, glp).astype(BF16),
        kkw=k_k.reshape(1, rw), kaw=k_a.reshape(1, rw), rkw=r_k.reshape(1, rw),
        lng=ln_x_g.reshape(1, rw), lnb=ln_x_b.reshape(1, rw),
        w_br_r=w_br_r.astype(BF16), w_br_s=w_br_s.astype(BF16), w_o=w_o.astype(BF16),
        g_ffn=g_ffn.reshape(1, d), w_up=w_up.astype(BF16), conv_w=conv_w,
        w_down=w_down.astype(BF16), g_ple=g_ple.reshape(1, d),
        w_ple=w_ple.astype(BF16), w_pg=w_pg.astype(BF16),
        bias2=jnp.broadcast_to(sb_bias.reshape(nh_s // 2, 2, 1), (nh_s // 2, 2, LANES)).astype(F32),
        sb_bias=sb_bias,
    )


def _pages_transposed(cache):
    n_pool, page, nh, hd = cache.shape
    return jnp.transpose(cache, (0, 2, 3, 1)).reshape(n_pool, nh * hd, page)


def _layer(x3, pe3, shift_prev, wkv_prev, conv_prev, past, wp, g_final, final_norm):
    bsz, t, d = x3.shape
    m = bsz * t
    rw, sw = wp["rw"], wp["sw"]
    nh = rw // HEAD_DIM
    x = x3.reshape(m, d)

    kv_t = past is None
    p_r, q, k_s, v_s, gates = _rms_proj(
        x, wp["g_mix"], wp["w_all"], wp["w_kv_t"], wp["proj_widths"], (F32, BF16, F32, F32, BF16),
        (False, False, kv_t, kv_t, False), t)
    nh_s = sw // HEAD_DIM
    if kv_t:
        k_out = jnp.transpose(k_s.reshape(bsz, nh_s, HEAD_DIM, t), (0, 3, 1, 2))
        v_out = jnp.transpose(v_s.reshape(bsz, nh_s, HEAD_DIM, t), (0, 3, 1, 2))
    else:
        k_out = k_s.reshape(bsz, t, nh_s, HEAD_DIM)
        v_out = v_s.reshape(bsz, t, nh_s, HEAD_DIM)

    p3 = p_r.reshape(bsz, t, -1)
    shift = wp["regroup"](shift_prev).reshape(bsz, 1, -1)
    g_c, h_c, rp, yl, gate, bonus = _wkv_intra(
        p3, shift, wp["mu"], wp["w0"], wp["a0"], wp["w2"], wp["a2"], wp["g2"],
        wp["kkw"], wp["kaw"], wp["rkw"], rw=rw, dlp=wp["dlp"], ilp=wp["ilp"], glp=wp["glp"])
    y_r, wkv_new = _wkv_scan(g_c, h_c, rp, yl, gate, bonus, wkv_prev, wp["lng"], wp["lnb"])
    y_r = y_r[:, :t].reshape(m, rw)
    last = p3[:, -1]
    dlp, ilp = wp["dlp"], wp["ilp"]
    c0 = 3 * rw
    shift_new = jnp.concatenate([
        last[:, :c0], last[:, c0:c0 + wp["dl"]], last[:, c0 + dlp:c0 + dlp + wp["il"]],
        last[:, c0 + dlp + ilp:c0 + dlp + ilp + wp["gl"]]], axis=-1)

    if past is None:
        o_s = _attn_prompt(q.reshape(bsz, t, sw), k_s, v_s, wp["bias2"])
    else:
        cache_k, cache_v, page_table = past
        nh_s = sw // HEAD_DIM
        rows_n = nh_s * t
        q3 = q.reshape(bsz, t, sw)
        rr = jnp.arange(rows_n)[:, None] // t
        cc = jnp.arange(sw)[None, :] // HEAD_DIM
        q_bd = jnp.where(rr == cc, jnp.tile(q3, (1, nh_s, 1)), jnp.zeros((), BF16))
        k_new = _pad_to(k_s.reshape(bsz, t, sw).astype(BF16), 1, LANES)
        v_new = _pad_to(v_s.reshape(bsz, t, sw).astype(BF16), 1, LANES)
        bias_rows = jnp.broadcast_to(jnp.repeat(wp["sb_bias"].astype(F32), t)[:, None], (rows_n, LANES))
        o_s = _attn_paged(q_bd, k_new, v_new, bias_rows,
                          _pages_transposed(cache_k), _pages_transposed(cache_v), page_table, t)
    o_s = o_s.reshape(m, sw)

    h, hn = _merge(y_r, o_s, gates, x, wp["w_br_r"], wp["w_br_s"], wp["w_o"], wp["g_ffn"])
    ff = wp["w_down"].shape[0]
    nprev = conv_prev.shape[1]
    pad_rows = jnp.zeros((bsz, SUBLANES - nprev, ff), F32)
    if t == SUBLANES:
        prev8 = jnp.concatenate([conv_prev, pad_rows], axis=1).reshape(m, ff)
    else:
        prev8 = jnp.concatenate([pad_rows, conv_prev], axis=1)
    ffn_out, tail = _ffn(hn, wp["w_up"], wp["conv_w"], wp["w_down"], prev8, t)
    conv_new = tail[:, SUBLANES - nprev:]
    y = _ple_out(h, ffn_out, pe3.reshape(m, -1), wp["w_ple"], wp["w_pg"], wp["g_ple"],
                 g_final.reshape(1, d), final_norm)
    return (y.reshape(bsz, t, d), shift_new, wkv_new, conv_new,
            k_out, v_out)


def kernel(x_prompt, x_sample, state_shift, state_wkv, state_conv, cache_k, cache_v, page_table, p_prompt, p_sample, g_mix, w_in, sb_bias, mu_shift, w0, w2, a0, a2, g2, k_k, k_a, r_k, ln_x_g, ln_x_b, w_br_r, w_br_s, w_o, g_ffn, w_up, conv_w, w_down, g_ple, w_ple, w_pg, g_final):
    depth = w_in.shape[0]
    bsz = x_prompt.shape[0]
    rw = w0.shape[1]
    nh = rw // HEAD_DIM
    ff = w_down.shape[1]
    rc = state_shift.shape[-1]
    nprev = state_conv.shape[2]
    h_p, h_s = x_prompt, x_sample
    outs_p = [[] for _ in range(5)]
    outs_s = [[] for _ in range(5)]
    for i in range(depth):
        wp = _prep_weights(g_mix[i], w_in[i], mu_shift[i], w0[i], w2[i], a0[i], a2[i], g2[i], k_k[i],
                           k_a[i], r_k[i], ln_x_g[i], ln_x_b[i], w_br_r[i], w_br_s[i], w_o[i],
                           g_ffn[i], w_up[i], conv_w[i], w_down[i], g_ple[i], w_ple[i], w_pg[i],
                           sb_bias[i])
        last = i == depth - 1
        res_p = _layer(h_p, p_prompt[i], jnp.zeros((bsz, rc), F32),
                       jnp.zeros((bsz, nh, HEAD_DIM, HEAD_DIM), F32),
                       jnp.zeros((bsz, nprev, ff), F32), None, wp, g_final, last)
        res_s = _layer(h_s, p_sample[i], state_shift[i], state_wkv[i], state_conv[i],
                       (cache_k[i], cache_v[i], page_table), wp, g_final, last)
        h_p, h_s = res_p[0], res_s[0]
        for dst, res in ((outs_p, res_p), (outs_s, res_s)):
            for lst, val in zip(dst, res[1:]):
                lst.append(val)
    return (h_p, h_s, *(jnp.stack(o) for o in outs_p), *(jnp.stack(o) for o in outs_s))
```

```python
import functools
import math

import jax
import jax.numpy as jnp
from jax import lax
from jax.experimental import pallas as pl
from jax.experimental.pallas import tpu as pltpu

F32 = jnp.float32
BF16 = jnp.bfloat16

LANES = 128
SUBLANES = 8
V7X_SCOPED_VMEM_BYTES = 60000 * 1024

HEAD_DIM = 64
CHUNK = 128
INTRA_PAIRS_PER_STEP = 8
SCAN_PAIRS_PER_STEP = 4
DECAY_SCALE = 0.606531
NORM_EPS = 1e-6
GN_EPS = HEAD_DIM * 1e-5
KK_EPS = 1e-24


def _vmem_limit(block_bytes):
    return int(min(V7X_SCOPED_VMEM_BYTES, 2 * block_bytes + (16 << 20)))


def _pick(n, pref, align):
    if n <= pref:
        return n
    best = None
    for d in range(align, pref + 1, align):
        if n % d == 0:
            best = d
    assert best is not None, (n, pref, align)
    return best


def _pad_to(x, axis, size):
    pad = size - x.shape[axis]
    if pad == 0:
        return x
    widths = [(0, 0)] * x.ndim
    widths[axis] = (0, pad)
    return jnp.pad(x, widths)


def _ceil_to(n, m):
    return -(-n // m) * m


def _dot(a, b):
    return jnp.dot(a, b, preferred_element_type=F32)


def _dot_nt(a, b):
    return lax.dot_general(a, b, (((1,), (1,)), ((), ())), preferred_element_type=F32)


def _dot_tn(a, b):
    return lax.dot_general(a, b, (((0,), (0,)), ((), ())), preferred_element_type=F32)


def _split2(x):
    hi = x.astype(BF16)
    lo = (x - hi.astype(F32)).astype(BF16)
    return hi, lo


def _split3(x):
    hi = x.astype(BF16)
    r1 = x - hi.astype(F32)
    mid = r1.astype(BF16)
    lo = (r1 - mid.astype(F32)).astype(BF16)
    return hi, mid, lo


def _iota2(shape, dim):
    return lax.broadcasted_iota(jnp.int32, shape, dim)


def _div_pow2(x, n):
    assert n & (n - 1) == 0, n
    return x >> (n.bit_length() - 1)


def _mod_pow2(x, n):
    assert n & (n - 1) == 0, n
    return x & (n - 1)


def _head_sum(x, e2):
    hi, lo = _split2(x)
    return _dot(jnp.concatenate([hi, lo], axis=1), e2)


def _same_head_matrix():
    r = _mod_pow2(_iota2((2 * LANES, LANES), 0), LANES)
    c = _iota2((2 * LANES, LANES), 1)
    return jnp.where(_div_pow2(r, HEAD_DIM) == _div_pow2(c, HEAD_DIM), 1.0, 0.0).astype(BF16)


def _rms_proj_kernel(x_ref, g_ref, w_ref, wt_ref, *refs, bounds, transposed):
    o_refs, xn_ref = refs[:-1], refs[-1]
    j = pl.program_id(1)

    @pl.when(j == 0)
    def _():
        x = x_ref[...]
        ms = jnp.mean(x * x, axis=-1, keepdims=True)
        xn_ref[...] = (x * lax.rsqrt(ms + NORM_EPS) * g_ref[...]).astype(BF16)

    for o_ref, (lo, hi), tr in zip(o_refs, bounds, transposed):
        @pl.when((j >= lo) & (j < hi))
        def _(o_ref=o_ref, tr=tr):
            if tr:
                o_ref[0] = _dot_nt(wt_ref[...], xn_ref[...]).astype(o_ref.dtype)
            else:
                o_ref[...] = _dot(xn_ref[...], w_ref[...]).astype(o_ref.dtype)


def _rms_proj(x, g, w, w_t, widths, dtypes, transposed, seq_len):
    m, d = x.shape
    assert w.shape[1] == sum(widths)
    bm = _pick(seq_len, 1024, SUBLANES) if any(transposed) else _pick(m, 1024, SUBLANES)
    bps = seq_len // bm
    bn = _pick(math.gcd(*widths), 512, LANES)
    bounds, lo = [], 0
    for wd in widths:
        bounds.append((lo, lo + wd // bn))
        lo += wd // bn
    t_ranges = [b for b, tr in zip(bounds, transposed) if tr]
    t_lo = t_ranges[0][0] if t_ranges else 0
    t_hi = t_ranges[-1][1] if t_ranges else 0
    assert sum(hi - a for a, hi in t_ranges) == t_hi - t_lo
    if not t_ranges:
        w_t = w_t[:bn]
    assert w_t.shape == (max(t_hi - t_lo, 1) * bn, d)

    def w_index(i, j):
        return (0, jnp.where((j >= t_lo) & (j < t_hi), max(t_lo - 1, 0), j))

    def out_spec(lo, hi, tr):
        if tr:
            return pl.BlockSpec((1, bn, bm),
                                lambda i, j: (i // bps, jnp.clip(j - lo, 0, hi - lo - 1), i % bps))
        return pl.BlockSpec((bm, bn), lambda i, j: (i, jnp.clip(j - lo, 0, hi - lo - 1)))

    out_shape = tuple(
        jax.ShapeDtypeStruct((m // seq_len, wd, seq_len) if tr else (m, wd), t)
        for wd, t, tr in zip(widths, dtypes, transposed))
    blk = (bm * d * 4 + 2 * d * bn * 2 + bm * d * 2
           + sum(bm * bn * jnp.dtype(t).itemsize for t in dtypes))
    return pl.pallas_call(
        functools.partial(_rms_proj_kernel, bounds=tuple(bounds), transposed=tuple(transposed)),
        out_shape=out_shape,
        grid=(m // bm, lo),
        in_specs=[
            pl.BlockSpec((bm, d), lambda i, j: (i, 0)),
            pl.BlockSpec((1, d), lambda i, j: (0, 0)),
            pl.BlockSpec((d, bn), w_index),
            pl.BlockSpec((bn, d), lambda i, j: (jnp.clip(j - t_lo, 0, max(t_hi - t_lo, 1) - 1), 0)),
        ],
        out_specs=tuple(out_spec(a, b, tr) for (a, b), tr in zip(bounds, transposed)),
        scratch_shapes=[pltpu.VMEM((bm, d), BF16)],
        compiler_params=pltpu.CompilerParams(
            dimension_semantics=("parallel", "arbitrary"),
            vmem_limit_bytes=_vmem_limit(blk)),
        name="rms_proj",
    )(x, g, w, w_t)


def _wkv_intra_kernel(r_ref, k_ref, v_ref, xw_ref, xa_ref, xg_ref,
                      rp_ref, kp_ref, vp_ref, xwp_ref, xap_ref, xgp_ref,
                      rs_ref, ks_ref, vs_ref, xws_ref, xas_ref, xgs_ref,
                      mur_ref, muk_ref, muv_ref, muw_ref, mua_ref, mug_ref,
                      w0_ref, a0_ref, w2_ref, a2_ref, g2_ref,
                      kkw_ref, kaw_ref, rkw_ref,
                      g_out, h_out, rp_out, yl_out, gate_out, bonus_out,
                      *, t_real, n_pairs):
    c_idx = pl.program_id(1)
    C = CHUNK
    first = c_idx == 0

    def mixed(x_ref, p_ref, s_ref, mu_ref):
        x = x_ref[0]
        prev_last = jnp.where(first, s_ref[0], p_ref[0][SUBLANES - 1:SUBLANES])
        rolled = pltpu.roll(x, 1, 0)
        prev = jnp.where(_iota2(x.shape, 0) == 0, prev_last, rolled)
        y = x + (prev - x) * mu_ref[...]
        if t_real < C:
            y = jnp.concatenate([y, jnp.zeros((C - t_real, y.shape[1]), F32)], axis=0)
        return y

    r_all = mixed(r_ref, rp_ref, rs_ref, mur_ref)
    k_all = mixed(k_ref, kp_ref, ks_ref, muk_ref)
    v_all = mixed(v_ref, vp_ref, vs_ref, muv_ref)
    xw = jnp.tanh(mixed(xw_ref, xwp_ref, xws_ref, muw_ref)).astype(BF16)
    xa = mixed(xa_ref, xap_ref, xas_ref, mua_ref).astype(BF16)
    xg = jax.nn.sigmoid(mixed(xg_ref, xgp_ref, xgs_ref, mug_ref)).astype(BF16)

    lw_all = -DECAY_SCALE * jax.nn.sigmoid(w0_ref[...] + _dot(xw, w2_ref[...]))
    if t_real < C:
        lw_all = jnp.where(_iota2(lw_all.shape, 0) < t_real, lw_all, 0.0)
    iclr_all = jax.nn.sigmoid(a0_ref[...] + _dot(xa, a2_ref[...]))
    gate_all = _dot(xg, g2_ref[...])
    gate_out[0] = gate_all.astype(gate_out.dtype)

    e2 = _same_head_matrix()
    row = _iota2((C, C), 0)
    col = _iota2((C, C), 1)
    strict_lower = row > col
    lower = row >= col
    l_incl = jnp.where(lower, 1.0, 0.0).astype(BF16)
    l3 = jnp.concatenate([l_incl, l_incl, l_incl], axis=1)
    lane = _iota2((C, LANES), 1)
    head_masks = (lane < HEAD_DIM, lane >= HEAD_DIM)
    rr = _iota2((LANES, LANES), 0)
    cc = _iota2((LANES, LANES), 1)
    block_mask = _div_pow2(rr, HEAD_DIM) == _div_pow2(cc, HEAD_DIM)
    eye_mask = rr == cc

    pairs = range(n_pairs)
    sls = [slice(p * LANES, (p + 1) * LANES) for p in pairs]
    r = [r_all[:, sl] for sl in sls]
    k = [k_all[:, sl] for sl in sls]
    v = [v_all[:, sl] for sl in sls]
    lw = [lw_all[:, sl] for sl in sls]
    iclr = [iclr_all[:, sl] for sl in sls]

    kkr = [k[p] * kkw_ref[:, sls[p]] for p in pairs]
    k_mod = [k[p] * (1.0 + (iclr[p] - 1.0) * kaw_ref[:, sls[p]]) for p in pairs]
    kk_ss = [_head_sum(kkr[p] * kkr[p], e2) for p in pairs]
    rk_sum = [_head_sum(r[p] * k_mod[p] * rkw_ref[:, sls[p]], e2) for p in pairs]
    cum = [_dot(l3, jnp.concatenate(_split3(lw[p]), axis=0)) for p in pairs]

    vb, em, ecl, kbar, bbar, rhs_scores, lhs_scores, am_b, rm_f = [], [], [], [], [], [], [], [], []
    for p in pairs:
        bonus_out[0, :, sls[p]] = rk_sum[p] * v[p]
        kk = kkr[p] * lax.rsqrt(jnp.maximum(kk_ss[p], KK_EPS))
        b = kk * iclr[p]
        m_row = cum[p][C // 2 - 1:C // 2]
        cum_last = cum[p][C - 1:C]
        g = cum[p] - m_row
        eng = jnp.exp(-g)
        ebar = jnp.exp(cum_last - cum[p])
        at = -kk * jnp.exp(g - lw[p])
        rt = r[p] * jnp.exp(g)
        kbar.append((k_mod[p] * ebar).astype(BF16))
        bbar.append((b * ebar).astype(BF16))
        em.append(jnp.exp(m_row))
        ecl.append(jnp.exp(cum_last))
        vb.append(v[p].astype(BF16))
        rhs_scores.append(jnp.concatenate([(k_mod[p] * eng).astype(BF16), (b * eng).astype(BF16)], axis=0))
        stack = []
        for mh in head_masks:
            am_b.append(jnp.where(mh, at, 0.0).astype(BF16))
            rm_f.append(jnp.where(mh, rt, 0.0))
            stack += [am_b[-1], rm_f[-1].astype(BF16)]
        lhs_scores.append(jnp.concatenate(stack, axis=0))

    sc = [_dot_nt(lhs_scores[p], rhs_scores[p]) for p in pairs]
    heads = range(2 * n_pairs)
    m_ak, m_ab, p_rk_b, p_rb_b = [], [], [], []
    for i in heads:
        s = sc[i // 2][(i % 2) * 2 * C:(i % 2 + 1) * 2 * C]
        m_ak.append(jnp.where(strict_lower, s[:C, :C], 0.0).astype(BF16))
        m_ab.append(jnp.where(strict_lower, s[:C, C:], 0.0))
        p_rk_b.append(jnp.where(lower, s[C:, :C], 0.0).astype(BF16))
        p_rb_b.append(jnp.where(lower, s[C:, C:], 0.0).astype(BF16))

    eye = jnp.where(row == col, 1.0, 0.0).astype(F32)
    t = [eye + m_ab[i] for i in heads]
    qb = [m_ab[i].astype(BF16) for i in heads]
    q = [_dot(qb[i], qb[i]) for i in heads]
    w1 = [_dot(m_ak[i], vb[i // 2]).astype(BF16) for i in heads]
    for _ in range(int(math.log2(C)) - 2):
        qb = [q[i].astype(BF16) for i in heads]
        st = [_dot(jnp.concatenate([t[i].astype(BF16), qb[i]], axis=0), qb[i]) for i in heads]
        t = [t[i] + st[i][:C] for i in heads]
        q = [st[i][C:] for i in heads]
    corr = [_dot(t[i].astype(BF16), q[i].astype(BF16)) for i in heads]
    t_inv = [(t[i] + corr[i]).astype(BF16) for i in heads]
    tu = [_dot(t_inv[i], jnp.concatenate([w1[i], am_b[i]], axis=1)) for i in heads]
    ul = [tu[i][:, :LANES] for i in heads]
    ap = [tu[i][:, LANES:] for i in heads]
    yl = [_dot(jnp.concatenate([p_rk_b[i], p_rb_b[i]], axis=1),
               jnp.concatenate([vb[i // 2], ul[i].astype(BF16)], axis=0)) for i in heads]
    rp_add = [_dot(p_rb_b[i], ap[i].astype(BF16)) for i in heads]

    m1 = head_masks[1]
    ul_pair = [jnp.where(m1, ul[2 * p + 1], ul[2 * p]).astype(BF16) for p in pairs]
    ap_pair = [((ap[2 * p] + ap[2 * p + 1]) * em[p]).astype(BF16) for p in pairs]
    ab = [_dot_tn(ap_pair[p], bbar[p]) for p in pairs]
    hh = [_dot_tn(jnp.concatenate([vb[p], ul_pair[p]], axis=0),
                  jnp.concatenate([kbar[p], bbar[p]], axis=0)) for p in pairs]
    for p in pairs:
        sl = sls[p]
        g_out[0, 0, :, sl] = jnp.where(block_mask, ab[p], 0.0) + jnp.where(eye_mask, ecl[p], 0.0)
        h_out[0, 0, :, sl] = jnp.where(block_mask, hh[p], 0.0)
        rp_pair = rm_f[2 * p] + rp_add[2 * p] + rm_f[2 * p + 1] + rp_add[2 * p + 1]
        rp_out[0, :, sl] = (rp_pair * em[p]).astype(rp_out.dtype)
        yl_out[0, :, sl] = jnp.where(m1, yl[2 * p + 1], yl[2 * p])


def _wkv_intra(p3, shift, mu, w0, a0, w2, a2, g2, kkw, kaw, rkw, *, rw, dlp, ilp, glp):
    bsz, t, npc = p3.shape
    C = CHUNK
    t_real = min(t, C)
    assert t % t_real == 0 and t_real % SUBLANES == 0
    nch = t // t_real
    tp = nch * C
    n_pairs = _pick(rw // LANES, INTRA_PAIRS_PER_STEP, 1)
    lw_ = n_pairs * LANES
    ng = rw // lw_
    o_w, o_a, o_g = 3 * rw, 3 * rw + dlp, 3 * rw + dlp + ilp
    assert o_w % dlp == 0 and o_a % ilp == 0 and o_g % glp == 0
    rows_prev = t_real // SUBLANES

    def cur(width, off):
        return pl.BlockSpec((1, t_real, width), lambda b, c, g, o=off // width: (b, c, o))

    def cur_g(width, off):
        return pl.BlockSpec((1, t_real, width), lambda b, c, g, o=off // width: (b, c, o + g))

    def prev(width, off):
        return pl.BlockSpec((1, SUBLANES, width),
                            lambda b, c, g, o=off // width: (b, jnp.maximum(c * rows_prev - 1, 0), o))

    def prev_g(width, off):
        return pl.BlockSpec((1, SUBLANES, width),
                            lambda b, c, g, o=off // width: (b, jnp.maximum(c * rows_prev - 1, 0), o + g))

    def sh(width, off):
        return pl.BlockSpec((1, 1, width), lambda b, c, g, o=off // width: (b, 0, o))

    def sh_g(width, off):
        return pl.BlockSpec((1, 1, width), lambda b, c, g, o=off // width: (b, 0, o + g))

    def vec(width, off):
        return pl.BlockSpec((1, width), lambda b, c, g, o=off // width: (0, o))

    def vec_g(width, off=0):
        return pl.BlockSpec((1, width), lambda b, c, g, o=off // width: (0, o + g))

    def mat_g(rows):
        return pl.BlockSpec((rows, lw_), lambda b, c, g: (0, g))

    in_specs = (
        [cur_g(lw_, 0), cur_g(lw_, rw), cur_g(lw_, 2 * rw), cur(dlp, o_w), cur(ilp, o_a), cur(glp, o_g)]
        + [prev_g(lw_, 0), prev_g(lw_, rw), prev_g(lw_, 2 * rw), prev(dlp, o_w), prev(ilp, o_a), prev(glp, o_g)]
        + [sh_g(lw_, 0), sh_g(lw_, rw), sh_g(lw_, 2 * rw), sh(dlp, o_w), sh(ilp, o_a), sh(glp, o_g)]
        + [vec_g(lw_, 0), vec_g(lw_, rw), vec_g(lw_, 2 * rw), vec(dlp, o_w), vec(ilp, o_a), vec(glp, o_g)]
        + [vec_g(lw_), vec_g(lw_), mat_g(dlp), mat_g(ilp), mat_g(glp)]
        + [vec_g(lw_), vec_g(lw_), vec_g(lw_)]
    )
    args = ([p3] * 6 + [p3] * 6 + [shift] * 6 + [mu] * 6 + [w0, a0, w2, a2, g2, kkw, kaw, rkw])
    out_shape = (
        jax.ShapeDtypeStruct((bsz, nch, LANES, rw), F32),
        jax.ShapeDtypeStruct((bsz, nch, LANES, rw), F32),
        jax.ShapeDtypeStruct((bsz, tp, rw), BF16),
        jax.ShapeDtypeStruct((bsz, tp, rw), F32),
        jax.ShapeDtypeStruct((bsz, tp, rw), BF16),
        jax.ShapeDtypeStruct((bsz, tp, rw), F32),
    )
    gh_spec = pl.BlockSpec((1, 1, LANES, lw_), lambda b, c, g: (b, c, 0, g))
    tok_spec = pl.BlockSpec((1, C, lw_), lambda b, c, g: (b, c, g))
    blk = (t_real * (3 * lw_ + dlp + ilp + glp) * 4 + (dlp + ilp + glp) * lw_ * 2
           + 2 * LANES * lw_ * 4 + C * lw_ * 12)
    return pl.pallas_call(
        functools.partial(_wkv_intra_kernel, t_real=t_real, n_pairs=n_pairs),
        out_shape=out_shape,
        grid=(bsz, nch, ng),
        in_specs=in_specs,
        out_specs=(gh_spec, gh_spec, tok_spec, tok_spec, tok_spec, tok_spec),
        compiler_params=pltpu.CompilerParams(
            dimension_semantics=("parallel", "parallel", "parallel"),
            vmem_limit_bytes=_vmem_limit(blk)),
        name="wkv_intra",
    )(*args)


def _wkv_scan_kernel(g_ref, h_ref, rp_ref, yl_ref, gate_ref, bonus_ref, s0_ref, lng_ref, lnb_ref,
                     y_ref, s_out_ref, st_ref, *, n_chunks, n_pairs):
    C = CHUNK
    N = HEAD_DIM
    e2 = _same_head_matrix()
    inv_n = 1.0 / HEAD_DIM
    lanes = [slice(p * LANES, (p + 1) * LANES) for p in range(n_pairs)]

    zero = jnp.zeros((N, N), F32)
    for p, sl in enumerate(lanes):
        st_ref[:, sl] = jnp.concatenate(
            [jnp.concatenate([s0_ref[0, 2 * p], zero], axis=1),
             jnp.concatenate([zero, s0_ref[0, 2 * p + 1]], axis=1)], axis=0)

    def body(c, carry):
        rows = pl.ds(pl.multiple_of(c * C, C), C)
        sb = [st_ref[:, sl].astype(BF16) for sl in lanes]
        y = [_dot_nt(rp_ref[0, rows, sl], s) for sl, s in zip(lanes, sb)]
        s_new = [_dot(s, g_ref[0, c, :, sl].astype(BF16)) for sl, s in zip(lanes, sb)]
        for sl, s in zip(lanes, s_new):
            st_ref[:, sl] = s + h_ref[0, c, :, sl]
        y = [yy + yl_ref[0, rows, sl] for sl, yy in zip(lanes, y)]
        mu = [_head_sum(yy, e2) * inv_n for yy in y]
        yc = [yy - m for yy, m in zip(y, mu)]
        var = [_head_sum(x * x, e2) * inv_n for x in yc]
        for sl, x, vv in zip(lanes, yc, var):
            yn = x * lax.rsqrt(vv + GN_EPS) * lng_ref[:, sl] + lnb_ref[:, sl]
            out = (yn + bonus_ref[0, rows, sl]) * gate_ref[0, rows, sl].astype(F32)
            y_ref[0, rows, sl] = out.astype(y_ref.dtype)
        return carry

    lax.fori_loop(0, n_chunks, body, 0)
    for p, sl in enumerate(lanes):
        s = st_ref[:, sl]
        s_out_ref[0, 2 * p] = s[:N, :N]
        s_out_ref[0, 2 * p + 1] = s[N:, N:]


def _wkv_scan(g, h, rp, yl, gate, bonus, s0, lng, lnb):
    bsz, nch, _, rw = g.shape
    tp = rp.shape[1]
    n_pairs = _pick(rw // LANES, SCAN_PAIRS_PER_STEP, 1)
    lw_ = n_pairs * LANES
    gh_spec = pl.BlockSpec((1, nch, LANES, lw_), lambda b, q: (b, 0, 0, q))
    tok_spec = pl.BlockSpec((1, tp, lw_), lambda b, q: (b, 0, q))
    st_spec = pl.BlockSpec((1, 2 * n_pairs, HEAD_DIM, HEAD_DIM), lambda b, q: (b, q, 0, 0))
    vec_spec = pl.BlockSpec((1, lw_), lambda b, q: (0, q))
    blk = 2 * nch * LANES * lw_ * 4 + tp * lw_ * (2 + 4 + 2 + 4 + 2) + 4 * LANES * lw_ * 4
    return pl.pallas_call(
        functools.partial(_wkv_scan_kernel, n_chunks=nch, n_pairs=n_pairs),
        out_shape=(jax.ShapeDtypeStruct((bsz, tp, rw), BF16),
                   jax.ShapeDtypeStruct(s0.shape, F32)),
        grid=(bsz, rw // lw_),
        in_specs=[gh_spec, gh_spec, tok_spec, tok_spec, tok_spec, tok_spec, st_spec, vec_spec, vec_spec],
        out_specs=(tok_spec, st_spec),
        scratch_shapes=[pltpu.VMEM((LANES, lw_), F32)],
        compiler_params=pltpu.CompilerParams(
            dimension_semantics=("parallel", "parallel"),
            vmem_limit_bytes=_vmem_limit(blk)),
        name="wkv_scan",
    )(g, h, rp, yl, gate, bonus, s0, lng, lnb)


def _cumsum_rhs():
    r = _iota2((LANES, 2 * LANES), 0)
    c = _iota2((LANES, 2 * LANES), 1)
    return jnp.where((c >= LANES) | (r > c), -1.0, 0.0).astype(BF16)


def _keep_sums(sp, w2):
    return _dot(sp.astype(BF16), w2)


SOFTPLUS_LINEAR_ABOVE = 30.0


def _softplus(z, mask=None):
    sp = jnp.log(1.0 + jnp.exp(jnp.minimum(z, SOFTPLUS_LINEAR_ABOVE)))
    sp = jnp.where(z > SOFTPLUS_LINEAR_ABOVE, z, sp)
    return sp if mask is None else jnp.where(mask, sp, 0.0)


def _sb_tile(z, v_tile, run, acc, w2, mask):
    sp = _softplus(z, mask)
    cs2 = _keep_sums(sp, w2)
    att = jnp.exp((z - sp) + cs2[:, :LANES] + run)
    if mask is not None:
        att = jnp.where(mask, att, 0.0)
    acc = acc + _dot(att.astype(BF16), v_tile)
    return run + cs2[:, LANES:], acc


def _attn_prompt_kernel(q_ref, k_ref, v_ref, bias_ref, o_ref, kb_ref, vb_ref, run_ref, acc_ref):
    qi = pl.program_id(2)
    qb = q_ref.shape[1]
    kb = 2 * LANES
    n_diag = qb // kb
    assert qb == n_diag * kb and n_diag in (1, 2)

    @pl.when(qi == 0)
    def _():
        for c in range(kb_ref.shape[0]):
            kb_ref[c] = k_ref[0, :, c * LANES:(c + 1) * LANES].astype(BF16)
            vb_ref[c] = v_ref[0, :, c * LANES:(c + 1) * LANES].astype(BF16)

    q = q_ref[0]
    m0 = _iota2((qb, LANES), 1) < HEAD_DIM
    zero = jnp.zeros((), BF16)
    qs = jnp.concatenate([jnp.where(m0, q, zero), jnp.where(m0, zero, q)], axis=0)
    w2 = _cumsum_rhs()
    run_ref[...] = jnp.zeros(run_ref.shape, F32)
    acc_ref[...] = jnp.zeros(acc_ref.shape, F32)

    def step(j, n_blocks, mask):
        nk = 2 * n_blocks
        t0 = 2 * j
        z = _dot(qs, jnp.concatenate([kb_ref[t0 + c] for c in range(nk)], axis=1))
        b0 = jnp.concatenate([bias_ref[0, 0:1, :]] * nk, axis=1)
        b1 = jnp.concatenate([bias_ref[0, 1:2, :]] * nk, axis=1)
        z = jnp.concatenate([z[:qb] + b0, z[qb:] + b1], axis=0)
        sp = _softplus(z, mask)
        tiles = [slice(c * LANES, (c + 1) * LANES) for c in range(nk)]
        cs = [_keep_sums(sp[:, c], w2) for c in tiles]
        zs = z - sp
        run = run_ref[...]
        e = [None] * nk
        for c in reversed(range(nk)):
            e[c] = zs[:, tiles[c]] + cs[c][:, :LANES] + run
            run = run + cs[c][:, LANES:]
        att = jnp.exp(jnp.concatenate(e, axis=1))
        if mask is not None:
            att = jnp.where(mask, att, 0.0)
        vt = jnp.concatenate([vb_ref[t0 + c] for c in range(nk)], axis=1)
        acc_ref[...] += _dot_nt(att.astype(BF16), vt)
        run_ref[...] = run

    qpos = _mod_pow2(_iota2((2 * qb, qb), 0), qb)
    step(qi * n_diag, n_diag, _iota2((2 * qb, qb), 1) < qpos)

    n_off = qi * n_diag

    def body(jj, carry):
        step(n_off - 2 - 2 * jj, 2, None)
        return carry

    lax.fori_loop(0, n_off // 2, body, 0)

    if n_diag % 2 == 1:
        @pl.when(n_off % 2 == 1)
        def _():
            step(0, 1, None)

    o_ref[0] = jnp.where(m0, acc_ref[:qb], acc_ref[qb:]).astype(o_ref.dtype)


def _attn_prompt(q, k, v, bias2):
    bsz, t, w = q.shape
    qb = 4 * LANES if t % (4 * LANES) == 0 else 2 * LANES
    assert t % qb == 0
    blk = qb * LANES * 2 * 2 + 2 * t * LANES * 4
    return pl.pallas_call(
        _attn_prompt_kernel,
        out_shape=jax.ShapeDtypeStruct((bsz, t, w), BF16),
        grid=(bsz, w // LANES, t // qb),
        in_specs=[
            pl.BlockSpec((1, qb, LANES), lambda b, p, i: (b, i, p)),
            pl.BlockSpec((1, LANES, t), lambda b, p, i: (b, p, 0)),
            pl.BlockSpec((1, LANES, t), lambda b, p, i: (b, p, 0)),
            pl.BlockSpec((1, 2, LANES), lambda b, p, i: (p, 0, 0)),
        ],
        out_specs=pl.BlockSpec((1, qb, LANES), lambda b, p, i: (b, i, p)),
        scratch_shapes=[pltpu.VMEM((t // LANES, LANES, LANES), BF16),
                        pltpu.VMEM((t // LANES, LANES, LANES), BF16),
                        pltpu.VMEM((2 * qb, LANES), F32), pltpu.VMEM((2 * qb, LANES), F32)],
        compiler_params=pltpu.CompilerParams(
            dimension_semantics=("parallel", "parallel", "arbitrary"),
            vmem_limit_bytes=_vmem_limit(blk + 2 * t * LANES * 2)),
        name="attn_prompt",
    )(q, k, v, bias2)


PAGES_PER_STEP = 16


def _attn_paged_kernel(pt_ref, *refs, n_groups, pps):
    del pt_ref
    k_refs = refs[:pps]
    v_refs = refs[pps:2 * pps]
    q_ref, kn_ref, vn_ref, bias_ref, o_ref, run_ref, acc_ref = refs[2 * pps:]
    g = pl.program_id(1)
    rows_n = q_ref.shape[1]
    t_new = o_ref.shape[1]
    w2 = _cumsum_rhs()
    q = q_ref[0]
    bias = bias_ref[...]

    @pl.when(g == 0)
    def _():
        lane = _iota2((rows_n, LANES), 1)
        row = _iota2((rows_n, LANES), 0)
        mask = lane < _mod_pow2(row, t_new)
        z = _dot_nt(q, kn_ref[0]) + bias
        run, acc = _sb_tile(z, vn_ref[0], jnp.zeros((rows_n, LANES), F32),
                            jnp.zeros(acc_ref.shape, F32), w2, mask)
        run_ref[...] = run
        acc_ref[...] = acc

    pages = range(pps)
    kt = [k_refs[p][0].astype(BF16) for p in pages]
    z2 = [_dot(q, jnp.concatenate(kt[i:i + 2], axis=1)) for i in range(0, pps, 2)]
    z = [z2[p // 2][:, (p % 2) * LANES:(p % 2 + 1) * LANES] + bias for p in pages]
    sp = [_softplus(zz) for zz in z]
    cs2 = [_keep_sums(x, w2) for x in sp]
    run = run_ref[...]
    att = []
    for p in pages:
        att.append(jnp.exp((z[p] - sp[p]) + cs2[p][:, :LANES] + run).astype(BF16))
        run = run + cs2[p][:, LANES:]
    run_ref[...] = run
    vt = jnp.concatenate([v_refs[p][0].astype(BF16) for p in pages], axis=1)
    acc_ref[...] += _dot_nt(jnp.concatenate(att, axis=1), vt)

    @pl.when(g == n_groups - 1)
    def _():
        acc = acc_ref[...]
        rr = _iota2(acc.shape, 0)
        cc = _iota2(acc.shape, 1)
        picked = jnp.where(_div_pow2(rr, t_new) == _div_pow2(cc, HEAD_DIM), acc, 0.0)
        out = picked[0:t_new]
        for h in range(1, rows_n // t_new):
            out = out + picked[h * t_new:(h + 1) * t_new]
        o_ref[0] = out.astype(o_ref.dtype)


def _attn_paged(q_bd, k_new, v_new, bias_rows, cache_k, cache_v, page_table, t_new):
    bsz, rows_n, w = q_bd.shape
    n_pages = page_table.shape[1]
    page = cache_k.shape[2]
    pps = _pick(n_pages, PAGES_PER_STEP, 2)
    assert page == LANES and cache_k.shape[1] == w
    n_groups = n_pages // pps

    def page_spec(p):
        return pl.BlockSpec((1, w, page),
                            lambda b, g, pt, p=p % pps: (pt[b, n_pages - 1 - (g * pps + p)], 0, 0))

    in_specs = ([page_spec(p) for p in range(2 * pps)] + [
        pl.BlockSpec((1, rows_n, w), lambda b, g, pt: (b, 0, 0)),
        pl.BlockSpec((1, LANES, w), lambda b, g, pt: (b, 0, 0)),
        pl.BlockSpec((1, LANES, w), lambda b, g, pt: (b, 0, 0)),
        pl.BlockSpec((rows_n, LANES), lambda b, g, pt: (0, 0)),
    ])
    blk = 2 * pps * page * w * 4 + rows_n * w * 2 + 2 * LANES * w * 2 + rows_n * w * 4
    return pl.pallas_call(
        functools.partial(_attn_paged_kernel, n_groups=n_groups, pps=pps),
        out_shape=jax.ShapeDtypeStruct((bsz, t_new, w), BF16),
        grid_spec=pltpu.PrefetchScalarGridSpec(
            num_scalar_prefetch=1,
            grid=(bsz, n_groups),
            in_specs=in_specs,
            out_specs=pl.BlockSpec((1, t_new, w), lambda b, g, pt: (b, 0, 0)),
            scratch_shapes=[pltpu.VMEM((rows_n, LANES), F32), pltpu.VMEM((rows_n, w), F32)],
        ),
        compiler_params=pltpu.CompilerParams(
            dimension_semantics=("parallel", "arbitrary"),
            vmem_limit_bytes=_vmem_limit(blk)),
        name="attn_paged",
    )(page_table, *([cache_k] * pps), *([cache_v] * pps), q_bd, k_new, v_new, bias_rows)


def _rms(x, g):
    ms = jnp.mean(x * x, axis=-1, keepdims=True)
    return x * lax.rsqrt(ms + NORM_EPS) * g


def _merge_kernel(yr_ref, os_ref, gr_ref, gs_ref, x_ref, wr_ref, ws_ref, wo_ref, g_ref, h_ref, hn_ref):
    a = _dot(yr_ref[...], wr_ref[...])
    b = _dot(os_ref[...], ws_ref[...])
    mixed = (jax.nn.sigmoid(gr_ref[...].astype(F32)) * a
             + jax.nn.sigmoid(gs_ref[...].astype(F32)) * b)
    h = x_ref[...] + _dot(mixed.astype(BF16), wo_ref[...])
    h_ref[...] = h
    hn_ref[...] = _rms(h, g_ref[...]).astype(hn_ref.dtype)


def _resident(shape):
    return pl.BlockSpec(shape, lambda *_: (0,) * len(shape), pipeline_mode=pl.Buffered(1))


def _merge(yr, os_, gates, x, wr, ws, wo, g_ffn):
    m, d = x.shape
    rw = yr.shape[1]
    bm = _pick(m, 512, SUBLANES)
    blk = bm * (2 * rw * 2 + 2 * d * 2 + d * 4 + d * 4 + d * 2) + (2 * rw * d + d * d)
    return pl.pallas_call(
        _merge_kernel,
        out_shape=(jax.ShapeDtypeStruct((m, d), F32), jax.ShapeDtypeStruct((m, d), BF16)),
        grid=(m // bm,),
        in_specs=[
            pl.BlockSpec((bm, rw), lambda i: (i, 0)),
            pl.BlockSpec((bm, rw), lambda i: (i, 0)),
            pl.BlockSpec((bm, d), lambda i: (i, 0)),
            pl.BlockSpec((bm, d), lambda i: (i, 1)),
            pl.BlockSpec((bm, d), lambda i: (i, 0)),
            _resident(wr.shape), _resident(ws.shape), _resident(wo.shape),
            pl.BlockSpec((1, d), lambda i: (0, 0)),
        ],
        out_specs=(pl.BlockSpec((bm, d), lambda i: (i, 0)), pl.BlockSpec((bm, d), lambda i: (i, 0))),
        compiler_params=pltpu.CompilerParams(
            dimension_semantics=("parallel",),
            vmem_limit_bytes=_vmem_limit(blk + bm * d * 16)),
        name="merge",
    )(yr, os_, gates, gates, x, wr, ws, wo, g_ffn)


def _gelu_tanh(x):
    return 0.5 * x * (1.0 + jnp.tanh(math.sqrt(2.0 / math.pi) * (x + 0.044715 * x * x * x)))


def _ffn_kernel(hn_ref, wg_ref, wv_ref, cw_ref, wd_ref, prev_ref, o_ref, tail_ref, carry_ref,
                *, blocks_per_seq, seq_len):
    i = pl.program_id(0)
    f = pl.program_id(1)

    @pl.when(f == 0)
    def _():
        o_ref[...] = jnp.zeros(o_ref.shape, F32)

    hn = hn_ref[...]
    ug = _dot(hn, wg_ref[...])
    uv = _dot(hn, wv_ref[...])
    bm = ug.shape[0]
    row = _iota2(ug.shape, 0)
    r1 = pltpu.roll(ug, 1, 0)
    r2 = pltpu.roll(ug, 2, 0)
    if seq_len == SUBLANES:
        p2 = prev_ref[...]
        p1 = pltpu.roll(p2, bm - 1, 0)
        t = _mod_pow2(row, SUBLANES)
        s1 = jnp.where(t == 0, p1, r1)
        s2 = jnp.where(t < 2, p2, r2)
        tail_ref[...] = ug
    else:
        first = (i % blocks_per_seq) == 0
        pv = jnp.where(first, prev_ref[0], carry_ref[f])
        head = _iota2((SUBLANES, ug.shape[1]), 0)
        s1 = jnp.concatenate(
            [jnp.where(head == 0, pv[7:8], r1[:SUBLANES]), r1[SUBLANES:]], axis=0)
        s2 = jnp.concatenate(
            [jnp.where(head == 0, pv[6:7], jnp.where(head == 1, pv[7:8], r2[:SUBLANES])),
             r2[SUBLANES:]], axis=0)
        carry_ref[f] = ug[bm - SUBLANES:]
        tail_ref[0] = ug[bm - SUBLANES:]
    conv = cw_ref[0:1, :] * s2 + cw_ref[1:2, :] * s1 + cw_ref[2:3, :] * ug
    act = (_gelu_tanh(conv) * uv).astype(BF16)
    o_ref[...] += _dot(act, wd_ref[...])


def _ffn(hn, w_up, conv_w, w_down, prev8, seq_len):
    m, d = hn.shape
    ff = w_down.shape[0]
    bf = _pick(ff, 512, 2 * LANES) if ff % (2 * LANES) == 0 else _pick(ff, 512, LANES)
    nf = ff // bf
    if seq_len == SUBLANES:
        bm = m
        blocks_per_seq = 1
        prev_spec = pl.BlockSpec((bm, bf), lambda i, f: (i, f))
        tail_shape = (m, ff)
        tail_spec = pl.BlockSpec((bm, bf), lambda i, f: (i, f))
    else:
        bm = _pick(seq_len, 1024, SUBLANES)
        blocks_per_seq = seq_len // bm
        prev_spec = pl.BlockSpec((1, SUBLANES, bf), lambda i, f: (i // blocks_per_seq, 0, f))
        tail_shape = (m // bm, SUBLANES, ff)
        tail_spec = pl.BlockSpec((1, SUBLANES, bf), lambda i, f: (i, 0, f))
    blk = bm * d * (2 + 4) + 3 * d * bf * 2 + 4 * bm * bf * 4
    out, tail = pl.pallas_call(
        functools.partial(_ffn_kernel, blocks_per_seq=blocks_per_seq, seq_len=seq_len),
        out_shape=(jax.ShapeDtypeStruct((m, d), F32), jax.ShapeDtypeStruct(tail_shape, F32)),
        grid=(m // bm, nf),
        in_specs=[
            pl.BlockSpec((bm, d), lambda i, f: (i, 0)),
            pl.BlockSpec((d, bf), lambda i, f: (0, f)),
            pl.BlockSpec((d, bf), lambda i, f: (0, nf + f)),
            pl.BlockSpec((3, bf), lambda i, f: (0, f)),
            pl.BlockSpec((bf, d), lambda i, f: (f, 0)),
            prev_spec,
        ],
        out_specs=(pl.BlockSpec((bm, d), lambda i, f: (i, 0)), tail_spec),
        scratch_shapes=[pltpu.VMEM((nf, SUBLANES, bf), F32)],
        compiler_params=pltpu.CompilerParams(
            dimension_semantics=("arbitrary", "arbitrary"),
            vmem_limit_bytes=_vmem_limit(blk)),
        name="ffn",
    )(hn, w_up, w_up, conv_w, w_down, prev8)
    n_seq = m // seq_len
    if seq_len == SUBLANES:
        return out, tail.reshape(n_seq, SUBLANES, ff)
    return out, tail.reshape(n_seq, blocks_per_seq, SUBLANES, ff)[:, -1]


def _ple_kernel(h_ref, f_ref, pe_ref, wple_ref, wpg_ref, gp_ref, gf_ref, y_ref, *, final_norm):
    h2 = h_ref[...] + f_ref[...]
    hn = _rms(h2, gp_ref[...]).astype(BF16)
    gate = jax.nn.sigmoid(_dot(hn, wpg_ref[...]))
    emb = _dot(pe_ref[...].astype(BF16), wple_ref[...])
    h3 = h2 + emb * gate
    y_ref[...] = _rms(h3, gf_ref[...]) if final_norm else h3


def _ple_out(h, ffn_out, pe, w_ple, w_pg, g_ple, g_final, final_norm):
    m, d = h.shape
    pd = pe.shape[1]
    bm = _pick(m, 512, SUBLANES)
    blk = bm * (3 * d * 4 + pd * 4) + pd * d * 2 + d * d * 2
    return pl.pallas_call(
        functools.partial(_ple_kernel, final_norm=final_norm),
        out_shape=jax.ShapeDtypeStruct((m, d), F32),
        grid=(m // bm,),
        in_specs=[
            pl.BlockSpec((bm, d), lambda i: (i, 0)),
            pl.BlockSpec((bm, d), lambda i: (i, 0)),
            pl.BlockSpec((bm, pd), lambda i: (i, 0)),
            _resident(w_ple.shape), _resident(w_pg.shape),
            pl.BlockSpec((1, d), lambda i: (0, 0)),
            pl.BlockSpec((1, d), lambda i: (0, 0)),
        ],
        out_specs=pl.BlockSpec((bm, d), lambda i: (i, 0)),
        compiler_params=pltpu.CompilerParams(
            dimension_semantics=("parallel",),
            vmem_limit_bytes=_vmem_limit(blk + bm * d * 8)),
        name="ple_out",
    )(h, ffn_out, pe, w_ple, w_pg, g_ple, g_final)


def _prep_weights(g_mix, w_in, mu_shift, w0, w2, a0, a2, g2, k_k, k_a, r_k, ln_x_g, ln_x_b,
                  w_br_r, w_br_s, w_o, g_ffn, w_up, conv_w, w_down, g_ple, w_ple, w_pg, sb_bias):
    d = w_in.shape[0]
    rw = w0.shape[0]
    dl, il, gl = w2.shape[0], a2.shape[0], g2.shape[0]
    dlp, ilp, glp = (_ceil_to(n, LANES) for n in (dl, il, gl))
    sw = w_br_s.shape[0]
    rc = 3 * rw + dl + il + gl
    c_w, c_a, c_g = 3 * rw, 3 * rw + dl, 3 * rw + dl + il

    def regroup(x):
        return jnp.concatenate([
            x[..., :c_w], _pad_to(x[..., c_w:c_a], -1, dlp), _pad_to(x[..., c_a:c_g], -1, ilp),
            _pad_to(x[..., c_g:rc], -1, glp)], axis=-1)

    w_r = regroup(w_in[:, :rc]).astype(BF16)
    scale = HEAD_DIM ** -0.5
    w_q = (w_in[:, rc:rc + sw] * scale).astype(BF16)
    w_k = w_in[:, rc + sw:rc + 2 * sw].astype(BF16)
    w_v = w_in[:, rc + 2 * sw:rc + 3 * sw].astype(BF16)
    w_g = w_in[:, rc + 3 * sw:].astype(BF16)
    nh_s = sw // HEAD_DIM
    return dict(
        d=d, rw=rw, sw=sw, dl=dl, il=il, gl=gl, dlp=dlp, ilp=ilp, glp=glp, rc=rc,
        regroup=regroup,
        g_mix=g_mix.reshape(1, d), w_all=jnp.concatenate([w_r, w_q, w_k, w_v, w_g], axis=1),
        w_kv_t=jnp.transpose(w_in)[rc + sw:rc + 3 * sw].astype(BF16),
        proj_widths=(w_r.shape[1], sw, sw, sw, w_g.shape[1]),
        mu=regroup(mu_shift).reshape(1, -1),
        w0=w0.reshape(1, rw), a0=a0.reshape(1, rw),
        w2=_pad_to(w2, 0, dlp).astype(BF16), a2=_pad_to(a2, 0, ilp).astype(BF16),
        g2=_pad_to(g2, 0, glp).astype(BF16),
        kkw=k_k.reshape(1, rw), kaw=k_a.reshape(1, rw), rkw=r_k.reshape(1, rw),
        lng=ln_x_g.reshape(1, rw), lnb=ln_x_b.reshape(1, rw),
        w_br_r=w_br_r.astype(BF16), w_br_s=w_br_s.astype(BF16), w_o=w_o.astype(BF16),
        g_ffn=g_ffn.reshape(1, d), w_up=w_up.astype(BF16), conv_w=conv_w,
        w_down=w_down.astype(BF16), g_ple=g_ple.reshape(1, d),
        w_ple=w_ple.astype(BF16), w_pg=w_pg.astype(BF16),
        bias2=jnp.broadcast_to(sb_bias.reshape(nh_s // 2, 2, 1), (nh_s // 2, 2, LANES)).astype(F32),
        sb_bias=sb_bias,
    )


def _pages_transposed(cache):
    n_pool, page, nh, hd = cache.shape
    return jnp.transpose(cache, (0, 2, 3, 1)).reshape(n_pool, nh * hd, page)


def _layer(x3, pe3, shift_prev, wkv_prev, conv_prev, past, wp, g_final, final_norm):
    bsz, t, d = x3.shape
    m = bsz * t
    rw, sw = wp["rw"], wp["sw"]
    nh = rw // HEAD_DIM
    x = x3.reshape(m, d)

    kv_t = past is None
    p_r, q, k_s, v_s, gates = _rms_proj(
        x, wp["g_mix"], wp["w_all"], wp["w_kv_t"], wp["proj_widths"], (F32, BF16, F32, F32, BF16),
        (False, False, kv_t, kv_t, False), t)
    nh_s = sw // HEAD_DIM
    if kv_t:
        k_out = jnp.transpose(k_s.reshape(bsz, nh_s, HEAD_DIM, t), (0, 3, 1, 2))
        v_out = jnp.transpose(v_s.reshape(bsz, nh_s, HEAD_DIM, t), (0, 3, 1, 2))
    else:
        k_out = k_s.reshape(bsz, t, nh_s, HEAD_DIM)
        v_out = v_s.reshape(bsz, t, nh_s, HEAD_DIM)

    p3 = p_r.reshape(bsz, t, -1)
    shift = wp["regroup"](shift_prev).reshape(bsz, 1, -1)
    g_c, h_c, rp, yl, gate, bonus = _wkv_intra(
        p3, shift, wp["mu"], wp["w0"], wp["a0"], wp["w2"], wp["a2"], wp["g2"],
        wp["kkw"], wp["kaw"], wp["rkw"], rw=rw, dlp=wp["dlp"], ilp=wp["ilp"], glp=wp["glp"])
    y_r, wkv_new = _wkv_scan(g_c, h_c, rp, yl, gate, bonus, wkv_prev, wp["lng"], wp["lnb"])
    y_r = y_r[:, :t].reshape(m, rw)
    last = p3[:, -1]
    dlp, ilp = wp["dlp"], wp["ilp"]
    c0 = 3 * rw
    shift_new = jnp.concatenate([
        last[:, :c0], last[:, c0:c0 + wp["dl"]], last[:, c0 + dlp:c0 + dlp + wp["il"]],
        last[:, c0 + dlp + ilp:c0 + dlp + ilp + wp["gl"]]], axis=-1)

    if past is None:
        o_s = _attn_prompt(q.reshape(bsz, t, sw), k_s, v_s, wp["bias2"])
    else:
        cache_k, cache_v, page_table = past
        nh_s = sw // HEAD_DIM
        rows_n = nh_s * t
        q3 = q.reshape(bsz, t, sw)
        rr = jnp.arange(rows_n)[:, None] // t
        cc = jnp.arange(sw)[None, :] // HEAD_DIM
        q_bd = jnp.where(rr == cc, jnp.tile(q3, (1, nh_s, 1)), jnp.zeros((), BF16))
        k_new = _pad_to(k_s.reshape(bsz, t, sw).astype(BF16), 1, LANES)
        v_new = _pad_to(v_s.reshape(bsz, t, sw).astype(BF16), 1, LANES)
        bias_rows = jnp.broadcast_to(jnp.repeat(wp["sb_bias"].astype(F32), t)[:, None], (rows_n, LANES))
        o_s = _attn_paged(q_bd, k_new, v_new, bias_rows,
                          _pages_transposed(cache_k), _pages_transposed(cache_v), page_table, t)
    o_s = o_s.reshape(m, sw)

    h, hn = _merge(y_r, o_s, gates, x, wp["w_br_r"], wp["w_br_s"], wp["w_o"], wp["g_ffn"])
    ff = wp["w_down"].shape[0]
    nprev = conv_prev.shape[1]
    pad_rows = jnp.zeros((bsz, SUBLANES - nprev, ff), F32)
    if t == SUBLANES:
        prev8 = jnp.concatenate([conv_prev, pad_rows], axis=1).reshape(m, ff)
    else:
        prev8 = jnp.concatenate([pad_rows, conv_prev], axis=1)
    ffn_out, tail = _ffn(hn, wp["w_up"], wp["conv_w"], wp["w_down"], prev8, t)
    conv_new = tail[:, SUBLANES - nprev:]
    y = _ple_out(h, ffn_out, pe3.reshape(m, -1), wp["w_ple"], wp["w_pg"], wp["g_ple"],
                 g_final.reshape(1, d), final_norm)
    return (y.reshape(bsz, t, d), shift_new, wkv_new, conv_new,
            k_out, v_out)


def kernel(x_prompt, x_sample, state_shift, state_wkv, state_conv, cache_k, cache_v, page_table, p_prompt, p_sample, g_mix, w_in, sb_bias, mu_shift, w0, w2, a0, a2, g2, k_k, k_a, r_k, ln_x_g, ln_x_b, w_br_r, w_br_s, w_o, g_ffn, w_up, conv_w, w_down, g_ple, w_ple, w_pg, g_final):
    depth = w_in.shape[0]
    bsz = x_prompt.shape[0]
    rw = w0.shape[1]
    nh = rw // HEAD_DIM
    ff = w_down.shape[1]
    rc = state_shift.shape[-1]
    nprev = state_conv.shape[2]
    h_p, h_s = x_prompt, x_sample
    outs_p = [[] for _ in range(5)]
    outs_s = [[] for _ in range(5)]
    for i in range(depth):
        wp = _prep_weights(g_mix[i], w_in[i], mu_shift[i], w0[i], w2[i], a0[i], a2[i], g2[i], k_k[i],
                           k_a[i], r_k[i], ln_x_g[i], ln_x_b[i], w_br_r[i], w_br_s[i], w_o[i],
                           g_ffn[i], w_up[i], conv_w[i], w_down[i], g_ple[i], w_ple[i], w_pg[i],
                           sb_bias[i])
        last = i == depth - 1
        res_p = _layer(h_p, p_prompt[i], jnp.zeros((bsz, rc), F32),
                       jnp.zeros((bsz, nh, HEAD_DIM, HEAD_DIM), F32),
                       jnp.zeros((bsz, nprev, ff), F32), None, wp, g_final, last)
        res_s = _layer(h_s, p_sample[i], state_shift[i], state_wkv[i], state_conv[i],
                       (cache_k[i], cache_v[i], page_table), wp, g_final, last)
        h_p, h_s = res_p[0], res_s[0]
        for dst, res in ((outs_p, res_p), (outs_s, res_s)):
            for lst, val in zip(dst, res[1:]):
                lst.append(val)
    return (h_p, h_s, *(jnp.stack(o) for o in outs_p), *(jnp.stack(o) for o in outs_s))
```

```python
import functools
import math

import jax
import jax.numpy as jnp
from jax import lax
from jax.experimental import pallas as pl
from jax.experimental.pallas import tpu as pltpu

F32 = jnp.float32
BF16 = jnp.bfloat16

LANES = 128
SUBLANES = 8
V7X_SCOPED_VMEM_BYTES = 60000 * 1024

HEAD_DIM = 64
CHUNK = 128
INTRA_PAIRS_PER_STEP = 8
SCAN_PAIRS_PER_STEP = 4
DECAY_SCALE = 0.606531
NORM_EPS = 1e-6
GN_EPS = HEAD_DIM * 1e-5
KK_EPS = 1e-24


def _vmem_limit(block_bytes):
    return int(min(V7X_SCOPED_VMEM_BYTES, 2 * block_bytes + (16 << 20)))


def _pick(n, pref, align):
    if n <= pref:
        return n
    best = None
    for d in range(align, pref + 1, align):
        if n % d == 0:
            best = d
    assert best is not None, (n, pref, align)
    return best


def _pad_to(x, axis, size):
    pad = size - x.shape[axis]
    if pad == 0:
        return x
    widths = [(0, 0)] * x.ndim
    widths[axis] = (0, pad)
    return jnp.pad(x, widths)


def _ceil_to(n, m):
    return -(-n // m) * m


def _dot(a, b):
    return jnp.dot(a, b, preferred_element_type=F32)


def _dot_nt(a, b):
    return lax.dot_general(a, b, (((1,), (1,)), ((), ())), preferred_element_type=F32)


def _dot_tn(a, b):
    return lax.dot_general(a, b, (((0,), (0,)), ((), ())), preferred_element_type=F32)


def _split2(x):
    hi = x.astype(BF16)
    lo = (x - hi.astype(F32)).astype(BF16)
    return hi, lo


def _split3(x):
    hi = x.astype(BF16)
    r1 = x - hi.astype(F32)
    mid = r1.astype(BF16)
    lo = (r1 - mid.astype(F32)).astype(BF16)
    return hi, mid, lo


def _iota2(shape, dim):
    return lax.broadcasted_iota(jnp.int32, shape, dim)


def _div_pow2(x, n):
    assert n & (n - 1) == 0, n
    return x >> (n.bit_length() - 1)


def _mod_pow2(x, n):
    assert n & (n - 1) == 0, n
    return x & (n - 1)


def _head_sum(x, e2):
    hi, lo = _split2(x)
    return _dot(jnp.concatenate([hi, lo], axis=1), e2)


def _same_head_matrix():
    r = _mod_pow2(_iota2((2 * LANES, LANES), 0), LANES)
    c = _iota2((2 * LANES, LANES), 1)
    return jnp.where(_div_pow2(r, HEAD_DIM) == _div_pow2(c, HEAD_DIM), 1.0, 0.0).astype(BF16)


def _rms_proj_kernel(x_ref, g_ref, w_ref, wt_ref, *refs, bounds, transposed):
    o_refs, xn_ref = refs[:-1], refs[-1]
    j = pl.program_id(1)

    @pl.when(j == 0)
    def _():
        x = x_ref[...]
        ms = jnp.mean(x * x, axis=-1, keepdims=True)
        xn_ref[...] = (x * lax.rsqrt(ms + NORM_EPS) * g_ref[...]).astype(BF16)

    for o_ref, (lo, hi), tr in zip(o_refs, bounds, transposed):
        @pl.when((j >= lo) & (j < hi))
        def _(o_ref=o_ref, tr=tr):
            if tr:
                o_ref[0] = _dot_nt(wt_ref[...], xn_ref[...]).astype(o_ref.dtype)
            else:
                o_ref[...] = _dot(xn_ref[...], w_ref[...]).astype(o_ref.dtype)


def _rms_proj(x, g, w, w_t, widths, dtypes, transposed, seq_len):
    m, d = x.shape
    assert w.shape[1] == sum(widths)
    bm = _pick(seq_len, 1024, SUBLANES) if any(transposed) else _pick(m, 1024, SUBLANES)
    bps = seq_len // bm
    bn = _pick(math.gcd(*widths), 512, LANES)
    bounds, lo = [], 0
    for wd in widths:
        bounds.append((lo, lo + wd // bn))
        lo += wd // bn
    t_ranges = [b for b, tr in zip(bounds, transposed) if tr]
    t_lo = t_ranges[0][0] if t_ranges else 0
    t_hi = t_ranges[-1][1] if t_ranges else 0
    assert sum(hi - a for a, hi in t_ranges) == t_hi - t_lo
    if not t_ranges:
        w_t = w_t[:bn]
    assert w_t.shape == (max(t_hi - t_lo, 1) * bn, d)

    def w_index(i, j):
        return (0, jnp.where((j >= t_lo) & (j < t_hi), max(t_lo - 1, 0), j))

    def out_spec(lo, hi, tr):
        if tr:
            return pl.BlockSpec((1, bn, bm),
                                lambda i, j: (i // bps, jnp.clip(j - lo, 0, hi - lo - 1), i % bps))
        return pl.BlockSpec((bm, bn), lambda i, j: (i, jnp.clip(j - lo, 0, hi - lo - 1)))

    out_shape = tuple(
        jax.ShapeDtypeStruct((m // seq_len, wd, seq_len) if tr else (m, wd), t)
        for wd, t, tr in zip(widths, dtypes, transposed))
    blk = (bm * d * 4 + 2 * d * bn * 2 + bm * d * 2
           + sum(bm * bn * jnp.dtype(t).itemsize for t in dtypes))
    return pl.pallas_call(
        functools.partial(_rms_proj_kernel, bounds=tuple(bounds), transposed=tuple(transposed)),
        out_shape=out_shape,
        grid=(m // bm, lo),
        in_specs=[
            pl.BlockSpec((bm, d), lambda i, j: (i, 0)),
            pl.BlockSpec((1, d), lambda i, j: (0, 0)),
            pl.BlockSpec((d, bn), w_index),
            pl.BlockSpec((bn, d), lambda i, j: (jnp.clip(j - t_lo, 0, max(t_hi - t_lo, 1) - 1), 0)),
        ],
        out_specs=tuple(out_spec(a, b, tr) for (a, b), tr in zip(bounds, transposed)),
        scratch_shapes=[pltpu.VMEM((bm, d), BF16)],
        compiler_params=pltpu.CompilerParams(
            dimension_semantics=("parallel", "arbitrary"),
            vmem_limit_bytes=_vmem_limit(blk)),
        name="rms_proj",
    )(x, g, w, w_t)


def _wkv_intra_kernel(r_ref, k_ref, v_ref, xw_ref, xa_ref, xg_ref,
                      rp_ref, kp_ref, vp_ref, xwp_ref, xap_ref, xgp_ref,
                      rs_ref, ks_ref, vs_ref, xws_ref, xas_ref, xgs_ref,
                      mur_ref, muk_ref, muv_ref, muw_ref, mua_ref, mug_ref,
                      w0_ref, a0_ref, w2_ref, a2_ref, g2_ref,
                      kkw_ref, kaw_ref, rkw_ref,
                      g_out, h_out, rp_out, yl_out, gate_out, bonus_out,
                      *, t_real, n_pairs):
    c_idx = pl.program_id(1)
    C = CHUNK
    first = c_idx == 0

    def mixed(x_ref, p_ref, s_ref, mu_ref):
        x = x_ref[0]
        prev_last = jnp.where(first, s_ref[0], p_ref[0][SUBLANES - 1:SUBLANES])
        rolled = pltpu.roll(x, 1, 0)
        prev = jnp.where(_iota2(x.shape, 0) == 0, prev_last, rolled)
        y = x + (prev - x) * mu_ref[...]
        if t_real < C:
            y = jnp.concatenate([y, jnp.zeros((C - t_real, y.shape[1]), F32)], axis=0)
        return y

    r_all = mixed(r_ref, rp_ref, rs_ref, mur_ref)
    k_all = mixed(k_ref, kp_ref, ks_ref, muk_ref)
    v_all = mixed(v_ref, vp_ref, vs_ref, muv_ref)
    xw = jnp.tanh(mixed(xw_ref, xwp_ref, xws_ref, muw_ref)).astype(BF16)
    xa = mixed(xa_ref, xap_ref, xas_ref, mua_ref).astype(BF16)
    xg = jax.nn.sigmoid(mixed(xg_ref, xgp_ref, xgs_ref, mug_ref)).astype(BF16)

    lw_all = -DECAY_SCALE * jax.nn.sigmoid(w0_ref[...] + _dot(xw, w2_ref[...]))
    if t_real < C:
        lw_all = jnp.where(_iota2(lw_all.shape, 0) < t_real, lw_all, 0.0)
    iclr_all = jax.nn.sigmoid(a0_ref[...] + _dot(xa, a2_ref[...]))
    gate_all = _dot(xg, g2_ref[...])
    gate_out[0] = gate_all.astype(gate_out.dtype)

    e2 = _same_head_matrix()
    row = _iota2((C, C), 0)
    col = _iota2((C, C), 1)
    strict_lower = row > col
    lower = row >= col
    l_incl = jnp.where(lower, 1.0, 0.0).astype(BF16)
    l3 = jnp.concatenate([l_incl, l_incl, l_incl], axis=1)
    lane = _iota2((C, LANES), 1)
    head_masks = (lane < HEAD_DIM, lane >= HEAD_DIM)
    rr = _iota2((LANES, LANES), 0)
    cc = _iota2((LANES, LANES), 1)
    block_mask = _div_pow2(rr, HEAD_DIM) == _div_pow2(cc, HEAD_DIM)
    eye_mask = rr == cc

    pairs = range(n_pairs)
    sls = [slice(p * LANES, (p + 1) * LANES) for p in pairs]
    r = [r_all[:, sl] for sl in sls]
    k = [k_all[:, sl] for sl in sls]
    v = [v_all[:, sl] for sl in sls]
    lw = [lw_all[:, sl] for sl in sls]
    iclr = [iclr_all[:, sl] for sl in sls]

    kkr = [k[p] * kkw_ref[:, sls[p]] for p in pairs]
    k_mod = [k[p] * (1.0 + (iclr[p] - 1.0) * kaw_ref[:, sls[p]]) for p in pairs]
    kk_ss = [_head_sum(kkr[p] * kkr[p], e2) for p in pairs]
    rk_sum = [_head_sum(r[p] * k_mod[p] * rkw_ref[:, sls[p]], e2) for p in pairs]
    cum = [_dot(l3, jnp.concatenate(_split3(lw[p]), axis=0)) for p in pairs]

    vb, em, ecl, kbar, bbar, rhs_scores, lhs_scores, am_b, rm_f = [], [], [], [], [], [], [], [], []
    for p in pairs:
        bonus_out[0, :, sls[p]] = rk_sum[p] * v[p]
        kk = kkr[p] * lax.rsqrt(jnp.maximum(kk_ss[p], KK_EPS))
        b = kk * iclr[p]
        m_row = cum[p][C // 2 - 1:C // 2]
        cum_last = cum[p][C - 1:C]
        g = cum[p] - m_row
        eng = jnp.exp(-g)
        ebar = jnp.exp(cum_last - cum[p])
        at = -kk * jnp.exp(g - lw[p])
        rt = r[p] * jnp.exp(g)
        kbar.append((k_mod[p] * ebar).astype(BF16))
        bbar.append((b * ebar).astype(BF16))
        em.append(jnp.exp(m_row))
        ecl.append(jnp.exp(cum_last))
        vb.append(v[p].astype(BF16))
        rhs_scores.append(jnp.concatenate([(k_mod[p] * eng).astype(BF16), (b * eng).astype(BF16)], axis=0))
        stack = []
        for mh in head_masks:
            am_b.append(jnp.where(mh, at, 0.0).astype(BF16))
            rm_f.append(jnp.where(mh, rt, 0.0))
            stack += [am_b[-1], rm_f[-1].astype(BF16)]
        lhs_scores.append(jnp.concatenate(stack, axis=0))

    sc = [_dot_nt(lhs_scores[p], rhs_scores[p]) for p in pairs]
    heads = range(2 * n_pairs)
    m_ak, m_ab, p_rk_b, p_rb_b = [], [], [], []
    for i in heads:
        s = sc[i // 2][(i % 2) * 2 * C:(i % 2 + 1) * 2 * C]
        m_ak.append(jnp.where(strict_lower, s[:C, :C], 0.0).astype(BF16))
        m_ab.append(jnp.where(strict_lower, s[:C, C:], 0.0))
        p_rk_b.append(jnp.where(lower, s[C:, :C], 0.0).astype(BF16))
        p_rb_b.append(jnp.where(lower, s[C:, C:], 0.0).astype(BF16))

    eye = jnp.where(row == col, 1.0, 0.0).astype(F32)
    t = [eye + m_ab[i] for i in heads]
    qb = [m_ab[i].astype(BF16) for i in heads]
    q = [_dot(qb[i], qb[i]) for i in heads]
    w1 = [_dot(m_ak[i], vb[i // 2]).astype(BF16) for i in heads]
    for _ in range(int(math.ceil(math.log2(min(t_real, C)))) - 2):
        qb = [q[i].astype(BF16) for i in heads]
        st = [_dot(jnp.concatenate([t[i].astype(BF16), qb[i]], axis=0), qb[i]) for i in heads]
        t = [t[i] + st[i][:C] for i in heads]
        q = [st[i][C:] for i in heads]
    corr = [_dot(t[i].astype(BF16), q[i].astype(BF16)) for i in heads]
    t_inv = [(t[i] + corr[i]).astype(BF16) for i in heads]
    tu = [_dot(t_inv[i], jnp.concatenate([w1[i], am_b[i]], axis=1)) for i in heads]
    ul = [tu[i][:, :LANES] for i in heads]
    ap = [tu[i][:, LANES:] for i in heads]
    yl = [_dot(jnp.concatenate([p_rk_b[i], p_rb_b[i]], axis=1),
               jnp.concatenate([vb[i // 2], ul[i].astype(BF16)], axis=0)) for i in heads]
    rp_add = [_dot(p_rb_b[i], ap[i].astype(BF16)) for i in heads]

    m1 = head_masks[1]
    ul_pair = [jnp.where(m1, ul[2 * p + 1], ul[2 * p]).astype(BF16) for p in pairs]
    ap_pair = [((ap[2 * p] + ap[2 * p + 1]) * em[p]).astype(BF16) for p in pairs]
    ab = [_dot_tn(ap_pair[p], bbar[p]) for p in pairs]
    hh = [_dot_tn(jnp.concatenate([vb[p], ul_pair[p]], axis=0),
                  jnp.concatenate([kbar[p], bbar[p]], axis=0)) for p in pairs]
    for p in pairs:
        sl = sls[p]
        g_out[0, 0, :, sl] = (jnp.where(block_mask, ab[p], 0.0)
                              + jnp.where(eye_mask, ecl[p], 0.0)).astype(g_out.dtype)
        h_out[0, 0, :, sl] = jnp.where(block_mask, hh[p], 0.0)
        rp_pair = rm_f[2 * p] + rp_add[2 * p] + rm_f[2 * p + 1] + rp_add[2 * p + 1]
        rp_out[0, :, sl] = (rp_pair * em[p]).astype(rp_out.dtype)
        yl_out[0, :, sl] = jnp.where(m1, yl[2 * p + 1], yl[2 * p])


def _wkv_intra(p3, shift, mu, w0, a0, w2, a2, g2, kkw, kaw, rkw, *, rw, dlp, ilp, glp):
    bsz, t, npc = p3.shape
    C = CHUNK
    t_real = min(t, C)
    assert t % t_real == 0 and t_real % SUBLANES == 0
    nch = t // t_real
    tp = nch * C
    n_pairs = _pick(rw // LANES, INTRA_PAIRS_PER_STEP, 1)
    lw_ = n_pairs * LANES
    ng = rw // lw_
    o_w, o_a, o_g = 3 * rw, 3 * rw + dlp, 3 * rw + dlp + ilp
    assert o_w % dlp == 0 and o_a % ilp == 0 and o_g % glp == 0
    rows_prev = t_real // SUBLANES

    def cur(width, off):
        return pl.BlockSpec((1, t_real, width), lambda b, c, g, o=off // width: (b, c, o))

    def cur_g(width, off):
        return pl.BlockSpec((1, t_real, width), lambda b, c, g, o=off // width: (b, c, o + g))

    def prev(width, off):
        return pl.BlockSpec((1, SUBLANES, width),
                            lambda b, c, g, o=off // width: (b, jnp.maximum(c * rows_prev - 1, 0), o))

    def prev_g(width, off):
        return pl.BlockSpec((1, SUBLANES, width),
                            lambda b, c, g, o=off // width: (b, jnp.maximum(c * rows_prev - 1, 0), o + g))

    def sh(width, off):
        return pl.BlockSpec((1, 1, width), lambda b, c, g, o=off // width: (b, 0, o))

    def sh_g(width, off):
        return pl.BlockSpec((1, 1, width), lambda b, c, g, o=off // width: (b, 0, o + g))

    def vec(width, off):
        return pl.BlockSpec((1, width), lambda b, c, g, o=off // width: (0, o))

    def vec_g(width, off=0):
        return pl.BlockSpec((1, width), lambda b, c, g, o=off // width: (0, o + g))

    def mat_g(rows):
        return pl.BlockSpec((rows, lw_), lambda b, c, g: (0, g))

    in_specs = (
        [cur_g(lw_, 0), cur_g(lw_, rw), cur_g(lw_, 2 * rw), cur(dlp, o_w), cur(ilp, o_a), cur(glp, o_g)]
        + [prev_g(lw_, 0), prev_g(lw_, rw), prev_g(lw_, 2 * rw), prev(dlp, o_w), prev(ilp, o_a), prev(glp, o_g)]
        + [sh_g(lw_, 0), sh_g(lw_, rw), sh_g(lw_, 2 * rw), sh(dlp, o_w), sh(ilp, o_a), sh(glp, o_g)]
        + [vec_g(lw_, 0), vec_g(lw_, rw), vec_g(lw_, 2 * rw), vec(dlp, o_w), vec(ilp, o_a), vec(glp, o_g)]
        + [vec_g(lw_), vec_g(lw_), mat_g(dlp), mat_g(ilp), mat_g(glp)]
        + [vec_g(lw_), vec_g(lw_), vec_g(lw_)]
    )
    args = ([p3] * 6 + [p3] * 6 + [shift] * 6 + [mu] * 6 + [w0, a0, w2, a2, g2, kkw, kaw, rkw])
    out_shape = (
        jax.ShapeDtypeStruct((bsz, nch, LANES, rw), BF16),
        jax.ShapeDtypeStruct((bsz, nch, LANES, rw), F32),
        jax.ShapeDtypeStruct((bsz, tp, rw), BF16),
        jax.ShapeDtypeStruct((bsz, tp, rw), F32),
        jax.ShapeDtypeStruct((bsz, tp, rw), BF16),
        jax.ShapeDtypeStruct((bsz, tp, rw), F32),
    )
    gh_spec = pl.BlockSpec((1, 1, LANES, lw_), lambda b, c, g: (b, c, 0, g))
    tok_spec = pl.BlockSpec((1, C, lw_), lambda b, c, g: (b, c, g))
    blk = (t_real * (3 * lw_ + dlp + ilp + glp) * 4 + (dlp + ilp + glp) * lw_ * 2
           + 2 * LANES * lw_ * 4 + C * lw_ * 12)
    return pl.pallas_call(
        functools.partial(_wkv_intra_kernel, t_real=t_real, n_pairs=n_pairs),
        out_shape=out_shape,
        grid=(bsz, nch, ng),
        in_specs=in_specs,
        out_specs=(gh_spec, gh_spec, tok_spec, tok_spec, tok_spec, tok_spec),
        compiler_params=pltpu.CompilerParams(
            dimension_semantics=("parallel", "parallel", "parallel"),
            vmem_limit_bytes=_vmem_limit(blk)),
        name="wkv_intra",
    )(*args)


def _wkv_scan_kernel(g_ref, h_ref, rp_ref, yl_ref, gate_ref, bonus_ref, s0_ref, lng_ref, lnb_ref,
                     y_ref, s_out_ref, st_ref, *, n_chunks, n_pairs):
    C = CHUNK
    N = HEAD_DIM
    e2 = _same_head_matrix()
    inv_n = 1.0 / HEAD_DIM
    lanes = [slice(p * LANES, (p + 1) * LANES) for p in range(n_pairs)]

    zero = jnp.zeros((N, N), F32)
    for p, sl in enumerate(lanes):
        st_ref[:, sl] = jnp.concatenate(
            [jnp.concatenate([s0_ref[0, 2 * p], zero], axis=1),
             jnp.concatenate([zero, s0_ref[0, 2 * p + 1]], axis=1)], axis=0)

    def body(c, carry):
        rows = pl.ds(pl.multiple_of(c * C, C), C)
        sb = [st_ref[:, sl].astype(BF16) for sl in lanes]
        y = [_dot_nt(rp_ref[0, rows, sl], s) for sl, s in zip(lanes, sb)]
        s_new = [_dot(s, g_ref[0, c, :, sl].astype(BF16)) for sl, s in zip(lanes, sb)]
        for sl, s in zip(lanes, s_new):
            st_ref[:, sl] = s + h_ref[0, c, :, sl]
        y = [yy + yl_ref[0, rows, sl] for sl, yy in zip(lanes, y)]
        mu = [_head_sum(yy, e2) * inv_n for yy in y]
        yc = [yy - m for yy, m in zip(y, mu)]
        var = [_head_sum(x * x, e2) * inv_n for x in yc]
        for sl, x, vv in zip(lanes, yc, var):
            yn = x * lax.rsqrt(vv + GN_EPS) * lng_ref[:, sl] + lnb_ref[:, sl]
            out = (yn + bonus_ref[0, rows, sl]) * gate_ref[0, rows, sl].astype(F32)
            y_ref[0, rows, sl] = out.astype(y_ref.dtype)
        return carry

    lax.fori_loop(0, n_chunks, body, 0)
    for p, sl in enumerate(lanes):
        s = st_ref[:, sl]
        s_out_ref[0, 2 * p] = s[:N, :N]
        s_out_ref[0, 2 * p + 1] = s[N:, N:]


def _wkv_scan(g, h, rp, yl, gate, bonus, s0, lng, lnb):
    bsz, nch, _, rw = g.shape
    tp = rp.shape[1]
    n_pairs = _pick(rw // LANES, SCAN_PAIRS_PER_STEP, 1)
    lw_ = n_pairs * LANES
    gh_spec = pl.BlockSpec((1, nch, LANES, lw_), lambda b, q: (b, 0, 0, q))
    tok_spec = pl.BlockSpec((1, tp, lw_), lambda b, q: (b, 0, q))
    st_spec = pl.BlockSpec((1, 2 * n_pairs, HEAD_DIM, HEAD_DIM), lambda b, q: (b, q, 0, 0))
    vec_spec = pl.BlockSpec((1, lw_), lambda b, q: (0, q))
    blk = 2 * nch * LANES * lw_ * 4 + tp * lw_ * (2 + 4 + 2 + 4 + 2) + 4 * LANES * lw_ * 4
    return pl.pallas_call(
        functools.partial(_wkv_scan_kernel, n_chunks=nch, n_pairs=n_pairs),
        out_shape=(jax.ShapeDtypeStruct((bsz, tp, rw), BF16),
                   jax.ShapeDtypeStruct(s0.shape, F32)),
        grid=(bsz, rw // lw_),
        in_specs=[gh_spec, gh_spec, tok_spec, tok_spec, tok_spec, tok_spec, st_spec, vec_spec, vec_spec],
        out_specs=(tok_spec, st_spec),
        scratch_shapes=[pltpu.VMEM((LANES, lw_), F32)],
        compiler_params=pltpu.CompilerParams(
            dimension_semantics=("parallel", "parallel"),
            vmem_limit_bytes=_vmem_limit(blk)),
        name="wkv_scan",
    )(g, h, rp, yl, gate, bonus, s0, lng, lnb)


def _cumsum_rhs():
    r = _iota2((LANES, 2 * LANES), 0)
    c = _iota2((LANES, 2 * LANES), 1)
    return jnp.where((c >= LANES) | (r > c), -1.0, 0.0).astype(BF16)


def _keep_sums(sp, w2):
    return _dot(sp.astype(BF16), w2)


SOFTPLUS_LINEAR_ABOVE = 30.0


def _softplus(z, mask=None):
    sp = jnp.log(1.0 + jnp.exp(jnp.minimum(z, SOFTPLUS_LINEAR_ABOVE)))
    sp = jnp.where(z > SOFTPLUS_LINEAR_ABOVE, z, sp)
    return sp if mask is None else jnp.where(mask, sp, 0.0)


def _sb_tile(z, v_tile, run, acc, w2, mask):
    sp = _softplus(z, mask)
    cs2 = _keep_sums(sp, w2)
    att = jnp.exp((z - sp) + cs2[:, :LANES] + run)
    if mask is not None:
        att = jnp.where(mask, att, 0.0)
    acc = acc + _dot(att.astype(BF16), v_tile)
    return run + cs2[:, LANES:], acc


def _attn_prompt_kernel(q_ref, k_ref, v_ref, bias_ref, o_ref, kb_ref, vb_ref, run_ref, acc_ref):
    qi = pl.program_id(2)
    qb = q_ref.shape[1]
    kb = 2 * LANES
    n_diag = qb // kb
    assert qb == n_diag * kb and n_diag in (1, 2)

    @pl.when(qi == 0)
    def _():
        for c in range(kb_ref.shape[0]):
            kb_ref[c] = k_ref[0, :, c * LANES:(c + 1) * LANES].astype(BF16)
            vb_ref[c] = v_ref[0, :, c * LANES:(c + 1) * LANES].astype(BF16)

    q = q_ref[0]
    m0 = _iota2((qb, LANES), 1) < HEAD_DIM
    zero = jnp.zeros((), BF16)
    qs = jnp.concatenate([jnp.where(m0, q, zero), jnp.where(m0, zero, q)], axis=0)
    w2 = _cumsum_rhs()
    run_ref[...] = jnp.zeros(run_ref.shape, F32)
    acc_ref[...] = jnp.zeros(acc_ref.shape, F32)

    def step(j, n_blocks, mask):
        nk = 2 * n_blocks
        t0 = 2 * j
        z = _dot(qs, jnp.concatenate([kb_ref[t0 + c] for c in range(nk)], axis=1))
        b0 = jnp.concatenate([bias_ref[0, 0:1, :]] * nk, axis=1)
        b1 = jnp.concatenate([bias_ref[0, 1:2, :]] * nk, axis=1)
        z = jnp.concatenate([z[:qb] + b0, z[qb:] + b1], axis=0)
        sp = _softplus(z, mask)
        tiles = [slice(c * LANES, (c + 1) * LANES) for c in range(nk)]
        cs = [_keep_sums(sp[:, c], w2) for c in tiles]
        zs = z - sp
        run = run_ref[...]
        e = [None] * nk
        for c in reversed(range(nk)):
            e[c] = zs[:, tiles[c]] + cs[c][:, :LANES] + run
            run = run + cs[c][:, LANES:]
        att = jnp.exp(jnp.concatenate(e, axis=1))
        if mask is not None:
            att = jnp.where(mask, att, 0.0)
        vt = jnp.concatenate([vb_ref[t0 + c] for c in range(nk)], axis=1)
        acc_ref[...] += _dot_nt(att.astype(BF16), vt)
        run_ref[...] = run

    qpos = _mod_pow2(_iota2((2 * qb, qb), 0), qb)
    step(qi * n_diag, n_diag, _iota2((2 * qb, qb), 1) < qpos)

    n_off = qi * n_diag

    def body(jj, carry):
        step(n_off - 2 - 2 * jj, 2, None)
        return carry

    lax.fori_loop(0, n_off // 2, body, 0)

    if n_diag % 2 == 1:
        @pl.when(n_off % 2 == 1)
        def _():
            step(0, 1, None)

    o_ref[0] = jnp.where(m0, acc_ref[:qb], acc_ref[qb:]).astype(o_ref.dtype)


def _attn_prompt(q, k, v, bias2):
    bsz, t, w = q.shape
    qb = 4 * LANES if t % (4 * LANES) == 0 else 2 * LANES
    assert t % qb == 0
    blk = qb * LANES * 2 * 2 + 2 * t * LANES * 4
    return pl.pallas_call(
        _attn_prompt_kernel,
        out_shape=jax.ShapeDtypeStruct((bsz, t, w), BF16),
        grid=(bsz, w // LANES, t // qb),
        in_specs=[
            pl.BlockSpec((1, qb, LANES), lambda b, p, i: (b, i, p)),
            pl.BlockSpec((1, LANES, t), lambda b, p, i: (b, p, 0)),
            pl.BlockSpec((1, LANES, t), lambda b, p, i: (b, p, 0)),
            pl.BlockSpec((1, 2, LANES), lambda b, p, i: (p, 0, 0)),
        ],
        out_specs=pl.BlockSpec((1, qb, LANES), lambda b, p, i: (b, i, p)),
        scratch_shapes=[pltpu.VMEM((t // LANES, LANES, LANES), BF16),
                        pltpu.VMEM((t // LANES, LANES, LANES), BF16),
                        pltpu.VMEM((2 * qb, LANES), F32), pltpu.VMEM((2 * qb, LANES), F32)],
        compiler_params=pltpu.CompilerParams(
            dimension_semantics=("parallel", "parallel", "arbitrary"),
            vmem_limit_bytes=_vmem_limit(blk + 2 * t * LANES * 2)),
        name="attn_prompt",
    )(q, k, v, bias2)


PAGES_PER_STEP = 16


def _attn_paged_kernel(pt_ref, *refs, n_groups, pps):
    del pt_ref
    k_refs = refs[:pps]
    v_refs = refs[pps:2 * pps]
    q_ref, kn_ref, vn_ref, bias_ref, o_ref, run_ref, acc_ref = refs[2 * pps:]
    g = pl.program_id(1)
    rows_n = q_ref.shape[1]
    t_new = o_ref.shape[1]
    w2 = _cumsum_rhs()
    q = q_ref[0]
    bias = bias_ref[...]

    @pl.when(g == 0)
    def _():
        lane = _iota2((rows_n, LANES), 1)
        row = _iota2((rows_n, LANES), 0)
        mask = lane < _mod_pow2(row, t_new)
        z = _dot_nt(q, kn_ref[0]) + bias
        run, acc = _sb_tile(z, vn_ref[0], jnp.zeros((rows_n, LANES), F32),
                            jnp.zeros(acc_ref.shape, F32), w2, mask)
        run_ref[...] = run
        acc_ref[...] = acc

    pages = range(pps)
    kt = [k_refs[p][0].astype(BF16) for p in pages]
    z2 = [_dot(q, jnp.concatenate(kt[i:i + 2], axis=1)) for i in range(0, pps, 2)]
    z = [z2[p // 2][:, (p % 2) * LANES:(p % 2 + 1) * LANES] + bias for p in pages]
    sp = [_softplus(zz) for zz in z]
    cs2 = [_keep_sums(x, w2) for x in sp]
    run = run_ref[...]
    att = []
    for p in pages:
        att.append(jnp.exp((z[p] - sp[p]) + cs2[p][:, :LANES] + run).astype(BF16))
        run = run + cs2[p][:, LANES:]
    run_ref[...] = run
    vt = jnp.concatenate([v_refs[p][0].astype(BF16) for p in pages], axis=1)
    acc_ref[...] += _dot_nt(jnp.concatenate(att, axis=1), vt)

    @pl.when(g == n_groups - 1)
    def _():
        acc = acc_ref[...]
        rr = _iota2(acc.shape, 0)
        cc = _iota2(acc.shape, 1)
        picked = jnp.where(_div_pow2(rr, t_new) == _div_pow2(cc, HEAD_DIM), acc, 0.0)
        out = picked[0:t_new]
        for h in range(1, rows_n // t_new):
            out = out + picked[h * t_new:(h + 1) * t_new]
        o_ref[0] = out.astype(o_ref.dtype)


def _attn_paged(q_bd, k_new, v_new, bias_rows, cache_k, cache_v, page_table, t_new):
    bsz, rows_n, w = q_bd.shape
    n_pages = page_table.shape[1]
    page = cache_k.shape[2]
    pps = _pick(n_pages, PAGES_PER_STEP, 2)
    assert page == LANES and cache_k.shape[1] == w
    n_groups = n_pages // pps

    def page_spec(p):
        return pl.BlockSpec((1, w, page),
                            lambda b, g, pt, p=p % pps: (pt[b, n_pages - 1 - (g * pps + p)], 0, 0))

    in_specs = ([page_spec(p) for p in range(2 * pps)] + [
        pl.BlockSpec((1, rows_n, w), lambda b, g, pt: (b, 0, 0)),
        pl.BlockSpec((1, LANES, w), lambda b, g, pt: (b, 0, 0)),
        pl.BlockSpec((1, LANES, w), lambda b, g, pt: (b, 0, 0)),
        pl.BlockSpec((rows_n, LANES), lambda b, g, pt: (0, 0)),
    ])
    blk = 2 * pps * page * w * 4 + rows_n * w * 2 + 2 * LANES * w * 2 + rows_n * w * 4
    return pl.pallas_call(
        functools.partial(_attn_paged_kernel, n_groups=n_groups, pps=pps),
        out_shape=jax.ShapeDtypeStruct((bsz, t_new, w), BF16),
        grid_spec=pltpu.PrefetchScalarGridSpec(
            num_scalar_prefetch=1,
            grid=(bsz, n_groups),
            in_specs=in_specs,
            out_specs=pl.BlockSpec((1, t_new, w), lambda b, g, pt: (b, 0, 0)),
            scratch_shapes=[pltpu.VMEM((rows_n, LANES), F32), pltpu.VMEM((rows_n, w), F32)],
        ),
        compiler_params=pltpu.CompilerParams(
            dimension_semantics=("parallel", "arbitrary"),
            vmem_limit_bytes=_vmem_limit(blk)),
        name="attn_paged",
    )(page_table, *([cache_k] * pps), *([cache_v] * pps), q_bd, k_new, v_new, bias_rows)


def _rms(x, g):
    ms = jnp.mean(x * x, axis=-1, keepdims=True)
    return x * lax.rsqrt(ms + NORM_EPS) * g


def _merge_kernel(yr_ref, os_ref, gr_ref, gs_ref, x_ref, wr_ref, ws_ref, wo_ref, g_ref, h_ref, hn_ref):
    a = _dot(yr_ref[...], wr_ref[...])
    b = _dot(os_ref[...], ws_ref[...])
    mixed = (jax.nn.sigmoid(gr_ref[...].astype(F32)) * a
             + jax.nn.sigmoid(gs_ref[...].astype(F32)) * b)
    h = x_ref[...] + _dot(mixed.astype(BF16), wo_ref[...])
    h_ref[...] = h
    hn_ref[...] = _rms(h, g_ref[...]).astype(hn_ref.dtype)


def _resident(shape):
    return pl.BlockSpec(shape, lambda *_: (0,) * len(shape), pipeline_mode=pl.Buffered(1))


def _merge(yr, os_, gates, x, wr, ws, wo, g_ffn):
    m, d = x.shape
    rw = yr.shape[1]
    bm = _pick(m, 512, SUBLANES)
    blk = bm * (2 * rw * 2 + 2 * d * 2 + d * 4 + d * 4 + d * 2) + (2 * rw * d + d * d)
    return pl.pallas_call(
        _merge_kernel,
        out_shape=(jax.ShapeDtypeStruct((m, d), F32), jax.ShapeDtypeStruct((m, d), BF16)),
        grid=(m // bm,),
        in_specs=[
            pl.BlockSpec((bm, rw), lambda i: (i, 0)),
            pl.BlockSpec((bm, rw), lambda i: (i, 0)),
            pl.BlockSpec((bm, d), lambda i: (i, 0)),
            pl.BlockSpec((bm, d), lambda i: (i, 1)),
            pl.BlockSpec((bm, d), lambda i: (i, 0)),
            _resident(wr.shape), _resident(ws.shape), _resident(wo.shape),
            pl.BlockSpec((1, d), lambda i: (0, 0)),
        ],
        out_specs=(pl.BlockSpec((bm, d), lambda i: (i, 0)), pl.BlockSpec((bm, d), lambda i: (i, 0))),
        compiler_params=pltpu.CompilerParams(
            dimension_semantics=("parallel",),
            vmem_limit_bytes=_vmem_limit(blk + bm * d * 16)),
        name="merge",
    )(yr, os_, gates, gates, x, wr, ws, wo, g_ffn)


def _gelu_tanh(x):
    return 0.5 * x * (1.0 + jnp.tanh(math.sqrt(2.0 / math.pi) * (x + 0.044715 * x * x * x)))


def _ffn_kernel(hn_ref, wg_ref, wv_ref, cw_ref, wd_ref, prev_ref, o_ref, tail_ref, carry_ref,
                *, blocks_per_seq, seq_len):
    i = pl.program_id(0)
    f = pl.program_id(1)

    @pl.when(f == 0)
    def _():
        o_ref[...] = jnp.zeros(o_ref.shape, F32)

    hn = hn_ref[...]
    ug = _dot(hn, wg_ref[...])
    uv = _dot(hn, wv_ref[...])
    bm = ug.shape[0]
    row = _iota2(ug.shape, 0)
    r1 = pltpu.roll(ug, 1, 0)
    r2 = pltpu.roll(ug, 2, 0)
    if seq_len == SUBLANES:
        p2 = prev_ref[...]
        p1 = pltpu.roll(p2, bm - 1, 0)
        t = _mod_pow2(row, SUBLANES)
        s1 = jnp.where(t == 0, p1, r1)
        s2 = jnp.where(t < 2, p2, r2)
        tail_ref[...] = ug
    else:
        first = (i % blocks_per_seq) == 0
        pv = jnp.where(first, prev_ref[0], carry_ref[f])
        head = _iota2((SUBLANES, ug.shape[1]), 0)
        s1 = jnp.concatenate(
            [jnp.where(head == 0, pv[7:8], r1[:SUBLANES]), r1[SUBLANES:]], axis=0)
        s2 = jnp.concatenate(
            [jnp.where(head == 0, pv[6:7], jnp.where(head == 1, pv[7:8], r2[:SUBLANES])),
             r2[SUBLANES:]], axis=0)
        carry_ref[f] = ug[bm - SUBLANES:]
        tail_ref[0] = ug[bm - SUBLANES:]
    conv = cw_ref[0:1, :] * s2 + cw_ref[1:2, :] * s1 + cw_ref[2:3, :] * ug
    act = (_gelu_tanh(conv) * uv).astype(BF16)
    o_ref[...] += _dot(act, wd_ref[...])


def _ffn(hn, w_up, conv_w, w_down, prev8, seq_len):
    m, d = hn.shape
    ff = w_down.shape[0]
    bf = _pick(ff, 512, 2 * LANES) if ff % (2 * LANES) == 0 else _pick(ff, 512, LANES)
    nf = ff // bf
    if seq_len == SUBLANES:
        bm = m
        blocks_per_seq = 1
        prev_spec = pl.BlockSpec((bm, bf), lambda i, f: (i, f))
        tail_shape = (m, ff)
        tail_spec = pl.BlockSpec((bm, bf), lambda i, f: (i, f))
    else:
        bm = _pick(seq_len, 1024, SUBLANES)
        blocks_per_seq = seq_len // bm
        prev_spec = pl.BlockSpec((1, SUBLANES, bf), lambda i, f: (i // blocks_per_seq, 0, f))
        tail_shape = (m // bm, SUBLANES, ff)
        tail_spec = pl.BlockSpec((1, SUBLANES, bf), lambda i, f: (i, 0, f))
    blk = bm * d * (2 + 4) + 3 * d * bf * 2 + 4 * bm * bf * 4
    out, tail = pl.pallas_call(
        functools.partial(_ffn_kernel, blocks_per_seq=blocks_per_seq, seq_len=seq_len),
        out_shape=(jax.ShapeDtypeStruct((m, d), F32), jax.ShapeDtypeStruct(tail_shape, F32)),
        grid=(m // bm, nf),
        in_specs=[
            pl.BlockSpec((bm, d), lambda i, f: (i, 0)),
            pl.BlockSpec((d, bf), lambda i, f: (0, f)),
            pl.BlockSpec((d, bf), lambda i, f: (0, nf + f)),
            pl.BlockSpec((3, bf), lambda i, f: (0, f)),
            pl.BlockSpec((bf, d), lambda i, f: (f, 0)),
            prev_spec,
        ],
        out_specs=(pl.BlockSpec((bm, d), lambda i, f: (i, 0)), tail_spec),
        scratch_shapes=[pltpu.VMEM((nf, SUBLANES, bf), F32)],
        compiler_params=pltpu.CompilerParams(
            dimension_semantics=("arbitrary", "arbitrary"),
            vmem_limit_bytes=_vmem_limit(blk)),
        name="ffn",
    )(hn, w_up, w_up, conv_w, w_down, prev8)
    n_seq = m // seq_len
    if seq_len == SUBLANES:
        return out, tail.reshape(n_seq, SUBLANES, ff)
    return out, tail.reshape(n_seq, blocks_per_seq, SUBLANES, ff)[:, -1]


def _ple_kernel(h_ref, f_ref, pe_ref, wple_ref, wpg_ref, gp_ref, gf_ref, y_ref, *, final_norm):
    h2 = h_ref[...] + f_ref[...]
    hn = _rms(h2, gp_ref[...]).astype(BF16)
    gate = jax.nn.sigmoid(_dot(hn, wpg_ref[...]))
    emb = _dot(pe_ref[...].astype(BF16), wple_ref[...])
    h3 = h2 + emb * gate
    y_ref[...] = _rms(h3, gf_ref[...]) if final_norm else h3


def _ple_out(h, ffn_out, pe, w_ple, w_pg, g_ple, g_final, final_norm):
    m, d = h.shape
    pd = pe.shape[1]
    bm = _pick(m, 512, SUBLANES)
    blk = bm * (3 * d * 4 + pd * 4) + pd * d * 2 + d * d * 2
    return pl.pallas_call(
        functools.partial(_ple_kernel, final_norm=final_norm),
        out_shape=jax.ShapeDtypeStruct((m, d), F32),
        grid=(m // bm,),
        in_specs=[
            pl.BlockSpec((bm, d), lambda i: (i, 0)),
            pl.BlockSpec((bm, d), lambda i: (i, 0)),
            pl.BlockSpec((bm, pd), lambda i: (i, 0)),
            _resident(w_ple.shape), _resident(w_pg.shape),
            pl.BlockSpec((1, d), lambda i: (0, 0)),
            pl.BlockSpec((1, d), lambda i: (0, 0)),
        ],
        out_specs=pl.BlockSpec((bm, d), lambda i: (i, 0)),
        compiler_params=pltpu.CompilerParams(
            dimension_semantics=("parallel",),
            vmem_limit_bytes=_vmem_limit(blk + bm * d * 8)),
        name="ple_out",
    )(h, ffn_out, pe, w_ple, w_pg, g_ple, g_final)


def _prep_weights(g_mix, w_in, mu_shift, w0, w2, a0, a2, g2, k_k, k_a, r_k, ln_x_g, ln_x_b,
                  w_br_r, w_br_s, w_o, g_ffn, w_up, conv_w, w_down, g_ple, w_ple, w_pg, sb_bias):
    d = w_in.shape[0]
    rw = w0.shape[0]
    dl, il, gl = w2.shape[0], a2.shape[0], g2.shape[0]
    dlp, ilp, glp = (_ceil_to(n, LANES) for n in (dl, il, gl))
    sw = w_br_s.shape[0]
    rc = 3 * rw + dl + il + gl
    c_w, c_a, c_g = 3 * rw, 3 * rw + dl, 3 * rw + dl + il

    def regroup(x):
        return jnp.concatenate([
            x[..., :c_w], _pad_to(x[..., c_w:c_a], -1, dlp), _pad_to(x[..., c_a:c_g], -1, ilp),
            _pad_to(x[..., c_g:rc], -1, glp)], axis=-1)

    w_r = regroup(w_in[:, :rc]).astype(BF16)
    scale = HEAD_DIM ** -0.5
    w_q = (w_in[:, rc:rc + sw] * scale).astype(BF16)
    w_k = w_in[:, rc + sw:rc + 2 * sw].astype(BF16)
    w_v = w_in[:, rc + 2 * sw:rc + 3 * sw].astype(BF16)
    w_g = w_in[:, rc + 3 * sw:].astype(BF16)
    nh_s = sw // HEAD_DIM
    return dict(
        d=d, rw=rw, sw=sw, dl=dl, il=il, gl=gl, dlp=dlp, ilp=ilp, glp=glp, rc=rc,
        regroup=regroup,
        g_mix=g_mix.reshape(1, d), w_all=jnp.concatenate([w_r, w_q, w_k, w_v, w_g], axis=1),
        w_kv_t=jnp.transpose(w_in)[rc + sw:rc + 3 * sw].astype(BF16),
        proj_widths=(w_r.shape[1], sw, sw, sw, w_g.shape[1]),
        mu=regroup(mu_shift).reshape(1, -1),
        w0=w0.reshape(1, rw), a0=a0.reshape(1, rw),
        w2=_pad_to(w2, 0, dlp).astype(BF16), a2=_pad_to(a2, 0, ilp).astype(BF16),
        g2=_pad_to(g2, 0, glp).astype(BF16),
        kkw=k_k.reshape(1, rw), kaw=k_a.reshape(1, rw), rkw=r_k.reshape(1, rw),
        lng=ln_x_g.reshape(1, rw), lnb=ln_x_b.reshape(1, rw),
        w_br_r=w_br_r.astype(BF16), w_br_s=w_br_s.astype(BF16), w_o=w_o.astype(BF16),
        g_ffn=g_ffn.reshape(1, d), w_up=w_up.astype(BF16), conv_w=conv_w,
        w_down=w_down.astype(BF16), g_ple=g_ple.reshape(1, d),
        w_ple=w_ple.astype(BF16), w_pg=w_pg.astype(BF16),
        bias2=jnp.broadcast_to(sb_bias.reshape(nh_s // 2, 2, 1), (nh_s // 2, 2, LANES)).astype(F32),
        sb_bias=sb_bias,
    )


def _pages_transposed(cache):
    n_pool, page, nh, hd = cache.shape
    return jnp.transpose(cache, (0, 2, 3, 1)).reshape(n_pool, nh * hd, page)


def _layer(x3, pe3, shift_prev, wkv_prev, conv_prev, past, wp, g_final, final_norm):
    bsz, t, d = x3.shape
    m = bsz * t
    rw, sw = wp["rw"], wp["sw"]
    nh = rw // HEAD_DIM
    x = x3.reshape(m, d)

    kv_t = past is None
    p_r, q, k_s, v_s, gates = _rms_proj(
        x, wp["g_mix"], wp["w_all"], wp["w_kv_t"], wp["proj_widths"], (F32, BF16, F32, F32, BF16),
        (False, False, kv_t, kv_t, False), t)
    nh_s = sw // HEAD_DIM
    if kv_t:
        k_out = jnp.transpose(k_s.reshape(bsz, nh_s, HEAD_DIM, t), (0, 3, 1, 2))
        v_out = jnp.transpose(v_s.reshape(bsz, nh_s, HEAD_DIM, t), (0, 3, 1, 2))
    else:
        k_out = k_s.reshape(bsz, t, nh_s, HEAD_DIM)
        v_out = v_s.reshape(bsz, t, nh_s, HEAD_DIM)

    p3 = p_r.reshape(bsz, t, -1)
    shift = wp["regroup"](shift_prev).reshape(bsz, 1, -1)
    g_c, h_c, rp, yl, gate, bonus = _wkv_intra(
        p3, shift, wp["mu"], wp["w0"], wp["a0"], wp["w2"], wp["a2"], wp["g2"],
        wp["kkw"], wp["kaw"], wp["rkw"], rw=rw, dlp=wp["dlp"], ilp=wp["ilp"], glp=wp["glp"])
    y_r, wkv_new = _wkv_scan(g_c, h_c, rp, yl, gate, bonus, wkv_prev, wp["lng"], wp["lnb"])
    y_r = y_r[:, :t].reshape(m, rw)
    last = p3[:, -1]
    dlp, ilp = wp["dlp"], wp["ilp"]
    c0 = 3 * rw
    shift_new = jnp.concatenate([
        last[:, :c0], last[:, c0:c0 + wp["dl"]], last[:, c0 + dlp:c0 + dlp + wp["il"]],
        last[:, c0 + dlp + ilp:c0 + dlp + ilp + wp["gl"]]], axis=-1)

    if past is None:
        o_s = _attn_prompt(q.reshape(bsz, t, sw), k_s, v_s, wp["bias2"])
    else:
        cache_k, cache_v, page_table = past
        nh_s = sw // HEAD_DIM
        rows_n = nh_s * t
        q3 = q.reshape(bsz, t, sw)
        rr = jnp.arange(rows_n)[:, None] // t
        cc = jnp.arange(sw)[None, :] // HEAD_DIM
        q_bd = jnp.where(rr == cc, jnp.tile(q3, (1, nh_s, 1)), jnp.zeros((), BF16))
        k_new = _pad_to(k_s.reshape(bsz, t, sw).astype(BF16), 1, LANES)
        v_new = _pad_to(v_s.reshape(bsz, t, sw).astype(BF16), 1, LANES)
        bias_rows = jnp.broadcast_to(jnp.repeat(wp["sb_bias"].astype(F32), t)[:, None], (rows_n, LANES))
        o_s = _attn_paged(q_bd, k_new, v_new, bias_rows,
                          _pages_transposed(cache_k), _pages_transposed(cache_v), page_table, t)
    o_s = o_s.reshape(m, sw)

    h, hn = _merge(y_r, o_s, gates, x, wp["w_br_r"], wp["w_br_s"], wp["w_o"], wp["g_ffn"])
    ff = wp["w_down"].shape[0]
    nprev = conv_prev.shape[1]
    pad_rows = jnp.zeros((bsz, SUBLANES - nprev, ff), F32)
    if t == SUBLANES:
        prev8 = jnp.concatenate([conv_prev, pad_rows], axis=1).reshape(m, ff)
    else:
        prev8 = jnp.concatenate([pad_rows, conv_prev], axis=1)
    ffn_out, tail = _ffn(hn, wp["w_up"], wp["conv_w"], wp["w_down"], prev8, t)
    conv_new = tail[:, SUBLANES - nprev:]
    y = _ple_out(h, ffn_out, pe3.reshape(m, -1), wp["w_ple"], wp["w_pg"], wp["g_ple"],
                 g_final.reshape(1, d), final_norm)
    return (y.reshape(bsz, t, d), shift_new, wkv_new, conv_new,
            k_out, v_out)


def kernel(x_prompt, x_sample, state_shift, state_wkv, state_conv, cache_k, cache_v, page_table, p_prompt, p_sample, g_mix, w_in, sb_bias, mu_shift, w0, w2, a0, a2, g2, k_k, k_a, r_k, ln_x_g, ln_x_b, w_br_r, w_br_s, w_o, g_ffn, w_up, conv_w, w_down, g_ple, w_ple, w_pg, g_final):
    depth = w_in.shape[0]
    bsz = x_prompt.shape[0]
    rw = w0.shape[1]
    nh = rw // HEAD_DIM
    ff = w_down.shape[1]
    rc = state_shift.shape[-1]
    nprev = state_conv.shape[2]
    h_p, h_s = x_prompt, x_sample
    outs_p = [[] for _ in range(5)]
    outs_s = [[] for _ in range(5)]
    for i in range(depth):
        wp = _prep_weights(g_mix[i], w_in[i], mu_shift[i], w0[i], w2[i], a0[i], a2[i], g2[i], k_k[i],
                           k_a[i], r_k[i], ln_x_g[i], ln_x_b[i], w_br_r[i], w_br_s[i], w_o[i],
                           g_ffn[i], w_up[i], conv_w[i], w_down[i], g_ple[i], w_ple[i], w_pg[i],
                           sb_bias[i])
        last = i == depth - 1
        res_p = _layer(h_p, p_prompt[i], jnp.zeros((bsz, rc), F32),
                       jnp.zeros((bsz, nh, HEAD_DIM, HEAD_DIM), F32),
                       jnp.zeros((bsz, nprev, ff), F32), None, wp, g_final, last)
        res_s = _layer(h_s, p_sample[i], state_shift[i], state_wkv[i], state_conv[i],
                       (cache_k[i], cache_v[i], page_table), wp, g_final, last)
        h_p, h_s = res_p[0], res_s[0]
        for dst, res in ((outs_p, res_p), (outs_s, res_s)):
            for lst, val in zip(dst, res[1:]):
                lst.append(val)
    return (h_p, h_s, *(jnp.stack(o) for o in outs_p), *(jnp.stack(o) for o in outs_s))
```
